```python
import math
import jax, jax.numpy as jnp
from jax import lax
import numpy as np

D_MODEL = 1024
BATCH = 32
SEQ = 256
DEPTH = 4
DEC_BATCH = 4
DEC_SEQ = 2048
PAST_LEN = 256

GRID_W = 64
BLOCK = 128
WINDOW = 128
ROPE_BASE = 10000.0
EPS = 1e-6
DA_HEADS = 4
DA_QK = 32
DA_V = 2 * DA_QK
SW_HEADS = 8
SW_KV_HEADS = 2
SW_DIM = 64
MLA_HEADS = 4
MLA_Q_RANK = 256
MLA_KV_RANK = 128
MLA_NOPE = 64
MLA_ROPE = 32
MLA_V = 64
D_FF = 4 * D_MODEL
IN_SPLITS = (DA_HEADS * 2 * DA_QK, DA_HEADS * 2 * DA_QK, DA_HEADS * DA_V,
             SW_HEADS * SW_DIM, SW_KV_HEADS * SW_DIM, SW_KV_HEADS * SW_DIM,
             MLA_Q_RANK, MLA_KV_RANK + MLA_ROPE)
P_IN = sum(IN_SPLITS)
MIX_WIDTH = DA_HEADS * DA_V + SW_HEADS * SW_DIM + MLA_HEADS * MLA_V

kernel_name = "hybrid_diffusion_prefix_step"


def rmsnorm(x, g):
    xf = x.astype(jnp.float32)
    y = xf * lax.rsqrt(jnp.mean(xf * xf, axis=-1, keepdims=True) + EPS)
    return (y * g.astype(jnp.float32)).astype(x.dtype)


def rope_1d(x, pos):
    m = x.shape[-1]
    half = m // 2
    freqs = ROPE_BASE ** (-jnp.arange(half, dtype=jnp.float32) / half)
    ang = pos.astype(jnp.float32)[:, None] * freqs[None, :]
    bshape = (ang.shape[0],) + (1,) * (x.ndim - 3) + (half,)
    cos = jnp.cos(ang).reshape(bshape)
    sin = jnp.sin(ang).reshape(bshape)
    xf = x.astype(jnp.float32)
    x1, x2 = xf[..., :half], xf[..., half:]
    return jnp.concatenate([x1 * cos - x2 * sin, x1 * sin + x2 * cos], axis=-1).astype(x.dtype)


def axial_rope(x):
    S = x.shape[1]
    n_rows = S // GRID_W
    t = jnp.arange(n_rows * GRID_W)
    rows, cols = t // GRID_W, t % GRID_W
    m = x.shape[-1] // 2
    return jnp.concatenate([rope_1d(x[..., :m], rows), rope_1d(x[..., m:], cols)], axis=-1)


def _split_blocks(q):
    B, S = q.shape[:2]
    return jnp.moveaxis(q.reshape((B, S // BLOCK, BLOCK) + q.shape[2:]), 1, 0)


def _merge_blocks(o):
    nb, B = o.shape[:2]
    return jnp.moveaxis(o, 0, 1).reshape((B, nb * BLOCK) + o.shape[3:])


def gqa_attention(q, k, v, sink=None):
    B, S, H, d = q.shape
    G = k.shape[2]
    R = H // G
    scale = d ** -0.5
    qb = _split_blocks(q.reshape(B, S, G, R, d))

    def one_block(qi):
        s = jnp.einsum('bqgrd,btgd->bgrqt', qi, k).astype(jnp.float32) * scale
        if sink is not None:
            s_sink = jnp.broadcast_to(sink.reshape(1, G, R, 1, 1).astype(jnp.float32), s.shape[:-1] + (1,))
            s = jnp.concatenate([s, s_sink], axis=-1)
        p = jax.nn.softmax(s, axis=-1)
        if sink is not None:
            p = p[..., :-1]
        return jnp.einsum('bgrqt,btgv->bqgrv', p.astype(v.dtype), v)

    o = lax.map(one_block, qb)
    return _merge_blocks(o).reshape(B, S, H, v.shape[-1])


def diff_attention(q1, q2, k1, k2, v, lam):
    scale = q1.shape[-1] ** -0.5

    def one_block(qs):
        a, b = qs
        p1 = jax.nn.softmax(jnp.einsum('bqhd,bthd->bhqt', a, k1).astype(jnp.float32) * scale, axis=-1)
        p2 = jax.nn.softmax(jnp.einsum('bqhd,bthd->bhqt', b, k2).astype(jnp.float32) * scale, axis=-1)
        return jnp.einsum('bhqt,bthv->bqhv', (p1 - lam * p2).astype(v.dtype), v)

    o = lax.map(one_block, (_split_blocks(q1), _split_blocks(q2)))
    return _merge_blocks(o)


def swa_latent_attention(q, k, v, ck, cv, sink):
    B, S, H, d = q.shape
    G = k.shape[2]
    R = H // G
    T = ck.shape[1]
    nb = S // BLOCK
    scale = d ** -0.5
    qb = q.reshape(B, nb, BLOCK, G, R, d)
    pad = ((0, 0), (BLOCK, BLOCK), (0, 0), (0, 0))
    kp, vp = jnp.pad(k, pad), jnp.pad(v, pad)
    idx = jnp.arange(nb)[:, None] * BLOCK + jnp.arange(3 * BLOCK)[None, :]
    kb, vb = kp[:, idx], vp[:, idx]
    qpos = jnp.arange(nb)[:, None] * BLOCK + jnp.arange(BLOCK)[None, :]
    kpos = idx - BLOCK
    valid = ((kpos[:, None, :] >= 0) & (kpos[:, None, :] < S)
             & (jnp.abs(qpos[:, :, None] - kpos[:, None, :]) <= WINDOW))
    s_band = jnp.einsum('bnqgrd,bnkgd->bngrqk', qb, kb).astype(jnp.float32) * scale
    s_band = jnp.where(valid[None, :, None, None], s_band, -1e30)
    s_ctx = jnp.einsum('bnqgrd,btgd->bngrqt', qb, ck).astype(jnp.float32) * scale
    s_sink = jnp.broadcast_to(sink.reshape(1, 1, G, R, 1, 1).astype(jnp.float32), s_ctx.shape[:-1] + (1,))
    p = jax.nn.softmax(jnp.concatenate([s_ctx, s_band, s_sink], axis=-1), axis=-1).astype(v.dtype)
    o = (jnp.einsum('bngrqt,btgv->bnqgrv', p[..., :T], cv)
         + jnp.einsum('bngrqk,bnkgv->bnqgrv', p[..., T:T + 3 * BLOCK], vb))
    return o.reshape(B, S, H, v.shape[-1])


def mla_kv(ckv, krope, w_kvb):
    B, T, _ = ckv.shape
    kv = (ckv @ w_kvb).reshape(B, T, MLA_HEADS, MLA_NOPE + MLA_V)
    k = jnp.concatenate([kv[..., :MLA_NOPE],
                         jnp.broadcast_to(krope[:, :, None, :], (B, T, MLA_HEADS, MLA_ROPE))], axis=-1)
    return k, kv[..., MLA_NOPE:]


def trunk_layer(x, cvec, lw, l, ctx_cache):
    B, S, _ = x.shape
    sh1, sc1, g1, sh2, sc2, g2 = jnp.split(jax.nn.silu(cvec) @ lw['w_ada'] + lw['b_ada'], 6, axis=-1)
    h = rmsnorm(x, lw['norm1_g']) * (1 + sc1) + sh1
    offs = [int(o) for o in np.cumsum(IN_SPLITS)[:-1]]
    da_q, da_k, da_v, sw_q, sw_k, sw_v, q_a, kv_a = jnp.split(h @ lw['w_in'], offs, axis=-1)
    da_q = da_q.reshape(B, S, DA_HEADS, 2, DA_QK)
    da_k = da_k.reshape(B, S, DA_HEADS, 2 * DA_QK)
    da_v = da_v.reshape(B, S, DA_HEADS, DA_V)
    sw_q = sw_q.reshape(B, S, SW_HEADS, SW_DIM)
    sw_k = sw_k.reshape(B, S, SW_KV_HEADS, SW_DIM)
    sw_v = sw_v.reshape(B, S, SW_KV_HEADS, SW_DIM)
    mla_q = (rmsnorm(q_a, lw['mla_q_norm_g']) @ lw['mla_w_qb']).reshape(B, S, MLA_HEADS, MLA_NOPE + MLA_ROPE)
    ckv = rmsnorm(kv_a[..., :MLA_KV_RANK], lw['mla_kv_norm_g'])
    krope = kv_a[..., MLA_KV_RANK:]
    lam_init = 0.8 - 0.6 * math.exp(-0.3 * l)
    lam = (jnp.exp(jnp.sum(lw['lq1'] * lw['lk1'])) - jnp.exp(jnp.sum(lw['lq2'] * lw['lk2']))).astype(jnp.float32) + lam_init
    if ctx_cache is None:
        state = (da_k, da_v, sw_k, sw_v, ckv, krope)
        dk = da_k.reshape(B, S, DA_HEADS, 2, DA_QK)
        da_o = diff_attention(da_q[..., 0, :], da_q[..., 1, :], dk[..., 0, :], dk[..., 1, :], da_v, lam)
        sw_o = gqa_attention(sw_q, sw_k, sw_v, lw['swa_sink'])
        mk, mv = mla_kv(ckv, krope, lw['mla_w_kvb'])
        mla_o = gqa_attention(mla_q, mk, mv)
    else:
        c_dk, c_dv, c_sk, c_sv, c_ckv, c_kr = ctx_cache
        T = c_dk.shape[1]
        da_q = axial_rope(da_q)
        dk = jnp.concatenate([c_dk.reshape(B, T, DA_HEADS, 2, DA_QK),
                              axial_rope(da_k.reshape(B, S, DA_HEADS, 2, DA_QK))], axis=1)
        dv = jnp.concatenate([c_dv, da_v], axis=1)
        da_o = diff_attention(da_q[..., 0, :], da_q[..., 1, :], dk[..., 0, :], dk[..., 1, :], dv, lam)
        sw_o = swa_latent_attention(axial_rope(sw_q), axial_rope(sw_k), sw_v, c_sk, c_sv, lw['swa_sink'])
        mla_q = jnp.concatenate([mla_q[..., :MLA_NOPE], axial_rope(mla_q[..., MLA_NOPE:])], axis=-1)
        kr_lat = axial_rope(krope[:, :, None, :])[:, :, 0, :]
        mk_c, mv_c = mla_kv(c_ckv, c_kr, lw['mla_w_kvb'])
        mk_l, mv_l = mla_kv(ckv, kr_lat, lw['mla_w_kvb'])
        mla_o = gqa_attention(mla_q, jnp.concatenate([mk_c, mk_l], axis=1), jnp.concatenate([mv_c, mv_l], axis=1))
        state = None
    da_o = rmsnorm(da_o, lw['diff_subln_g']) * (1.0 - lam_init)
    mix = jnp.concatenate([da_o.reshape(B, S, -1), sw_o.reshape(B, S, -1), mla_o.reshape(B, S, -1)], axis=-1)
    x = x + g1 * (mix @ lw['w_out'])
    h2 = rmsnorm(x, lw['norm2_g']) * (1 + sc2) + sh2
    x = x + g2 * (jnp.square(jax.nn.relu(h2 @ lw['w_up'])) @ lw['w_down'])
    return x, state


def setup_inputs(seed: int = 0) -> dict:
    key = jax.random.key(seed)
    ks = iter(jax.random.split(key, 40))

    def nrm(shape, scale=1.0):
        return jax.random.normal(next(ks), shape, jnp.float32) * scale

    def gain(shape):
        return 1.0 + 0.05 * nrm(shape)

    return {
        'x_prompt': nrm((BATCH, SEQ, D_MODEL)),
        'x_sample': nrm((DEC_BATCH, DEC_SEQ, D_MODEL)),
        'c': nrm((DEC_BATCH, D_MODEL)),
        'cache_diff_k': nrm((DEC_BATCH, DEPTH, PAST_LEN, DA_HEADS, 2 * DA_QK)),
        'cache_diff_v': nrm((DEC_BATCH, DEPTH, PAST_LEN, DA_HEADS, DA_V)),
        'cache_swa_k': nrm((DEC_BATCH, DEPTH, PAST_LEN, SW_KV_HEADS, SW_DIM)),
        'cache_swa_v': nrm((DEC_BATCH, DEPTH, PAST_LEN, SW_KV_HEADS, SW_DIM)),
        'cache_mla_ckv': nrm((DEC_BATCH, DEPTH, PAST_LEN, MLA_KV_RANK)),
        'cache_mla_krope': nrm((DEC_BATCH, DEPTH, PAST_LEN, MLA_ROPE)),
        'c_ctx': nrm((D_MODEL,)),
        'norm1_g': gain((DEPTH, D_MODEL)),
        'norm2_g': gain((DEPTH, D_MODEL)),
        'w_ada': nrm((DEPTH, D_MODEL, 6 * D_MODEL), D_MODEL ** -0.5),
        'b_ada': nrm((DEPTH, 6 * D_MODEL), 0.02),
        'w_in': nrm((DEPTH, D_MODEL, P_IN), D_MODEL ** -0.5),
        'w_out': nrm((DEPTH, MIX_WIDTH, D_MODEL), MIX_WIDTH ** -0.5),
        'diff_lambda_q1': nrm((DEPTH, DA_QK), 0.1),
        'diff_lambda_k1': nrm((DEPTH, DA_QK), 0.1),
        'diff_lambda_q2': nrm((DEPTH, DA_QK), 0.1),
        'diff_lambda_k2': nrm((DEPTH, DA_QK), 0.1),
        'diff_subln_g': gain((DEPTH, DA_V)),
        'swa_sink': nrm((DEPTH, SW_HEADS), 0.5),
        'mla_q_norm_g': gain((DEPTH, MLA_Q_RANK)),
        'mla_w_qb': nrm((DEPTH, MLA_Q_RANK, MLA_HEADS * (MLA_NOPE + MLA_ROPE)), MLA_Q_RANK ** -0.5),
        'mla_kv_norm_g': gain((DEPTH, MLA_KV_RANK)),
        'mla_w_kvb': nrm((DEPTH, MLA_KV_RANK, MLA_HEADS * (MLA_NOPE + MLA_V)), MLA_KV_RANK ** -0.5),
        'w_up': nrm((DEPTH, D_MODEL, D_FF), D_MODEL ** -0.5),
        'w_down': nrm((DEPTH, D_FF, D_MODEL), D_FF ** -0.5),
        'final_g': gain((D_MODEL,)),
    }


def reference(x_prompt, x_sample, c, cache_diff_k, cache_diff_v, cache_swa_k, cache_swa_v,
              cache_mla_ckv, cache_mla_krope, c_ctx, norm1_g, norm2_g, w_ada, b_ada, w_in, w_out,
              diff_lambda_q1, diff_lambda_k1, diff_lambda_q2, diff_lambda_k2, diff_subln_g, swa_sink,
              mla_q_norm_g, mla_w_qb, mla_kv_norm_g, mla_w_kvb, w_up, w_down, final_g):
    x_ctx, x_lat = x_prompt, x_sample
    c_ctx_vec = c_ctx[None, None, :]
    c_lat = c[:, None, :]
    new_dk, new_dv, new_sk, new_sv, new_ckv, new_kr = [], [], [], [], [], []
    for l in range(DEPTH):
        lw = {
            'norm1_g': norm1_g[l], 'norm2_g': norm2_g[l], 'w_ada': w_ada[l], 'b_ada': b_ada[l],
            'w_in': w_in[l], 'w_out': w_out[l],
            'lq1': diff_lambda_q1[l], 'lk1': diff_lambda_k1[l], 'lq2': diff_lambda_q2[l], 'lk2': diff_lambda_k2[l],
            'diff_subln_g': diff_subln_g[l], 'swa_sink': swa_sink[l],
            'mla_q_norm_g': mla_q_norm_g[l], 'mla_w_qb': mla_w_qb[l],
            'mla_kv_norm_g': mla_kv_norm_g[l], 'mla_w_kvb': mla_w_kvb[l],
            'w_up': w_up[l], 'w_down': w_down[l],
        }
        x_ctx, st = trunk_layer(x_ctx, c_ctx_vec, lw, l, None)
        new_dk.append(st[0]); new_dv.append(st[1]); new_sk.append(st[2])
        new_sv.append(st[3]); new_ckv.append(st[4]); new_kr.append(st[5])
        cache_l = (cache_diff_k[:, l], cache_diff_v[:, l], cache_swa_k[:, l], cache_swa_v[:, l],
                   cache_mla_ckv[:, l], cache_mla_krope[:, l])
        x_lat, _ = trunk_layer(x_lat, c_lat, lw, l, cache_l)
    y_prompt = rmsnorm(x_ctx, final_g)
    y_sample = rmsnorm(x_lat, final_g)
    new_diff_k = jnp.stack(new_dk, axis=1)
    new_diff_v = jnp.stack(new_dv, axis=1)
    new_swa_k = jnp.stack(new_sk, axis=1)
    new_swa_v = jnp.stack(new_sv, axis=1)
    new_mla_ckv = jnp.stack(new_ckv, axis=1)
    new_mla_krope = jnp.stack(new_kr, axis=1)
    return (y_prompt, y_sample, new_diff_k, new_diff_v, new_swa_k, new_swa_v, new_mla_ckv, new_mla_krope)
```

```python
import functools
import math

import numpy as np
import jax
import jax.numpy as jnp
from jax import lax
from jax.experimental import pallas as pl
from jax.experimental.pallas import tpu as pltpu

F32 = jnp.float32
BF16 = jnp.bfloat16

D_MODEL = 1024
BATCH = 32
SEQ = 256
DEPTH = 4
DEC_BATCH = 4
DEC_SEQ = 2048
PAST_LEN = 256
GRID_W = 64
WINDOW = 128
ROPE_BASE = 10000.0
EPS = 1e-6
DA_HEADS = 4
DA_QK = 32
DA_V = 64
SW_HEADS = 8
SW_KV_HEADS = 2
SW_DIM = 64
MLA_HEADS = 4
MLA_Q_RANK = 256
MLA_KV_RANK = 128
MLA_NOPE = 64
MLA_ROPE = 32
MLA_V = 64
D_FF = 4 * D_MODEL

LANES = 128
N_CTX_TOK = BATCH * SEQ
N_LAT_TOK = DEC_BATCH * DEC_SEQ
N_KEYS = PAST_LEN + DEC_SEQ
CTX_MOD_ROW = DEC_BATCH

C_DAQ, C_DAK, C_DAV, C_SWQ, C_SWK, C_SWV, C_QA, C_CKV, C_KR, P_EXT = (
    0, 256, 512, 768, 1280, 1408, 1536, 1792, 1920, 2048)
Q_W = 256 + 512 + 384
K_W = 256 + 128 + 384
V_W = 256 + 128 + 256

DA_SCALE = DA_QK ** -0.5
SW_SCALE = SW_DIM ** -0.5
MLA_SCALE = (MLA_NOPE + MLA_ROPE) ** -0.5

TM_CTX = SEQ
TM_LAT = 512
TM_MLP = 512
TQ = 256
SW_BAND = TQ + 2 * WINDOW
FF_CHUNK = 1024
MOD_TILE = 1536
VMEM_LIMIT = 56 * 1024 * 1024

_NT = (((1,), (1,)), ((), ()))

S_LAM_INIT = SW_HEADS
S_ONE_MINUS = SW_HEADS + 1
S_COLS = 16


def _rms(x, g):
    ms = jnp.mean(x * x, axis=-1, keepdims=True)
    return x * lax.rsqrt(ms + EPS) * g


def _lane_mask(n, lo, width):
    lane = lax.broadcasted_iota(jnp.int32, (1, n), 1)
    return (lane >= lo) & (lane < lo + width)


def _softmax(s, sink=None):
    m = jnp.max(s, axis=-1, keepdims=True)
    if sink is not None:
        m = jnp.maximum(m, sink)
    e = jnp.exp(s - m)
    d = jnp.sum(e, axis=-1, keepdims=True)
    if sink is not None:
        d = d + jnp.exp(sink - m)
    return e * (1.0 / d)


def _project(x_ref, mod_ref, n1g_ref, w_in_ref):
    m = mod_ref[0, 0]
    h = _rms(x_ref[...], n1g_ref[0]) * (1.0 + m[1:2]) + m[0:1]
    return jnp.dot(h.astype(BF16), w_in_ref[0], preferred_element_type=F32)


def _diff_lambda(lamv_ref, lam_init):
    lv = lamv_ref[0]
    a = jnp.sum(lv[0:1] * lv[1:2], axis=-1, keepdims=True)
    b = jnp.sum(lv[2:3] * lv[3:4], axis=-1, keepdims=True)
    return jnp.exp(a) - jnp.exp(b) + lam_init


def _diff_attention(qf, load_k, load_v, lam):
    width = DA_HEADS * DA_V

    def body(h, acc):
        lo = h * DA_V
        q1 = jnp.where(_lane_mask(width, lo, DA_QK), qf, 0.0).astype(BF16)
        q2 = jnp.where(_lane_mask(width, lo + DA_QK, DA_QK), qf, 0.0).astype(BF16)
        k = load_k()
        p1 = _softmax(lax.dot_general(q1, k, _NT, preferred_element_type=F32) * DA_SCALE)
        p2 = _softmax(lax.dot_general(q2, k, _NT, preferred_element_type=F32) * DA_SCALE)
        w = (p1 - lam * p2).astype(BF16)
        o = jnp.dot(w, load_v(), preferred_element_type=F32)
        return acc + jnp.where(_lane_mask(width, lo, DA_V), o, 0.0)

    return lax.fori_loop(0, DA_HEADS, body, jnp.zeros((qf.shape[0], width), F32))


def _diff_subln(o, subg, one_minus_lam_init):
    sq = o * o
    r = jnp.zeros_like(o)
    for h in range(DA_HEADS):
        mh = _lane_mask(DA_HEADS * DA_V, h * DA_V, DA_V)
        ms = jnp.sum(jnp.where(mh, sq, 0.0), axis=-1, keepdims=True) * (1.0 / DA_V)
        r = jnp.where(mh, lax.rsqrt(ms + EPS), r)
    return o * r * subg * one_minus_lam_init


def _swa_attention(qf, k, v, sinks, valid=None):
    lane = lax.broadcasted_iota(jnp.int32, (1, LANES), 1)
    outs = []
    for c in range(SW_HEADS // 2):
        qc = qf[:, c * LANES:(c + 1) * LANES]
        o_par = []
        for par in range(2):
            h = c + (SW_HEADS // 2) * par
            qm = jnp.where((lane >= par * SW_DIM) & (lane < (par + 1) * SW_DIM), qc, 0.0).astype(BF16)
            s = lax.dot_general(qm, k, _NT, preferred_element_type=F32) * SW_SCALE
            if valid is not None:
                s = jnp.where(valid, s, -1e30)
            p = _softmax(s, sink=sinks[h])
            o_par.append(jnp.dot(p.astype(BF16), v, preferred_element_type=F32))
        outs.append(jnp.where(lane < SW_DIM, o_par[0], o_par[1]))
    return outs


def _mla_attention(qf, load_k, load_v):
    n_q = MLA_HEADS * (MLA_NOPE + MLA_ROPE)
    n_nope = MLA_HEADS * MLA_NOPE
    width = MLA_HEADS * MLA_V

    def body(h, acc):
        m = _lane_mask(n_q, h * MLA_NOPE, MLA_NOPE) | _lane_mask(n_q, n_nope + h * MLA_ROPE, MLA_ROPE)
        qm = jnp.where(m, qf, 0.0).astype(BF16)
        p = _softmax(lax.dot_general(qm, load_k(), _NT, preferred_element_type=F32) * MLA_SCALE)
        o = jnp.dot(p.astype(BF16), load_v(), preferred_element_type=F32)
        return acc + jnp.where(_lane_mask(width, h * MLA_V, MLA_V), o, 0.0)

    return lax.fori_loop(0, MLA_HEADS, body, jnp.zeros((qf.shape[0], width), F32))


def _mod_kernel(c_ref, w_ref, b_ref, o_ref):
    c = c_ref[...]
    s = c * (1.0 / (1.0 + jnp.exp(-c)))
    o_ref[0] = jnp.dot(s.astype(BF16), w_ref[0].astype(BF16), preferred_element_type=F32) + b_ref[0]


def _modulation(cvecs, w_ada, b_ada):
    n_out = 6 * D_MODEL
    out = pl.pallas_call(
        _mod_kernel,
        grid=(DEPTH, n_out // MOD_TILE),
        in_specs=[
            pl.BlockSpec((8, D_MODEL), lambda l, j: (0, 0)),
            pl.BlockSpec((1, D_MODEL, MOD_TILE), lambda l, j: (l, 0, j)),
            pl.BlockSpec((1, 1, MOD_TILE), lambda l, j: (l, 0, j)),
        ],
        out_specs=pl.BlockSpec((1, 8, MOD_TILE), lambda l, j: (l, 0, j)),
        out_shape=jax.ShapeDtypeStruct((DEPTH, 8, n_out), F32),
        compiler_params=pltpu.CompilerParams(
            dimension_semantics=("arbitrary", "arbitrary"), vmem_limit_bytes=VMEM_LIMIT),
        name="adaln_modulation",
    )(cvecs, w_ada, b_ada.reshape(DEPTH, 1, n_out))
    return out.reshape(DEPTH, 8, 6, D_MODEL)


def _ctx_kernel(l_ref, x_ref, mod_ref, n1g_ref, w_in_ref, qg_ref, wqb_ref, kvg_ref, wkvb_ref,
                lamv_ref, subg_ref, scal_ref,
                dk_ref, dv_ref, sk_ref, sv_ref, ckv_ref, kr_ref, mix_ref):
    l = l_ref[0]
    proj = _project(x_ref, mod_ref, n1g_ref, w_in_ref)
    da_k = proj[:, C_DAK:C_DAV]
    da_v = proj[:, C_DAV:C_SWQ]
    sw_k = proj[:, C_SWK:C_SWV]
    sw_v = proj[:, C_SWV:C_QA]
    ckv = _rms(proj[:, C_CKV:C_KR], kvg_ref[0])
    kr4 = proj[:, C_KR:P_EXT]
    dk_ref[0] = da_k
    dv_ref[0] = da_v
    sk_ref[0] = sw_k
    sv_ref[0] = sw_v
    ckv_ref[0] = ckv
    kr_ref[0] = kr4[:, 0:MLA_ROPE]

    lam = _diff_lambda(lamv_ref, scal_ref[l, S_LAM_INIT])
    da_kb = da_k.astype(BF16)
    da_vb = da_v.astype(BF16)
    da_o = _diff_attention(proj[:, C_DAQ:C_DAK], lambda: da_kb, lambda: da_vb, lam)
    mix_ref[:, 0:256] = _diff_subln(da_o, subg_ref[0], scal_ref[l, S_ONE_MINUS]).astype(BF16)

    sinks = [scal_ref[l, h] for h in range(SW_HEADS)]
    sw_o = _swa_attention(proj[:, C_SWQ:C_SWK], sw_k.astype(BF16), sw_v.astype(BF16), sinks)
    for c, o in enumerate(sw_o):
        mix_ref[:, 256 + c * LANES:256 + (c + 1) * LANES] = o.astype(BF16)

    mq = jnp.dot(_rms(proj[:, C_QA:C_CKV], qg_ref[0]).astype(BF16), wqb_ref[0], preferred_element_type=F32)
    kv = jnp.dot(ckv.astype(BF16), wkvb_ref[0], preferred_element_type=F32)
    mk = jnp.concatenate([kv[:, 0:256], kr4], axis=-1).astype(BF16)
    mv = kv[:, 256:512].astype(BF16)
    mla_o = _mla_attention(mq, lambda: mk, lambda: mv)
    mix_ref[:, 768:1024] = mla_o.astype(BF16)


def _layer_spec(shape):
    nd = len(shape)
    return pl.BlockSpec((1,) + tuple(shape[1:]), lambda *a: (a[-1][0],) + (0,) * (nd - 1))


def _ctx_layer(l_idx, x, mod, p):
    n_req = BATCH
    cache_shapes = [DA_HEADS * 2 * DA_QK, DA_HEADS * DA_V, SW_KV_HEADS * SW_DIM, SW_KV_HEADS * SW_DIM,
                    MLA_KV_RANK, MLA_ROPE]
    out_shape = [jax.ShapeDtypeStruct((n_req, SEQ, w), F32) for w in cache_shapes]
    out_shape.append(jax.ShapeDtypeStruct((N_CTX_TOK, D_MODEL), BF16))
    out_specs = [pl.BlockSpec((1, SEQ, w), lambda i, l: (i, 0, 0)) for w in cache_shapes]
    out_specs.append(pl.BlockSpec((TM_CTX, D_MODEL), lambda i, l: (i, 0)))
    grid_spec = pltpu.PrefetchScalarGridSpec(
        num_scalar_prefetch=1,
        grid=(n_req,),
        in_specs=[
            pl.BlockSpec((TM_CTX, D_MODEL), lambda i, l: (i, 0)),
            pl.BlockSpec((1, 1, 6, D_MODEL), lambda i, l: (l[0], CTX_MOD_ROW, 0, 0)),
            _layer_spec(p["norm1_g"].shape),
            _layer_spec(p["w_in"].shape),
            _layer_spec(p["mla_q_norm_g"].shape),
            _layer_spec(p["mla_w_qb"].shape),
            _layer_spec(p["mla_kv_norm_g"].shape),
            _layer_spec(p["mla_w_kvb"].shape),
            _layer_spec(p["lamv"].shape),
            _layer_spec(p["subg"].shape),
            pl.BlockSpec(memory_space=pltpu.SMEM),
        ],
        out_specs=out_specs,
    )
    return pl.pallas_call(
        _ctx_kernel,
        grid_spec=grid_spec,
        out_shape=out_shape,
        compiler_params=pltpu.CompilerParams(
            dimension_semantics=("arbitrary",), vmem_limit_bytes=VMEM_LIMIT),
        name="ctx_project_attend",
    )(l_idx, x, mod, p["norm1_g"], p["w_in"], p["mla_q_norm_g"], p["mla_w_qb"], p["mla_kv_norm_g"],
      p["mla_w_kvb"], p["lamv"], p["subg"], p["scal"])


def _rope(x, cos, sa, sb, half):
    return x * cos + pltpu.roll(x, LANES - half, 1) * sa + pltpu.roll(x, half, 1) * sb


def _lat_proj_kernel(l_ref, x_ref, mod_ref, n1g_ref, w_in_ref, qg_ref, wqb_ref, kvg_ref, wkvb_ref,
                     c32_ref, a32_ref, b32_ref, c64_ref, a64_ref, b64_ref,
                     q_ref, k_ref, v_ref):
    proj = _project(x_ref, mod_ref, n1g_ref, w_in_ref)
    t32 = (c32_ref[...], a32_ref[...], b32_ref[...], DA_QK // 4)
    t64 = (c64_ref[...], a64_ref[...], b64_ref[...], SW_DIM // 4)

    def chunk(a, j):
        return a[:, j * LANES:(j + 1) * LANES]

    for j in range(2):
        q_ref[:, j * LANES:(j + 1) * LANES] = _rope(chunk(proj, j), *t32).astype(BF16)
    for j in range(4):
        q_ref[:, 256 + j * LANES:256 + (j + 1) * LANES] = _rope(chunk(proj, C_SWQ // LANES + j), *t64).astype(BF16)
    mq = jnp.dot(_rms(proj[:, C_QA:C_CKV], qg_ref[0]).astype(BF16), wqb_ref[0], preferred_element_type=F32)
    q_ref[:, 768:1024] = mq[:, 0:256].astype(BF16)
    q_ref[:, 1024:1152] = _rope(mq[:, 256:384], *t32).astype(BF16)

    for j in range(2):
        k_ref[:, j * LANES:(j + 1) * LANES] = _rope(chunk(proj, C_DAK // LANES + j), *t32).astype(BF16)
    k_ref[:, 256:384] = _rope(proj[:, C_SWK:C_SWV], *t64).astype(BF16)
    ckv = _rms(proj[:, C_CKV:C_KR], kvg_ref[0])
    kv = jnp.dot(ckv.astype(BF16), wkvb_ref[0], preferred_element_type=F32)
    k_ref[:, 384:640] = kv[:, 0:256].astype(BF16)
    k_ref[:, 640:768] = _rope(proj[:, C_KR:P_EXT], *t32).astype(BF16)

    v_ref[:, 0:256] = proj[:, C_DAV:C_SWQ].astype(BF16)
    v_ref[:, 256:384] = proj[:, C_SWV:C_QA].astype(BF16)
    v_ref[:, 384:640] = kv[:, 256:512].astype(BF16)


def _lat_project(l_idx, x, mod, p, tables):
    tiles_per_req = DEC_SEQ // TM_LAT
    tab_spec = pl.BlockSpec((TM_LAT, LANES), lambda i, l: (i % tiles_per_req, 0))
    grid_spec = pltpu.PrefetchScalarGridSpec(
        num_scalar_prefetch=1,
        grid=(N_LAT_TOK // TM_LAT,),
        in_specs=[
            pl.BlockSpec((TM_LAT, D_MODEL), lambda i, l: (i, 0)),
            pl.BlockSpec((1, 1, 6, D_MODEL), lambda i, l: (l[0], i // tiles_per_req, 0, 0)),
            _layer_spec(p["norm1_g"].shape),
            _layer_spec(p["w_in"].shape),
            _layer_spec(p["mla_q_norm_g"].shape),
            _layer_spec(p["mla_w_qb"].shape),
            _layer_spec(p["mla_kv_norm_g"].shape),
            _layer_spec(p["mla_w_kvb"].shape),
        ] + [tab_spec] * 6,
        out_specs=[
            pl.BlockSpec((TM_LAT, Q_W), lambda i, l: (i, 0)),
            pl.BlockSpec((TM_LAT, K_W), lambda i, l: (i, 0)),
            pl.BlockSpec((TM_LAT, V_W), lambda i, l: (i, 0)),
        ],
    )
    return pl.pallas_call(
        _lat_proj_kernel,
        grid_spec=grid_spec,
        out_shape=[jax.ShapeDtypeStruct((N_LAT_TOK, Q_W), BF16),
                   jax.ShapeDtypeStruct((N_LAT_TOK, K_W), BF16),
                   jax.ShapeDtypeStruct((N_LAT_TOK, V_W), BF16)],
        compiler_params=pltpu.CompilerParams(
            dimension_semantics=("arbitrary",), vmem_limit_bytes=VMEM_LIMIT),
        name="latent_project",
    )(l_idx, x, mod, p["norm1_g"], p["w_in"], p["mla_q_norm_g"], p["mla_w_qb"], p["mla_kv_norm_g"],
      p["mla_w_kvb"], *tables)


def _lat_attn_kernel(l_ref, q_ref, k_ref, v_ref, cdk_ref, cdv_ref, csk_ref, csv_ref, cckv_ref, ckr_ref,
                     wkvb_ref, tile4_ref, lamv_ref, subg_ref, scal_ref,
                     mix_ref,
                     kda, vda, ksw, vsw, kml, vml):
    l = l_ref[0]
    qb = pl.program_id(1)

    @pl.when(qb == 0)
    def _stage_keys():
        kda[0:PAST_LEN, :] = cdk_ref[0, 0].astype(BF16)
        kda[PAST_LEN:N_KEYS, :] = k_ref[0, :, 0:256]
        vda[0:PAST_LEN, :] = cdv_ref[0, 0].astype(BF16)
        vda[PAST_LEN:N_KEYS, :] = v_ref[0, :, 0:256]
        ksw[0:PAST_LEN, :] = csk_ref[0, 0].astype(BF16)
        ksw[PAST_LEN:N_KEYS, :] = k_ref[0, :, 256:384]
        vsw[0:PAST_LEN, :] = csv_ref[0, 0].astype(BF16)
        vsw[PAST_LEN:N_KEYS, :] = v_ref[0, :, 256:384]
        kvc = jnp.dot(cckv_ref[0, 0].astype(BF16), wkvb_ref[0], preferred_element_type=F32)
        kml[0:PAST_LEN, 0:256] = kvc[:, 0:256].astype(BF16)
        kml[0:PAST_LEN, 256:384] = jnp.dot(ckr_ref[0, 0].astype(BF16), tile4_ref[...],
                                           preferred_element_type=F32).astype(BF16)
        kml[PAST_LEN:N_KEYS, :] = k_ref[0, :, 384:768]
        vml[0:PAST_LEN, :] = kvc[:, 256:512].astype(BF16)
        vml[PAST_LEN:N_KEYS, :] = v_ref[0, :, 384:640]

    lam = _diff_lambda(lamv_ref, scal_ref[l, S_LAM_INIT])
    da_o = _diff_attention(q_ref[0, :, 0:256].astype(F32), lambda: kda[...], lambda: vda[...], lam)
    mix_ref[0, :, 0:256] = _diff_subln(da_o, subg_ref[0], scal_ref[l, S_ONE_MINUS]).astype(BF16)

    q0 = qb * TQ
    start = pl.multiple_of(jnp.clip(q0 - WINDOW, 0, DEC_SEQ - SW_BAND), LANES)
    band = pl.ds(PAST_LEN + start, SW_BAND)
    k_sel = jnp.concatenate([ksw[0:PAST_LEN, :], ksw[band, :]], axis=0)
    v_sel = jnp.concatenate([vsw[0:PAST_LEN, :], vsw[band, :]], axis=0)
    n_sel = PAST_LEN + SW_BAND
    col = lax.broadcasted_iota(jnp.int32, (TQ, n_sel), 1)
    row = lax.broadcasted_iota(jnp.int32, (TQ, n_sel), 0)
    dist = (q0 + row) - (start + col - PAST_LEN)
    valid = (col < PAST_LEN) | ((dist <= WINDOW) & (dist >= -WINDOW))
    sinks = [scal_ref[l, h] for h in range(SW_HEADS)]
    sw_o = _swa_attention(q_ref[0, :, 256:768].astype(F32), k_sel, v_sel, sinks, valid)
    for c, o in enumerate(sw_o):
        mix_ref[0, :, 256 + c * LANES:256 + (c + 1) * LANES] = o.astype(BF16)

    mla_o = _mla_attention(q_ref[0, :, 768:1152].astype(F32), lambda: kml[...], lambda: vml[...])
    mix_ref[0, :, 768:1024] = mla_o.astype(BF16)


def _lat_attend(l_idx, q, k, v, caches, p, tile4):
    cdk, cdv, csk, csv, cckv, ckr = caches

    def cache_spec(a):
        return pl.BlockSpec((1, 1) + a.shape[2:], lambda b, j, l: (b, l[0], 0, 0))

    grid_spec = pltpu.PrefetchScalarGridSpec(
        num_scalar_prefetch=1,
        grid=(DEC_BATCH, DEC_SEQ // TQ),
        in_specs=[
            pl.BlockSpec((1, TQ, Q_W), lambda b, j, l: (b, j, 0)),
            pl.BlockSpec((1, DEC_SEQ, K_W), lambda b, j, l: (b, 0, 0)),
            pl.BlockSpec((1, DEC_SEQ, V_W), lambda b, j, l: (b, 0, 0)),
            cache_spec(cdk), cache_spec(cdv), cache_spec(csk), cache_spec(csv), cache_spec(cckv), cache_spec(ckr),
            _layer_spec(p["mla_w_kvb"].shape),
            pl.BlockSpec(tile4.shape, lambda b, j, l: (0, 0)),
            _layer_spec(p["lamv"].shape),
            _layer_spec(p["subg"].shape),
            pl.BlockSpec(memory_space=pltpu.SMEM),
        ],
        out_specs=pl.BlockSpec((1, TQ, D_MODEL), lambda b, j, l: (b, j, 0)),
        scratch_shapes=[
            pltpu.VMEM((N_KEYS, 256), BF16), pltpu.VMEM((N_KEYS, 256), BF16),
            pltpu.VMEM((N_KEYS, 128), BF16), pltpu.VMEM((N_KEYS, 128), BF16),
            pltpu.VMEM((N_KEYS, 384), BF16), pltpu.VMEM((N_KEYS, 256), BF16),
        ],
    )
    return pl.pallas_call(
        _lat_attn_kernel,
        grid_spec=grid_spec,
        out_shape=jax.ShapeDtypeStruct((DEC_BATCH, DEC_SEQ, D_MODEL), BF16),
        compiler_params=pltpu.CompilerParams(
            dimension_semantics=("arbitrary", "arbitrary"), vmem_limit_bytes=VMEM_LIMIT),
        name="latent_attend",
    )(l_idx, q.reshape(DEC_BATCH, DEC_SEQ, Q_W), k.reshape(DEC_BATCH, DEC_SEQ, K_W),
      v.reshape(DEC_BATCH, DEC_SEQ, V_W), cdk, cdv, csk, csv, cckv, ckr,
      p["mla_w_kvb"], tile4, p["lamv"], p["subg"], p["scal"])


def _mlp_kernel(l_ref, x_ref, mix_ref, mod_ref, n2g_ref, w_out_ref, w_up_ref, w_down_ref, fg_ref, o_ref,
                *, final_norm):
    m = mod_ref[0, 0]
    x = x_ref[...] + m[2:3] * jnp.dot(mix_ref[...], w_out_ref[0], preferred_element_type=F32)
    h = (_rms(x, n2g_ref[0]) * (1.0 + m[4:5]) + m[3:4]).astype(BF16)
    acc = jnp.zeros_like(x)
    for c in range(D_FF // FF_CHUNK):
        u = jnp.dot(h, w_up_ref[0, :, c * FF_CHUNK:(c + 1) * FF_CHUNK], preferred_element_type=F32)
        a = jnp.square(jnp.maximum(u, 0.0)).astype(BF16)
        acc = acc + jnp.dot(a, w_down_ref[0, c * FF_CHUNK:(c + 1) * FF_CHUNK, :], preferred_element_type=F32)
    x = x + m[5:6] * acc
    if final_norm:
        x = _rms(x, fg_ref[...])
    o_ref[...] = x


def _mlp_layer(l_idx, x, mix, mod, p, final_g, mod_row, final_norm):
    n_tok = x.shape[0]
    grid_spec = pltpu.PrefetchScalarGridSpec(
        num_scalar_prefetch=1,
        grid=(n_tok // TM_MLP,),
        in_specs=[
            pl.BlockSpec((TM_MLP, D_MODEL), lambda i, l: (i, 0)),
            pl.BlockSpec((TM_MLP, D_MODEL), lambda i, l: (i, 0)),
            pl.BlockSpec((1, 1, 6, D_MODEL), lambda i, l: (l[0], mod_row(i), 0, 0)),
            _layer_spec(p["norm2_g"].shape),
            _layer_spec(p["w_out"].shape),
            _layer_spec(p["w_up"].shape),
            _layer_spec(p["w_down"].shape),
            pl.BlockSpec((1, D_MODEL), lambda i, l: (0, 0)),
        ],
        out_specs=pl.BlockSpec((TM_MLP, D_MODEL), lambda i, l: (i, 0)),
    )
    return pl.pallas_call(
        functools.partial(_mlp_kernel, final_norm=final_norm),
        grid_spec=grid_spec,
        out_shape=jax.ShapeDtypeStruct((n_tok, D_MODEL), F32),
        compiler_params=pltpu.CompilerParams(
            dimension_semantics=("arbitrary",), vmem_limit_bytes=VMEM_LIMIT),
        name="out_proj_mlp",
    )(l_idx, x, mix, mod, p["norm2_g"], p["w_out"], p["w_up"], p["w_down"], final_g)


def _rope_tables(period):
    m = period // 2
    half = m // 2
    t = np.arange(DEC_SEQ)
    pos = np.stack([t // GRID_W, t % GRID_W], axis=0).astype(np.float64)
    lane = np.arange(LANES) % period
    axis = lane // m
    k = lane % m
    freq = ROPE_BASE ** (-(k % half).astype(np.float64) / half)
    ang = pos[axis, :].T * freq[None, :]
    first = (k < half)[None, :]
    cos, sin = np.cos(ang), np.sin(ang)
    return (jnp.asarray(cos, F32), jnp.asarray(np.where(first, -sin, 0.0), F32),
            jnp.asarray(np.where(first, 0.0, sin), F32))


def _paired_head_order():
    order = []
    for c in range(SW_HEADS // 2):
        order += [c, c + SW_HEADS // 2]
    return order


def _prepare_params(norm1_g, norm2_g, w_in, w_out, lq1, lk1, lq2, lk2, diff_subln_g, swa_sink,
                    mla_q_norm_g, mla_w_qb, mla_kv_norm_g, mla_w_kvb, w_up, w_down):
    heads = _paired_head_order()
    o_swq = 768
    swq_cols = [w_in[:, :, o_swq + h * SW_DIM:o_swq + (h + 1) * SW_DIM] for h in heads]
    kr_cols = [w_in[:, :, 1920:1952]] * 4
    w_in_ext = jnp.concatenate([w_in[:, :, 0:768]] + swq_cols + [w_in[:, :, 1280:1920]] + kr_cols,
                               axis=-1).astype(BF16)
    o_swo = DA_HEADS * DA_V
    swo_rows = [w_out[:, o_swo + h * SW_DIM:o_swo + (h + 1) * SW_DIM, :] for h in heads]
    w_out_p = jnp.concatenate([w_out[:, 0:o_swo, :]] + swo_rows + [w_out[:, o_swo + SW_HEADS * SW_DIM:, :]],
                              axis=1).astype(BF16)
    dq = MLA_NOPE + MLA_ROPE
    wqb = jnp.concatenate([mla_w_qb[:, :, h * dq:h * dq + MLA_NOPE] for h in range(MLA_HEADS)]
                          + [mla_w_qb[:, :, h * dq + MLA_NOPE:(h + 1) * dq] for h in range(MLA_HEADS)],
                          axis=-1).astype(BF16)
    dkv = MLA_NOPE + MLA_V
    wkvb = jnp.concatenate([mla_w_kvb[:, :, h * dkv:h * dkv + MLA_NOPE] for h in range(MLA_HEADS)]
                           + [mla_w_kvb[:, :, h * dkv + MLA_NOPE:(h + 1) * dkv] for h in range(MLA_HEADS)],
                           axis=-1).astype(BF16)
    lam_init = np.array([0.8 - 0.6 * math.exp(-0.3 * l) for l in range(DEPTH)], np.float32)
    scal = jnp.concatenate([
        swa_sink.astype(F32),
        jnp.asarray(np.stack([lam_init, (1.0 - lam_init.astype(np.float64)).astype(np.float32)], axis=1)),
        jnp.zeros((DEPTH, S_COLS - SW_HEADS - 2), F32)], axis=1)
    return {
        "norm1_g": norm1_g.reshape(DEPTH, 1, D_MODEL),
        "norm2_g": norm2_g.reshape(DEPTH, 1, D_MODEL),
        "w_in": w_in_ext,
        "w_out": w_out_p,
        "mla_q_norm_g": mla_q_norm_g.reshape(DEPTH, 1, MLA_Q_RANK),
        "mla_w_qb": wqb,
        "mla_kv_norm_g": mla_kv_norm_g.reshape(DEPTH, 1, MLA_KV_RANK),
        "mla_w_kvb": wkvb,
        "lamv": jnp.stack([lq1, lk1, lq2, lk2], axis=1),
        "subg": jnp.tile(diff_subln_g, (1, DA_HEADS)).reshape(DEPTH, 1, DA_HEADS * DA_V),
        "scal": scal,
        "w_up": w_up.astype(BF16),
        "w_down": w_down.astype(BF16),
    }


def kernel(x_prompt, x_sample, c, cache_diff_k, cache_diff_v, cache_swa_k, cache_swa_v, cache_mla_ckv,
           cache_mla_krope, c_ctx, norm1_g, norm2_g, w_ada, b_ada, w_in, w_out, diff_lambda_q1,
           diff_lambda_k1, diff_lambda_q2, diff_lambda_k2, diff_subln_g, swa_sink, mla_q_norm_g,
           mla_w_qb, mla_kv_norm_g, mla_w_kvb, w_up, w_down, final_g):
    p = _prepare_params(norm1_g, norm2_g, w_in, w_out, diff_lambda_q1, diff_lambda_k1, diff_lambda_q2,
                        diff_lambda_k2, diff_subln_g, swa_sink, mla_q_norm_g, mla_w_qb, mla_kv_norm_g,
                        mla_w_kvb, w_up, w_down)
    cvecs = jnp.concatenate([c, c_ctx[None, :], jnp.zeros((8 - DEC_BATCH - 1, D_MODEL), F32)], axis=0)
    mod = _modulation(cvecs, w_ada, b_ada)
    tables = _rope_tables(DA_QK) + _rope_tables(SW_DIM)
    tile4 = jnp.asarray(np.tile(np.eye(MLA_ROPE, dtype=np.float32), (1, 4)), BF16)
    final_g2 = final_g.reshape(1, D_MODEL)
    caches = (cache_diff_k.reshape(DEC_BATCH, DEPTH, PAST_LEN, DA_HEADS * 2 * DA_QK),
              cache_diff_v.reshape(DEC_BATCH, DEPTH, PAST_LEN, DA_HEADS * DA_V),
              cache_swa_k.reshape(DEC_BATCH, DEPTH, PAST_LEN, SW_KV_HEADS * SW_DIM),
              cache_swa_v.reshape(DEC_BATCH, DEPTH, PAST_LEN, SW_KV_HEADS * SW_DIM),
              cache_mla_ckv, cache_mla_krope)

    x_ctx = x_prompt.reshape(N_CTX_TOK, D_MODEL)
    x_lat = x_sample.reshape(N_LAT_TOK, D_MODEL)
    new = [[] for _ in range(6)]
    lat_tiles_per_req = DEC_SEQ // TM_MLP
    for l in range(DEPTH):
        l_idx = jnp.full((1,), l, jnp.int32)
        last = l == DEPTH - 1
        *state, mix_ctx = _ctx_layer(l_idx, x_ctx, mod, p)
        for lst, s in zip(new, state):
            lst.append(s)
        x_ctx = _mlp_layer(l_idx, x_ctx, mix_ctx, mod, p, final_g2, lambda i: CTX_MOD_ROW, last)
        q, k, v = _lat_project(l_idx, x_lat, mod, p, tables)
        mix_lat = _lat_attend(l_idx, q, k, v, caches, p, tile4).reshape(N_LAT_TOK, D_MODEL)
        x_lat = _mlp_layer(l_idx, x_lat, mix_lat, mod, p, final_g2, lambda i: i // lat_tiles_per_req, last)

    y_prompt = x_ctx.reshape(BATCH, SEQ, D_MODEL)
    y_sample = x_lat.reshape(DEC_BATCH, DEC_SEQ, D_MODEL)
    new_dk, new_dv, new_sk, new_sv, new_ckv, new_kr = [jnp.stack(s, axis=1) for s in new]
    return (y_prompt, y_sample,
            new_dk.reshape(BATCH, DEPTH, SEQ, DA_HEADS, 2 * DA_QK),
            new_dv.reshape(BATCH, DEPTH, SEQ, DA_HEADS, DA_V),
            new_sk.reshape(BATCH, DEPTH, SEQ, SW_KV_HEADS, SW_DIM),
            new_sv.reshape(BATCH, DEPTH, SEQ, SW_KV_HEADS, SW_DIM),
            new_ckv, new_kr)
```

```python
import functools
import math

import numpy as np
import jax
import jax.numpy as jnp
from jax import lax
from jax.experimental import pallas as pl
from jax.experimental.pallas import tpu as pltpu

F32 = jnp.float32
BF16 = jnp.bfloat16

D_MODEL = 1024
BATCH = 32
SEQ = 256
DEPTH = 4
DEC_BATCH = 4
DEC_SEQ = 2048
PAST_LEN = 256
GRID_W = 64
WINDOW = 128
ROPE_BASE = 10000.0
EPS = 1e-6
DA_HEADS = 4
DA_QK = 32
DA_V = 64
SW_HEADS = 8
SW_KV_HEADS = 2
SW_DIM = 64
MLA_HEADS = 4
MLA_Q_RANK = 256
MLA_KV_RANK = 128
MLA_NOPE = 64
MLA_ROPE = 32
MLA_V = 64
D_FF = 4 * D_MODEL

LANES = 128
N_CTX_TOK = BATCH * SEQ
N_LAT_TOK = DEC_BATCH * DEC_SEQ
N_KEYS = PAST_LEN + DEC_SEQ
CTX_MOD_ROW = DEC_BATCH

C_DAQ, C_DAK, C_DAV, C_SWQ, C_SWK, C_SWV, C_QA, C_CKV, C_KR, P_EXT = (
    0, 256, 512, 768, 1280, 1408, 1536, 1792, 1920, 2048)
MLA_CHUNKS = MLA_HEADS * 128
Q_W = 256 + 512 + MLA_CHUNKS
K_W = 256 + 128 + MLA_CHUNKS
V_W = 256 + 128 + 256
O_MQ = 768
O_MK = 384
LOG2E = 1.4426950408889634

DA_SCALE = DA_QK ** -0.5
SW_SCALE = SW_DIM ** -0.5
MLA_SCALE = (MLA_NOPE + MLA_ROPE) ** -0.5

TM_CTX = SEQ
TM_LAT = 512
TM_MLP = 512
TQ = 256
SW_BAND = TQ + 2 * WINDOW
FF_CHUNK = 1024
MOD_TILE = 1536
VMEM_LIMIT = 56 * 1024 * 1024

_NT = (((1,), (1,)), ((), ()))

S_LAM_INIT = SW_HEADS
S_ONE_MINUS = SW_HEADS + 1
S_COLS = 16


def _rms(x, g):
    ms = jnp.mean(x * x, axis=-1, keepdims=True)
    return x * lax.rsqrt(ms + EPS) * g


def _lane_mask(n, lo, width):
    lane = lax.broadcasted_iota(jnp.int32, (1, n), 1)
    return (lane >= lo) & (lane < lo + width)


def _exp_scores(s, scale, sink=None):
    m = jnp.max(s, axis=-1, keepdims=True) * scale
    if sink is not None:
        m = jnp.maximum(m, sink)
    e = jnp.exp2(s * (scale * LOG2E) - m * LOG2E)
    d = jnp.sum(e, axis=-1, keepdims=True)
    if sink is not None:
        d = d + jnp.exp2((sink - m) * LOG2E)
    return e, d


def _chunk(a, j):
    return a[:, j * LANES:(j + 1) * LANES]


def _project(x_ref, mod_ref, n1g_ref, w_in_ref):
    m = mod_ref[0, 0]
    h = _rms(x_ref[...], n1g_ref[0]) * (1.0 + m[1:2]) + m[0:1]
    return jnp.dot(h.astype(BF16), w_in_ref[0], preferred_element_type=F32)


def _diff_lambda(lamv_ref, lam_init):
    lv = lamv_ref[0]
    a = jnp.sum(lv[0:1] * lv[1:2], axis=-1, keepdims=True)
    b = jnp.sum(lv[2:3] * lv[3:4], axis=-1, keepdims=True)
    return jnp.exp(a) - jnp.exp(b) + lam_init


def _diff_attention(qf, load_k, load_v, lam):
    width = DA_HEADS * DA_V
    acc = jnp.zeros((qf.shape[0], width), F32)
    for h in range(DA_HEADS):
        lo = h * DA_V
        q1 = jnp.where(_lane_mask(width, lo, DA_QK), qf, 0.0).astype(BF16)
        q2 = jnp.where(_lane_mask(width, lo + DA_QK, DA_QK), qf, 0.0).astype(BF16)
        e1, d1 = _exp_scores(lax.dot_general(q1, load_k(), _NT, preferred_element_type=F32), DA_SCALE)
        e2, d2 = _exp_scores(lax.dot_general(q2, load_k(), _NT, preferred_element_type=F32), DA_SCALE)
        w = (e1 - e2 * (lam * d1 * (1.0 / d2))).astype(BF16)
        o = jnp.dot(w, load_v(), preferred_element_type=F32) * (1.0 / d1)
        acc = jnp.where(_lane_mask(width, lo, DA_V), o, acc)
    return acc


def _diff_subln(o, subg, one_minus_lam_init):
    sq = o * o
    r = jnp.zeros_like(o)
    for h in range(DA_HEADS):
        mh = _lane_mask(DA_HEADS * DA_V, h * DA_V, DA_V)
        ms = jnp.sum(jnp.where(mh, sq, 0.0), axis=-1, keepdims=True) * (1.0 / DA_V)
        r = jnp.where(mh, lax.rsqrt(ms + EPS), r)
    return o * r * subg * one_minus_lam_init


def _swa_attention(qf, k, v, sinks, valid=None):
    lane = lax.broadcasted_iota(jnp.int32, (1, LANES), 1)
    outs = []
    for c in range(SW_HEADS // 2):
        qc = qf[:, c * LANES:(c + 1) * LANES]
        o_par = []
        for par in range(2):
            h = c + (SW_HEADS // 2) * par
            qm = jnp.where((lane >= par * SW_DIM) & (lane < (par + 1) * SW_DIM), qc, 0.0).astype(BF16)
            s = lax.dot_general(qm, k, _NT, preferred_element_type=F32)
            if valid is not None:
                s = jnp.where(valid, s, -1e30)
            e, d = _exp_scores(s, SW_SCALE, sink=sinks[h])
            o_par.append(jnp.dot(e.astype(BF16), v, preferred_element_type=F32) * (1.0 / d))
        outs.append(jnp.where(lane < SW_DIM, o_par[0], o_par[1]))
    return outs


def _mla_attention(load_q, load_k, load_v):
    width = MLA_HEADS * MLA_V
    acc = None
    for h in range(MLA_HEADS):
        e, d = _exp_scores(lax.dot_general(load_q(h), load_k(h), _NT, preferred_element_type=F32), MLA_SCALE)
        o = jnp.dot(e.astype(BF16), load_v(), preferred_element_type=F32) * (1.0 / d)
        acc = o if acc is None else jnp.where(_lane_mask(width, h * MLA_V, MLA_V), o, acc)
    return acc


def _mod_kernel(c_ref, w_ref, b_ref, o_ref):
    c = c_ref[...]
    s = c * (1.0 / (1.0 + jnp.exp(-c)))
    o_ref[0] = jnp.dot(s.astype(BF16), w_ref[0].astype(BF16), preferred_element_type=F32) + b_ref[0]


def _modulation(cvecs, w_ada, b_ada):
    n_out = 6 * D_MODEL
    out = pl.pallas_call(
        _mod_kernel,
        grid=(DEPTH, n_out // MOD_TILE),
        in_specs=[
            pl.BlockSpec((8, D_MODEL), lambda l, j: (0, 0)),
            pl.BlockSpec((1, D_MODEL, MOD_TILE), lambda l, j: (l, 0, j)),
            pl.BlockSpec((1, 1, MOD_TILE), lambda l, j: (l, 0, j)),
        ],
        out_specs=pl.BlockSpec((1, 8, MOD_TILE), lambda l, j: (l, 0, j)),
        out_shape=jax.ShapeDtypeStruct((DEPTH, 8, n_out), F32),
        compiler_params=pltpu.CompilerParams(
            dimension_semantics=("arbitrary", "arbitrary"), vmem_limit_bytes=VMEM_LIMIT),
        name="adaln_modulation",
    )(cvecs, w_ada, b_ada.reshape(DEPTH, 1, n_out))
    return out.reshape(DEPTH, 8, 6, D_MODEL)


def _ctx_kernel(l_ref, x_ref, mod_ref, n1g_ref, w_in_ref, qg_ref, wqb_ref, kvg_ref, wkvb_ref,
                lamv_ref, subg_ref, scal_ref,
                dk_ref, dv_ref, sk_ref, sv_ref, ckv_ref, kr_ref, mix_ref):
    l = l_ref[0]
    proj = _project(x_ref, mod_ref, n1g_ref, w_in_ref)
    da_k = proj[:, C_DAK:C_DAV]
    da_v = proj[:, C_DAV:C_SWQ]
    sw_k = proj[:, C_SWK:C_SWV]
    sw_v = proj[:, C_SWV:C_QA]
    ckv = _rms(proj[:, C_CKV:C_KR], kvg_ref[0])
    krz = proj[:, C_KR:P_EXT]
    dk_ref[0] = da_k
    dv_ref[0] = da_v
    sk_ref[0] = sw_k
    sv_ref[0] = sw_v
    ckv_ref[0] = ckv
    kr_ref[0] = krz[:, 0:MLA_ROPE]

    lam = _diff_lambda(lamv_ref, scal_ref[l, S_LAM_INIT])
    da_kb = da_k.astype(BF16)
    da_vb = da_v.astype(BF16)
    da_o = _diff_attention(proj[:, C_DAQ:C_DAK], lambda: da_kb, lambda: da_vb, lam)
    mix_ref[:, 0:256] = _diff_subln(da_o, subg_ref[0], scal_ref[l, S_ONE_MINUS]).astype(BF16)

    sinks = [scal_ref[l, h] for h in range(SW_HEADS)]
    sw_o = _swa_attention(proj[:, C_SWQ:C_SWK], sw_k.astype(BF16), sw_v.astype(BF16), sinks)
    for c, o in enumerate(sw_o):
        mix_ref[:, 256 + c * LANES:256 + (c + 1) * LANES] = o.astype(BF16)

    mq = jnp.dot(_rms(proj[:, C_QA:C_CKV], qg_ref[0]).astype(BF16), wqb_ref[0], preferred_element_type=F32)
    kv = jnp.dot(ckv.astype(BF16), wkvb_ref[0], preferred_element_type=F32)
    mv = kv[:, MLA_CHUNKS:].astype(BF16)
    mla_o = _mla_attention(lambda h: _chunk(mq, h).astype(BF16), lambda h: (_chunk(kv, h) + krz).astype(BF16),
                           lambda: mv)
    mix_ref[:, 768:1024] = mla_o.astype(BF16)


def _layer_spec(shape):
    nd = len(shape)
    return pl.BlockSpec((1,) + tuple(shape[1:]), lambda *a: (a[-1][0],) + (0,) * (nd - 1))


def _ctx_layer(l_idx, x, mod, p):
    n_req = BATCH
    cache_shapes = [DA_HEADS * 2 * DA_QK, DA_HEADS * DA_V, SW_KV_HEADS * SW_DIM, SW_KV_HEADS * SW_DIM,
                    MLA_KV_RANK, MLA_ROPE]
    out_shape = [jax.ShapeDtypeStruct((n_req, SEQ, w), F32) for w in cache_shapes]
    out_shape.append(jax.ShapeDtypeStruct((N_CTX_TOK, D_MODEL), BF16))
    out_specs = [pl.BlockSpec((1, SEQ, w), lambda i, l: (i, 0, 0)) for w in cache_shapes]
    out_specs.append(pl.BlockSpec((TM_CTX, D_MODEL), lambda i, l: (i, 0)))
    grid_spec = pltpu.PrefetchScalarGridSpec(
        num_scalar_prefetch=1,
        grid=(n_req,),
        in_specs=[
            pl.BlockSpec((TM_CTX, D_MODEL), lambda i, l: (i, 0)),
            pl.BlockSpec((1, 1, 6, D_MODEL), lambda i, l: (l[0], CTX_MOD_ROW, 0, 0)),
            _layer_spec(p["norm1_g"].shape),
            _layer_spec(p["w_in"].shape),
            _layer_spec(p["mla_q_norm_g"].shape),
            _layer_spec(p["mla_w_qb"].shape),
            _layer_spec(p["mla_kv_norm_g"].shape),
            _layer_spec(p["mla_w_kvb"].shape),
            _layer_spec(p["lamv"].shape),
            _layer_spec(p["subg"].shape),
            pl.BlockSpec(memory_space=pltpu.SMEM),
        ],
        out_specs=out_specs,
    )
    return pl.pallas_call(
        _ctx_kernel,
        grid_spec=grid_spec,
        out_shape=out_shape,
        compiler_params=pltpu.CompilerParams(
            dimension_semantics=("arbitrary",), vmem_limit_bytes=VMEM_LIMIT),
        name="ctx_project_attend",
    )(l_idx, x, mod, p["norm1_g"], p["w_in"], p["mla_q_norm_g"], p["mla_w_qb"], p["mla_kv_norm_g"],
      p["mla_w_kvb"], p["lamv"], p["subg"], p["scal"])


def _rope(x, cos, sa, sb, half):
    return x * cos + pltpu.roll(x, LANES - half, 1) * sa + pltpu.roll(x, half, 1) * sb


def _lat_proj_kernel(l_ref, x_ref, mod_ref, n1g_ref, w_in_ref, qg_ref, wqb_ref, kvg_ref, wkvb_ref,
                     c32_ref, a32_ref, b32_ref, c64_ref, a64_ref, b64_ref, cm_ref, am_ref, bm_ref,
                     q_ref, k_ref, v_ref):
    proj = _project(x_ref, mod_ref, n1g_ref, w_in_ref)
    t32 = (c32_ref[...], a32_ref[...], b32_ref[...], DA_QK // 4)
    t64 = (c64_ref[...], a64_ref[...], b64_ref[...], SW_DIM // 4)
    tml = (cm_ref[...], am_ref[...], bm_ref[...], MLA_ROPE // 4)

    for j in range(2):
        q_ref[:, j * LANES:(j + 1) * LANES] = _rope(_chunk(proj, j), *t32).astype(BF16)
    for j in range(4):
        q_ref[:, 256 + j * LANES:256 + (j + 1) * LANES] = _rope(_chunk(proj, C_SWQ // LANES + j), *t64).astype(BF16)
    mq = jnp.dot(_rms(proj[:, C_QA:C_CKV], qg_ref[0]).astype(BF16), wqb_ref[0], preferred_element_type=F32)
    for h in range(MLA_HEADS):
        q_ref[:, O_MQ + h * LANES:O_MQ + (h + 1) * LANES] = _rope(_chunk(mq, h), *tml).astype(BF16)

    for j in range(2):
        k_ref[:, j * LANES:(j + 1) * LANES] = _rope(_chunk(proj, C_DAK // LANES + j), *t32).astype(BF16)
    k_ref[:, 256:384] = _rope(proj[:, C_SWK:C_SWV], *t64).astype(BF16)
    ckv = _rms(proj[:, C_CKV:C_KR], kvg_ref[0])
    kv = jnp.dot(ckv.astype(BF16), wkvb_ref[0], preferred_element_type=F32)
    krz = _rope(proj[:, C_KR:P_EXT], *tml)
    for h in range(MLA_HEADS):
        k_ref[:, O_MK + h * LANES:O_MK + (h + 1) * LANES] = (_chunk(kv, h) + krz).astype(BF16)

    v_ref[:, 0:256] = proj[:, C_DAV:C_SWQ].astype(BF16)
    v_ref[:, 256:384] = proj[:, C_SWV:C_QA].astype(BF16)
    v_ref[:, 384:640] = kv[:, MLA_CHUNKS:].astype(BF16)


def _lat_project(l_idx, x, mod, p, tables):
    tiles_per_req = DEC_SEQ // TM_LAT
    tab_spec = pl.BlockSpec((TM_LAT, LANES), lambda i, l: (i % tiles_per_req, 0))
    grid_spec = pltpu.PrefetchScalarGridSpec(
        num_scalar_prefetch=1,
        grid=(N_LAT_TOK // TM_LAT,),
        in_specs=[
            pl.BlockSpec((TM_LAT, D_MODEL), lambda i, l: (i, 0)),
            pl.BlockSpec((1, 1, 6, D_MODEL), lambda i, l: (l[0], i // tiles_per_req, 0, 0)),
            _layer_spec(p["norm1_g"].shape),
            _layer_spec(p["w_in"].shape),
            _layer_spec(p["mla_q_norm_g"].shape),
            _layer_spec(p["mla_w_qb"].shape),
            _layer_spec(p["mla_kv_norm_g"].shape),
            _layer_spec(p["mla_w_kvb"].shape),
        ] + [tab_spec] * len(tables),
        out_specs=[
            pl.BlockSpec((TM_LAT, Q_W), lambda i, l: (i, 0)),
            pl.BlockSpec((TM_LAT, K_W), lambda i, l: (i, 0)),
            pl.BlockSpec((TM_LAT, V_W), lambda i, l: (i, 0)),
        ],
    )
    return pl.pallas_call(
        _lat_proj_kernel,
        grid_spec=grid_spec,
        out_shape=[jax.ShapeDtypeStruct((N_LAT_TOK, Q_W), BF16),
                   jax.ShapeDtypeStruct((N_LAT_TOK, K_W), BF16),
                   jax.ShapeDtypeStruct((N_LAT_TOK, V_W), BF16)],
        compiler_params=pltpu.CompilerParams(
            dimension_semantics=("arbitrary",), vmem_limit_bytes=VMEM_LIMIT),
        name="latent_project",
    )(l_idx, x, mod, p["norm1_g"], p["w_in"], p["mla_q_norm_g"], p["mla_w_qb"], p["mla_kv_norm_g"],
      p["mla_w_kvb"], *tables)


def _lat_attn_kernel(l_ref, q_ref, k_ref, v_ref, cdk_ref, cdv_ref, csk_ref, csv_ref, cckv_ref, ckr_ref,
                     wkvb_ref, pad_ref, lamv_ref, subg_ref, scal_ref,
                     mix_ref,
                     kda, vda, ksw, vsw, kml, vml):
    l = l_ref[0]
    qb = pl.program_id(1)

    @pl.when(qb == 0)
    def _stage_keys():
        kda[0:PAST_LEN, :] = cdk_ref[0, 0].astype(BF16)
        kda[PAST_LEN:N_KEYS, :] = k_ref[0, :, 0:256]
        vda[0:PAST_LEN, :] = cdv_ref[0, 0].astype(BF16)
        vda[PAST_LEN:N_KEYS, :] = v_ref[0, :, 0:256]
        ksw[0:PAST_LEN, :] = csk_ref[0, 0].astype(BF16)
        ksw[PAST_LEN:N_KEYS, :] = k_ref[0, :, 256:384]
        vsw[0:PAST_LEN, :] = csv_ref[0, 0].astype(BF16)
        vsw[PAST_LEN:N_KEYS, :] = v_ref[0, :, 256:384]
        kvc = jnp.dot(cckv_ref[0, 0].astype(BF16), wkvb_ref[0], preferred_element_type=F32)
        krz = jnp.dot(ckr_ref[0, 0].astype(BF16), pad_ref[...], preferred_element_type=F32)
        for h in range(MLA_HEADS):
            kml[0:PAST_LEN, h * LANES:(h + 1) * LANES] = (_chunk(kvc, h) + krz).astype(BF16)
        kml[PAST_LEN:N_KEYS, :] = k_ref[0, :, O_MK:K_W]
        vml[0:PAST_LEN, :] = kvc[:, MLA_CHUNKS:].astype(BF16)
        vml[PAST_LEN:N_KEYS, :] = v_ref[0, :, 384:640]

    lam = _diff_lambda(lamv_ref, scal_ref[l, S_LAM_INIT])
    da_o = _diff_attention(q_ref[0, :, 0:256].astype(F32), lambda: kda[...], lambda: vda[...], lam)
    mix_ref[0, :, 0:256] = _diff_subln(da_o, subg_ref[0], scal_ref[l, S_ONE_MINUS]).astype(BF16)

    q0 = qb * TQ
    start = pl.multiple_of(jnp.clip(q0 - WINDOW, 0, DEC_SEQ - SW_BAND), LANES)
    band = pl.ds(PAST_LEN + start, SW_BAND)
    k_sel = jnp.concatenate([ksw[0:PAST_LEN, :], ksw[band, :]], axis=0)
    v_sel = jnp.concatenate([vsw[0:PAST_LEN, :], vsw[band, :]], axis=0)
    n_sel = PAST_LEN + SW_BAND
    col = lax.broadcasted_iota(jnp.int32, (TQ, n_sel), 1)
    row = lax.broadcasted_iota(jnp.int32, (TQ, n_sel), 0)
    dist = (q0 + row) - (start + col - PAST_LEN)
    valid = (col < PAST_LEN) | ((dist <= WINDOW) & (dist >= -WINDOW))
    sinks = [scal_ref[l, h] for h in range(SW_HEADS)]
    sw_o = _swa_attention(q_ref[0, :, 256:768].astype(F32), k_sel, v_sel, sinks, valid)
    for c, o in enumerate(sw_o):
        mix_ref[0, :, 256 + c * LANES:256 + (c + 1) * LANES] = o.astype(BF16)

    mla_o = _mla_attention(lambda h: q_ref[0, :, O_MQ + h * LANES:O_MQ + (h + 1) * LANES],
                           lambda h: kml[:, h * LANES:(h + 1) * LANES], lambda: vml[...])
    mix_ref[0, :, 768:1024] = mla_o.astype(BF16)


def _lat_attend(l_idx, q, k, v, caches, p, pad_eye):
    cdk, cdv, csk, csv, cckv, ckr = caches

    def cache_spec(a):
        return pl.BlockSpec((1, 1) + a.shape[2:], lambda b, j, l: (b, l[0], 0, 0))

    grid_spec = pltpu.PrefetchScalarGridSpec(
        num_scalar_prefetch=1,
        grid=(DEC_BATCH, DEC_SEQ // TQ),
        in_specs=[
            pl.BlockSpec((1, TQ, Q_W), lambda b, j, l: (b, j, 0)),
            pl.BlockSpec((1, DEC_SEQ, K_W), lambda b, j, l: (b, 0, 0)),
            pl.BlockSpec((1, DEC_SEQ, V_W), lambda b, j, l: (b, 0, 0)),
            cache_spec(cdk), cache_spec(cdv), cache_spec(csk), cache_spec(csv), cache_spec(cckv), cache_spec(ckr),
            _layer_spec(p["mla_w_kvb"].shape),
            pl.BlockSpec(pad_eye.shape, lambda b, j, l: (0, 0)),
            _layer_spec(p["lamv"].shape),
            _layer_spec(p["subg"].shape),
            pl.BlockSpec(memory_space=pltpu.SMEM),
        ],
        out_specs=pl.BlockSpec((1, TQ, D_MODEL), lambda b, j, l: (b, j, 0)),
        scratch_shapes=[
            pltpu.VMEM((N_KEYS, 256), BF16), pltpu.VMEM((N_KEYS, 256), BF16),
            pltpu.VMEM((N_KEYS, 128), BF16), pltpu.VMEM((N_KEYS, 128), BF16),
            pltpu.VMEM((N_KEYS, MLA_CHUNKS), BF16), pltpu.VMEM((N_KEYS, 256), BF16),
        ],
    )
    return pl.pallas_call(
        _lat_attn_kernel,
        grid_spec=grid_spec,
        out_shape=jax.ShapeDtypeStruct((DEC_BATCH, DEC_SEQ, D_MODEL), BF16),
        compiler_params=pltpu.CompilerParams(
            dimension_semantics=("arbitrary", "arbitrary"), vmem_limit_bytes=VMEM_LIMIT),
        name="latent_attend",
    )(l_idx, q.reshape(DEC_BATCH, DEC_SEQ, Q_W), k.reshape(DEC_BATCH, DEC_SEQ, K_W),
      v.reshape(DEC_BATCH, DEC_SEQ, V_W), cdk, cdv, csk, csv, cckv, ckr,
      p["mla_w_kvb"], pad_eye, p["lamv"], p["subg"], p["scal"])


def _mlp_kernel(l_ref, x_ref, mix_ref, mod_ref, n2g_ref, w_out_ref, w_up_ref, w_down_ref, fg_ref, o_ref,
                *, final_norm):
    m = mod_ref[0, 0]
    x = x_ref[...] + m[2:3] * jnp.dot(mix_ref[...], w_out_ref[0], preferred_element_type=F32)
    h = (_rms(x, n2g_ref[0]) * (1.0 + m[4:5]) + m[3:4]).astype(BF16)
    acc = jnp.zeros_like(x)
    for c in range(D_FF // FF_CHUNK):
        u = jnp.dot(h, w_up_ref[0, :, c * FF_CHUNK:(c + 1) * FF_CHUNK], preferred_element_type=F32)
        a = jnp.square(jnp.maximum(u, 0.0)).astype(BF16)
        acc = acc + jnp.dot(a, w_down_ref[0, c * FF_CHUNK:(c + 1) * FF_CHUNK, :], preferred_element_type=F32)
    x = x + m[5:6] * acc
    if final_norm:
        x = _rms(x, fg_ref[...])
    o_ref[...] = x


def _mlp_layer(l_idx, x, mix, mod, p, final_g, mod_row, final_norm):
    n_tok = x.shape[0]
    grid_spec = pltpu.PrefetchScalarGridSpec(
        num_scalar_prefetch=1,
        grid=(n_tok // TM_MLP,),
        in_specs=[
            pl.BlockSpec((TM_MLP, D_MODEL), lambda i, l: (i, 0)),
            pl.BlockSpec((TM_MLP, D_MODEL), lambda i, l: (i, 0)),
            pl.BlockSpec((1, 1, 6, D_MODEL), lambda i, l: (l[0], mod_row(i), 0, 0)),
            _layer_spec(p["norm2_g"].shape),
            _layer_spec(p["w_out"].shape),
            _layer_spec(p["w_up"].shape),
            _layer_spec(p["w_down"].shape),
            pl.BlockSpec((1, D_MODEL), lambda i, l: (0, 0)),
        ],
        out_specs=pl.BlockSpec((TM_MLP, D_MODEL), lambda i, l: (i, 0)),
    )
    return pl.pallas_call(
        functools.partial(_mlp_kernel, final_norm=final_norm),
        grid_spec=grid_spec,
        out_shape=jax.ShapeDtypeStruct((n_tok, D_MODEL), F32),
        compiler_params=pltpu.CompilerParams(
            dimension_semantics=("arbitrary",), vmem_limit_bytes=VMEM_LIMIT),
        name="out_proj_mlp",
    )(l_idx, x, mix, mod, p["norm2_g"], p["w_out"], p["w_up"], p["w_down"], final_g)


def _rope_tables(period, active_lanes=LANES):
    m = period // 2
    half = m // 2
    t = np.arange(DEC_SEQ)
    pos = np.stack([t // GRID_W, t % GRID_W], axis=0).astype(np.float64)
    lane = np.arange(LANES) % period
    axis = lane // m
    k = lane % m
    freq = ROPE_BASE ** (-(k % half).astype(np.float64) / half)
    ang = pos[axis, :].T * freq[None, :]
    first = (k < half)[None, :]
    active = (np.arange(LANES) < active_lanes)[None, :]
    cos, sin = np.where(active, np.cos(ang), 1.0), np.where(active, np.sin(ang), 0.0)
    return (jnp.asarray(cos, F32), jnp.asarray(np.where(first, -sin, 0.0), F32),
            jnp.asarray(np.where(first, 0.0, sin), F32))


def _paired_head_order():
    order = []
    for c in range(SW_HEADS // 2):
        order += [c, c + SW_HEADS // 2]
    return order


def _prepare_params(norm1_g, norm2_g, w_in, w_out, lq1, lk1, lq2, lk2, diff_subln_g, swa_sink,
                    mla_q_norm_g, mla_w_qb, mla_kv_norm_g, mla_w_kvb, w_up, w_down):
    heads = _paired_head_order()
    o_swq = 768
    swq_cols = [w_in[:, :, o_swq + h * SW_DIM:o_swq + (h + 1) * SW_DIM] for h in heads]
    kr_cols = [w_in[:, :, 1920:1952], jnp.zeros((DEPTH, D_MODEL, LANES - MLA_ROPE), w_in.dtype)]
    w_in_ext = jnp.concatenate([w_in[:, :, 0:768]] + swq_cols + [w_in[:, :, 1280:1920]] + kr_cols,
                               axis=-1).astype(BF16)
    o_swo = DA_HEADS * DA_V
    swo_rows = [w_out[:, o_swo + h * SW_DIM:o_swo + (h + 1) * SW_DIM, :] for h in heads]
    w_out_p = jnp.concatenate([w_out[:, 0:o_swo, :]] + swo_rows + [w_out[:, o_swo + SW_HEADS * SW_DIM:, :]],
                              axis=1).astype(BF16)
    dq = MLA_NOPE + MLA_ROPE
    gap = LANES - dq
    zq = jnp.zeros((DEPTH, MLA_Q_RANK, gap), mla_w_qb.dtype)
    wqb = jnp.concatenate(
        [blk for h in range(MLA_HEADS)
         for blk in (mla_w_qb[:, :, h * dq + MLA_NOPE:(h + 1) * dq], zq, mla_w_qb[:, :, h * dq:h * dq + MLA_NOPE])],
        axis=-1).astype(BF16)
    dkv = MLA_NOPE + MLA_V
    zk = jnp.zeros((DEPTH, MLA_KV_RANK, LANES - MLA_NOPE), mla_w_kvb.dtype)
    wkvb = jnp.concatenate(
        [blk for h in range(MLA_HEADS) for blk in (zk, mla_w_kvb[:, :, h * dkv:h * dkv + MLA_NOPE])]
        + [mla_w_kvb[:, :, h * dkv + MLA_NOPE:(h + 1) * dkv] for h in range(MLA_HEADS)],
        axis=-1).astype(BF16)
    lam_init = np.array([0.8 - 0.6 * math.exp(-0.3 * l) for l in range(DEPTH)], np.float32)
    scal = jnp.concatenate([
        swa_sink.astype(F32),
        jnp.asarray(np.stack([lam_init, (1.0 - lam_init.astype(np.float64)).astype(np.float32)], axis=1)),
        jnp.zeros((DEPTH, S_COLS - SW_HEADS - 2), F32)], axis=1)
    return {
        "norm1_g": norm1_g.reshape(DEPTH, 1, D_MODEL),
        "norm2_g": norm2_g.reshape(DEPTH, 1, D_MODEL),
        "w_in": w_in_ext,
        "w_out": w_out_p,
        "mla_q_norm_g": mla_q_norm_g.reshape(DEPTH, 1, MLA_Q_RANK),
        "mla_w_qb": wqb,
        "mla_kv_norm_g": mla_kv_norm_g.reshape(DEPTH, 1, MLA_KV_RANK),
        "mla_w_kvb": wkvb,
        "lamv": jnp.stack([lq1, lk1, lq2, lk2], axis=1),
        "subg": jnp.tile(diff_subln_g, (1, DA_HEADS)).reshape(DEPTH, 1, DA_HEADS * DA_V),
        "scal": scal,
        "w_up": w_up.astype(BF16),
        "w_down": w_down.astype(BF16),
    }


def kernel(x_prompt, x_sample, c, cache_diff_k, cache_diff_v, cache_swa_k, cache_swa_v, cache_mla_ckv,
           cache_mla_krope, c_ctx, norm1_g, norm2_g, w_ada, b_ada, w_in, w_out, diff_lambda_q1,
           diff_lambda_k1, diff_lambda_q2, diff_lambda_k2, diff_subln_g, swa_sink, mla_q_norm_g,
           mla_w_qb, mla_kv_norm_g, mla_w_kvb, w_up, w_down, final_g):
    p = _prepare_params(norm1_g, norm2_g, w_in, w_out, diff_lambda_q1, diff_lambda_k1, diff_lambda_q2,
                        diff_lambda_k2, diff_subln_g, swa_sink, mla_q_norm_g, mla_w_qb, mla_kv_norm_g,
                        mla_w_kvb, w_up, w_down)
    cvecs = jnp.concatenate([c, c_ctx[None, :], jnp.zeros((8 - DEC_BATCH - 1, D_MODEL), F32)], axis=0)
    mod = _modulation(cvecs, w_ada, b_ada)
    tables = _rope_tables(DA_QK) + _rope_tables(SW_DIM) + _rope_tables(MLA_ROPE, MLA_ROPE)
    pad_eye = jnp.asarray(np.eye(MLA_ROPE, LANES, dtype=np.float32), BF16)
    final_g2 = final_g.reshape(1, D_MODEL)
    caches = (cache_diff_k.reshape(DEC_BATCH, DEPTH, PAST_LEN, DA_HEADS * 2 * DA_QK),
              cache_diff_v.reshape(DEC_BATCH, DEPTH, PAST_LEN, DA_HEADS * DA_V),
              cache_swa_k.reshape(DEC_BATCH, DEPTH, PAST_LEN, SW_KV_HEADS * SW_DIM),
              cache_swa_v.reshape(DEC_BATCH, DEPTH, PAST_LEN, SW_KV_HEADS * SW_DIM),
              cache_mla_ckv, cache_mla_krope)

    x_ctx = x_prompt.reshape(N_CTX_TOK, D_MODEL)
    x_lat = x_sample.reshape(N_LAT_TOK, D_MODEL)
    new = [[] for _ in range(6)]
    lat_tiles_per_req = DEC_SEQ // TM_MLP
    for l in range(DEPTH):
        l_idx = jnp.full((1,), l, jnp.int32)
        last = l == DEPTH - 1
        *state, mix_ctx = _ctx_layer(l_idx, x_ctx, mod, p)
        for lst, s in zip(new, state):
            lst.append(s)
        x_ctx = _mlp_layer(l_idx, x_ctx, mix_ctx, mod, p, final_g2, lambda i: CTX_MOD_ROW, last)
        q, k, v = _lat_project(l_idx, x_lat, mod, p, tables)
        mix_lat = _lat_attend(l_idx, q, k, v, caches, p, pad_eye).reshape(N_LAT_TOK, D_MODEL)
        x_lat = _mlp_layer(l_idx, x_lat, mix_lat, mod, p, final_g2, lambda i: i // lat_tiles_per_req, last)

    y_prompt = x_ctx.reshape(BATCH, SEQ, D_MODEL)
    y_sample = x_lat.reshape(DEC_BATCH, DEC_SEQ, D_MODEL)
    new_dk, new_dv, new_sk, new_sv, new_ckv, new_kr = [jnp.stack(s, axis=1) for s in new]
    return (y_prompt, y_sample,
            new_dk.reshape(BATCH, DEPTH, SEQ, DA_HEADS, 2 * DA_QK),
            new_dv.reshape(BATCH, DEPTH, SEQ, DA_HEADS, DA_V),
            new_sk.reshape(BATCH, DEPTH, SEQ, SW_KV_HEADS, SW_DIM),
            new_sv.reshape(BATCH, DEPTH, SEQ, SW_KV_HEADS, SW_DIM),
            new_ckv, new_kr)
```

```python
import functools
import math

import numpy as np
import jax
import jax.numpy as jnp
from jax import lax
from jax.experimental import pallas as pl
from jax.experimental.pallas import tpu as pltpu

F32 = jnp.float32
BF16 = jnp.bfloat16

D_MODEL = 1024
BATCH = 32
SEQ = 256
DEPTH = 4
DEC_BATCH = 4
DEC_SEQ = 2048
PAST_LEN = 256
GRID_W = 64
WINDOW = 128
ROPE_BASE = 10000.0
EPS = 1e-6
DA_HEADS = 4
DA_QK = 32
DA_V = 64
SW_HEADS = 8
SW_KV_HEADS = 2
SW_DIM = 64
MLA_HEADS = 4
MLA_Q_RANK = 256
MLA_KV_RANK = 128
MLA_NOPE = 64
MLA_ROPE = 32
MLA_V = 64
D_FF = 4 * D_MODEL

LANES = 128
N_CTX_TOK = BATCH * SEQ
N_LAT_TOK = DEC_BATCH * DEC_SEQ
N_KEYS = PAST_LEN + DEC_SEQ
CTX_MOD_ROW = DEC_BATCH

C_DAQ, C_DAK, C_DAV, C_SWQ, C_SWK, C_SWV, C_QA, C_CKV, C_KR, P_EXT = (
    0, 256, 512, 768, 1280, 1408, 1536, 1792, 1920, 2048)
MLA_CHUNKS = MLA_HEADS * 128
Q_W = 256 + 512 + MLA_CHUNKS
K_W = 256 + 128 + MLA_CHUNKS
V_W = 256 + 128 + 256
O_MQ = 768
O_MK = 384
LOG2E = 1.4426950408889634

DA_SCALE = DA_QK ** -0.5
SW_SCALE = SW_DIM ** -0.5
MLA_SCALE = (MLA_NOPE + MLA_ROPE) ** -0.5

TM_CTX = SEQ
TM_LAT = 512
TM_MLP = 512
TQ = 256
SW_BAND = TQ + 2 * WINDOW
FF_CHUNK = 1024
MOD_TILE = 1536
VMEM_LIMIT = 56 * 1024 * 1024

_NT = (((1,), (1,)), ((), ()))

S_LAM_INIT = SW_HEADS
S_ONE_MINUS = SW_HEADS + 1
S_COLS = 16


def _rms(x, g):
    ms = jnp.mean(x * x, axis=-1, keepdims=True)
    return x * lax.rsqrt(ms + EPS) * g


def _lane_mask(n, lo, width):
    lane = lax.broadcasted_iota(jnp.int32, (1, n), 1)
    return (lane >= lo) & (lane < lo + width)


def _exp_scores(s, sink=None):
    m = jnp.max(s, axis=-1, keepdims=True)
    if sink is not None:
        m = jnp.maximum(m, sink)
    return jnp.exp2(s - m), m


def _chunk(a, j):
    return a[:, j * LANES:(j + 1) * LANES]


def _project(x_ref, mod_ref, n1g_ref, w_in_ref):
    m = mod_ref[0, 0]
    h = _rms(x_ref[...], n1g_ref[0]) * (1.0 + m[1:2]) + m[0:1]
    return jnp.dot(h.astype(BF16), w_in_ref[0], preferred_element_type=F32)


def _diff_lambda(lamv_ref, lam_init):
    lv = lamv_ref[0]
    a = jnp.sum(lv[0:1] * lv[1:2], axis=-1, keepdims=True)
    b = jnp.sum(lv[2:3] * lv[3:4], axis=-1, keepdims=True)
    return jnp.exp(a) - jnp.exp(b) + lam_init


def _half_mask(par):
    lane = lax.broadcasted_iota(jnp.int32, (1, LANES), 1)
    return (lane >= par * 64) & (lane < (par + 1) * 64)


def _diff_attention(qf, load_k, load_v, lam):
    width = DA_HEADS * DA_V
    chunks = []
    for c in range(DA_HEADS // 2):
        o_par = []
        for par in range(2):
            lo = (2 * c + par) * DA_V
            q1 = jnp.where(_lane_mask(width, lo, DA_QK), qf, 0.0).astype(BF16)
            q2 = jnp.where(_lane_mask(width, lo + DA_QK, DA_QK), qf, 0.0).astype(BF16)
            e1, _ = _exp_scores(lax.dot_general(q1, load_k(), _NT, preferred_element_type=F32))
            e2, _ = _exp_scores(lax.dot_general(q2, load_k(), _NT, preferred_element_type=F32))
            d1 = jnp.sum(e1, axis=-1, keepdims=True)
            d2 = jnp.sum(e2, axis=-1, keepdims=True)
            w = (e1 - e2 * (lam * d1 * (1.0 / d2))).astype(BF16)
            o_par.append(jnp.dot(w, load_v(c), preferred_element_type=F32) * (1.0 / d1))
        chunks.append(jnp.where(_half_mask(0), o_par[0], o_par[1]))
    return chunks


def _diff_subln(o, subg, one_minus_lam_init):
    sq = o * o
    r = None
    for par in range(2):
        ms = jnp.sum(jnp.where(_half_mask(par), sq, 0.0), axis=-1, keepdims=True) * (1.0 / DA_V)
        rs = lax.rsqrt(ms + EPS)
        r = rs if r is None else jnp.where(_half_mask(0), r, rs)
    return o * r * subg * one_minus_lam_init


def _swa_attention(qf, k, v_ext, sinks, valid=None):
    outs = []
    for c in range(SW_HEADS // 2):
        qc = _chunk(qf, c)
        o_par = []
        for par in range(2):
            h = c + (SW_HEADS // 2) * par
            qm = jnp.where(_half_mask(par), qc, 0.0).astype(BF16)
            s = lax.dot_general(qm, k, _NT, preferred_element_type=F32)
            if valid is not None:
                s = jnp.where(valid, s, -1e30)
            e, m = _exp_scores(s, sink=sinks[h])
            o = jnp.dot(e.astype(BF16), v_ext, preferred_element_type=F32)
            d = _chunk(o, 1) + jnp.exp2(sinks[h] - m)
            o_par.append(_chunk(o, 0) * (1.0 / d))
        outs.append(jnp.where(_half_mask(0), o_par[0], o_par[1]))
    return outs


def _mla_attention(load_q, load_k, load_v):
    chunks = []
    for c in range(MLA_HEADS // 2):
        o_par = []
        for par in range(2):
            h = 2 * c + par
            e, _ = _exp_scores(lax.dot_general(load_q(h), load_k(h), _NT, preferred_element_type=F32))
            o = jnp.dot(e.astype(BF16), load_v(h), preferred_element_type=F32)
            o_par.append(o * (1.0 / pltpu.roll(o, 64, 1)))
        chunks.append(jnp.where(_half_mask(0), o_par[0], o_par[1]))
    return chunks


def _ones_in_other_half(v, par):
    return jnp.where(_half_mask(par), v, 1.0)


def _mod_kernel(c_ref, w_ref, b_ref, o_ref):
    c = c_ref[...]
    s = c * (1.0 / (1.0 + jnp.exp(-c)))
    o_ref[0] = jnp.dot(s.astype(BF16), w_ref[0].astype(BF16), preferred_element_type=F32) + b_ref[0]


def _modulation(cvecs, w_ada, b_ada):
    n_out = 6 * D_MODEL
    out = pl.pallas_call(
        _mod_kernel,
        grid=(DEPTH, n_out // MOD_TILE),
        in_specs=[
            pl.BlockSpec((8, D_MODEL), lambda l, j: (0, 0)),
            pl.BlockSpec((1, D_MODEL, MOD_TILE), lambda l, j: (l, 0, j)),
            pl.BlockSpec((1, 1, MOD_TILE), lambda l, j: (l, 0, j)),
        ],
        out_specs=pl.BlockSpec((1, 8, MOD_TILE), lambda l, j: (l, 0, j)),
        out_shape=jax.ShapeDtypeStruct((DEPTH, 8, n_out), F32),
        compiler_params=pltpu.CompilerParams(
            dimension_semantics=("arbitrary", "arbitrary"), vmem_limit_bytes=VMEM_LIMIT),
        name="adaln_modulation",
    )(cvecs, w_ada, b_ada.reshape(DEPTH, 1, n_out))
    return out.reshape(DEPTH, 8, 6, D_MODEL)


def _ctx_kernel(l_ref, x_ref, mod_ref, n1g_ref, w_in_ref, qg_ref, wqb_ref, kvg_ref, wkvb_ref,
                lamv_ref, subg_ref, scal_ref, *rest):
    dk_ref, dv_ref, sk_ref, sv_ref, ckv_ref, kr_ref, mix_ref = rest[-7:]
    l = l_ref[0]
    proj = _project(x_ref, mod_ref, n1g_ref, w_in_ref)
    da_k = proj[:, C_DAK:C_DAV]
    da_v = proj[:, C_DAV:C_SWQ]
    sw_k = proj[:, C_SWK:C_SWV]
    sw_v = proj[:, C_SWV:C_QA]
    ckv = _rms(proj[:, C_CKV:C_KR], kvg_ref[0])
    krz = proj[:, C_KR:P_EXT]
    dk_ref[0, 0] = da_k
    dv_ref[0, 0] = da_v
    sk_ref[0, 0] = sw_k
    sv_ref[0, 0] = sw_v
    ckv_ref[0, 0] = ckv
    kr_ref[0, 0] = krz[:, 0:MLA_ROPE]

    lam = _diff_lambda(lamv_ref, scal_ref[l, S_LAM_INIT])
    da_kb = da_k.astype(BF16)
    da_o = _diff_attention(proj[:, C_DAQ:C_DAK] * (DA_SCALE * LOG2E), lambda: da_kb,
                           lambda c: _chunk(da_v, c).astype(BF16), lam)
    subg = subg_ref[0]
    for c, o in enumerate(da_o):
        mix_ref[:, c * LANES:(c + 1) * LANES] = _diff_subln(o, _chunk(subg, c), scal_ref[l, S_ONE_MINUS]).astype(BF16)

    sinks = [scal_ref[l, h] for h in range(SW_HEADS)]
    sw_v_ext = jnp.concatenate([sw_v, jnp.ones_like(sw_v)], axis=-1).astype(BF16)
    sw_o = _swa_attention(proj[:, C_SWQ:C_SWK] * (SW_SCALE * LOG2E), sw_k.astype(BF16), sw_v_ext, sinks)
    for c, o in enumerate(sw_o):
        mix_ref[:, 256 + c * LANES:256 + (c + 1) * LANES] = o.astype(BF16)

    mq = jnp.dot(_rms(proj[:, C_QA:C_CKV], qg_ref[0]).astype(BF16), wqb_ref[0], preferred_element_type=F32)
    mq = mq * (MLA_SCALE * LOG2E)
    kv = jnp.dot(ckv.astype(BF16), wkvb_ref[0], preferred_element_type=F32)
    mla_o = _mla_attention(
        lambda h: _chunk(mq, h).astype(BF16), lambda h: (_chunk(kv, h) + krz).astype(BF16),
        lambda h: _ones_in_other_half(_chunk(kv, MLA_HEADS + h // 2), h % 2).astype(BF16))
    for c, o in enumerate(mla_o):
        mix_ref[:, 768 + c * LANES:768 + (c + 1) * LANES] = o.astype(BF16)


def _layer_spec(shape):
    nd = len(shape)
    return pl.BlockSpec((1,) + tuple(shape[1:]), lambda *a: (a[-1][0],) + (0,) * (nd - 1))


CACHE_WIDTHS = (DA_HEADS * 2 * DA_QK, DA_HEADS * DA_V, SW_KV_HEADS * SW_DIM, SW_KV_HEADS * SW_DIM,
                MLA_KV_RANK, MLA_ROPE)


def _ctx_layer(l_idx, x, mod, p, new_caches):
    n_req = BATCH
    out_shape = [jax.ShapeDtypeStruct(c.shape, c.dtype) for c in new_caches]
    out_shape.append(jax.ShapeDtypeStruct((N_CTX_TOK, D_MODEL), BF16))
    out_specs = [pl.BlockSpec((1, 1, SEQ, w), lambda i, l: (i, l[0], 0, 0)) for w in CACHE_WIDTHS]
    out_specs.append(pl.BlockSpec((TM_CTX, D_MODEL), lambda i, l: (i, 0)))
    n_fixed_inputs = 12
    aliases = {n_fixed_inputs + j: j for j in range(len(new_caches))}
    grid_spec = pltpu.PrefetchScalarGridSpec(
        num_scalar_prefetch=1,
        grid=(n_req,),
        in_specs=[
            pl.BlockSpec((TM_CTX, D_MODEL), lambda i, l: (i, 0)),
            pl.BlockSpec((1, 1, 6, D_MODEL), lambda i, l: (l[0], CTX_MOD_ROW, 0, 0)),
            _layer_spec(p["norm1_g"].shape),
            _layer_spec(p["w_in"].shape),
            _layer_spec(p["mla_q_norm_g"].shape),
            _layer_spec(p["mla_w_qb"].shape),
            _layer_spec(p["mla_kv_norm_g"].shape),
            _layer_spec(p["mla_w_kvb"].shape),
            _layer_spec(p["lamv"].shape),
            _layer_spec(p["subg"].shape),
            pl.BlockSpec(memory_space=pltpu.SMEM),
        ] + [pl.BlockSpec(memory_space=pl.ANY)] * len(new_caches),
        out_specs=out_specs,
    )
    return pl.pallas_call(
        _ctx_kernel,
        grid_spec=grid_spec,
        out_shape=out_shape,
        input_output_aliases=aliases,
        compiler_params=pltpu.CompilerParams(
            dimension_semantics=("arbitrary",), vmem_limit_bytes=VMEM_LIMIT),
        name="ctx_project_attend",
    )(l_idx, x, mod, p["norm1_g"], p["w_in"], p["mla_q_norm_g"], p["mla_w_qb"], p["mla_kv_norm_g"],
      p["mla_w_kvb"], p["lamv"], p["subg"], p["scal"], *new_caches)


def _rope(x, cos, sa, sb, half):
    return x * cos + pltpu.roll(x, LANES - half, 1) * sa + pltpu.roll(x, half, 1) * sb


def _lat_proj_kernel(l_ref, x_ref, mod_ref, n1g_ref, w_in_ref, qg_ref, wqb_ref, kvg_ref, wkvb_ref,
                     c32_ref, a32_ref, b32_ref, c64_ref, a64_ref, b64_ref, cm_ref, am_ref, bm_ref,
                     q_ref, k_ref, v_ref):
    proj = _project(x_ref, mod_ref, n1g_ref, w_in_ref)
    t32 = (c32_ref[...], a32_ref[...], b32_ref[...], DA_QK // 4)
    t64 = (c64_ref[...], a64_ref[...], b64_ref[...], SW_DIM // 4)
    tml = (cm_ref[...], am_ref[...], bm_ref[...], MLA_ROPE // 4)

    for j in range(2):
        q = _rope(_chunk(proj, j), *t32) * (DA_SCALE * LOG2E)
        q_ref[:, j * LANES:(j + 1) * LANES] = q.astype(BF16)
    for j in range(4):
        q = _rope(_chunk(proj, C_SWQ // LANES + j), *t64) * (SW_SCALE * LOG2E)
        q_ref[:, 256 + j * LANES:256 + (j + 1) * LANES] = q.astype(BF16)
    mq = jnp.dot(_rms(proj[:, C_QA:C_CKV], qg_ref[0]).astype(BF16), wqb_ref[0], preferred_element_type=F32)
    for h in range(MLA_HEADS):
        q = _rope(_chunk(mq, h), *tml) * (MLA_SCALE * LOG2E)
        q_ref[:, O_MQ + h * LANES:O_MQ + (h + 1) * LANES] = q.astype(BF16)

    for j in range(2):
        k_ref[:, j * LANES:(j + 1) * LANES] = _rope(_chunk(proj, C_DAK // LANES + j), *t32).astype(BF16)
    k_ref[:, 256:384] = _rope(proj[:, C_SWK:C_SWV], *t64).astype(BF16)
    ckv = _rms(proj[:, C_CKV:C_KR], kvg_ref[0])
    kv = jnp.dot(ckv.astype(BF16), wkvb_ref[0], preferred_element_type=F32)
    krz = _rope(proj[:, C_KR:P_EXT], *tml)
    for h in range(MLA_HEADS):
        k_ref[:, O_MK + h * LANES:O_MK + (h + 1) * LANES] = (_chunk(kv, h) + krz).astype(BF16)

    v_ref[:, 0:256] = proj[:, C_DAV:C_SWQ].astype(BF16)
    v_ref[:, 256:384] = proj[:, C_SWV:C_QA].astype(BF16)
    v_ref[:, 384:640] = kv[:, MLA_CHUNKS:].astype(BF16)


def _lat_project(l_idx, x, mod, p, tables):
    tiles_per_req = DEC_SEQ // TM_LAT
    tab_spec = pl.BlockSpec((TM_LAT, LANES), lambda i, l: (i % tiles_per_req, 0))
    grid_spec = pltpu.PrefetchScalarGridSpec(
        num_scalar_prefetch=1,
        grid=(N_LAT_TOK // TM_LAT,),
        in_specs=[
            pl.BlockSpec((TM_LAT, D_MODEL), lambda i, l: (i, 0)),
            pl.BlockSpec((1, 1, 6, D_MODEL), lambda i, l: (l[0], i // tiles_per_req, 0, 0)),
            _layer_spec(p["norm1_g"].shape),
            _layer_spec(p["w_in"].shape),
            _layer_spec(p["mla_q_norm_g"].shape),
            _layer_spec(p["mla_w_qb"].shape),
            _layer_spec(p["mla_kv_norm_g"].shape),
            _layer_spec(p["mla_w_kvb"].shape),
        ] + [tab_spec] * len(tables),
        out_specs=[
            pl.BlockSpec((TM_LAT, Q_W), lambda i, l: (i, 0)),
            pl.BlockSpec((TM_LAT, K_W), lambda i, l: (i, 0)),
            pl.BlockSpec((TM_LAT, V_W), lambda i, l: (i, 0)),
        ],
    )
    return pl.pallas_call(
        _lat_proj_kernel,
        grid_spec=grid_spec,
        out_shape=[jax.ShapeDtypeStruct((N_LAT_TOK, Q_W), BF16),
                   jax.ShapeDtypeStruct((N_LAT_TOK, K_W), BF16),
                   jax.ShapeDtypeStruct((N_LAT_TOK, V_W), BF16)],
        compiler_params=pltpu.CompilerParams(
            dimension_semantics=("arbitrary",), vmem_limit_bytes=VMEM_LIMIT),
        name="latent_project",
    )(l_idx, x, mod, p["norm1_g"], p["w_in"], p["mla_q_norm_g"], p["mla_w_qb"], p["mla_kv_norm_g"],
      p["mla_w_kvb"], *tables)


def _lat_attn_kernel(l_ref, q_ref, k_ref, v_ref, cdk_ref, cdv_ref, csk_ref, csv_ref, cckv_ref, ckr_ref,
                     wkvb_ref, pad_ref, lamv_ref, subg_ref, scal_ref,
                     mix_ref,
                     kda, vda, ksw, vsw, kml, vml):
    l = l_ref[0]
    qb = pl.program_id(1)

    @pl.when(qb == 0)
    def _stage_keys():
        kda[0:PAST_LEN, :] = cdk_ref[0, 0].astype(BF16)
        kda[PAST_LEN:N_KEYS, :] = k_ref[0, :, 0:256]
        vda[0:PAST_LEN, :] = cdv_ref[0, 0].astype(BF16)
        vda[PAST_LEN:N_KEYS, :] = v_ref[0, :, 0:256]
        ksw[0:PAST_LEN, :] = csk_ref[0, 0].astype(BF16)
        ksw[PAST_LEN:N_KEYS, :] = k_ref[0, :, 256:384]
        vsw[0:PAST_LEN, 0:LANES] = csv_ref[0, 0].astype(BF16)
        vsw[PAST_LEN:N_KEYS, 0:LANES] = v_ref[0, :, 256:384]
        vsw[:, LANES:2 * LANES] = jnp.ones((N_KEYS, LANES), BF16)
        kvc = jnp.dot(cckv_ref[0, 0].astype(BF16), wkvb_ref[0], preferred_element_type=F32)
        krz = jnp.dot(ckr_ref[0, 0].astype(BF16), pad_ref[...], preferred_element_type=F32)
        for h in range(MLA_HEADS):
            kml[0:PAST_LEN, h * LANES:(h + 1) * LANES] = (_chunk(kvc, h) + krz).astype(BF16)
            v_ctx = _chunk(kvc, MLA_HEADS + h // 2)
            v_lat = v_ref[0, :, 384 + (h // 2) * LANES:384 + (h // 2 + 1) * LANES].astype(F32)
            vml[0:PAST_LEN, h * LANES:(h + 1) * LANES] = _ones_in_other_half(v_ctx, h % 2).astype(BF16)
            vml[PAST_LEN:N_KEYS, h * LANES:(h + 1) * LANES] = _ones_in_other_half(v_lat, h % 2).astype(BF16)
        kml[PAST_LEN:N_KEYS, :] = k_ref[0, :, O_MK:K_W]

    lam = _diff_lambda(lamv_ref, scal_ref[l, S_LAM_INIT])
    da_o = _diff_attention(q_ref[0, :, 0:256].astype(F32), lambda: kda[...],
                           lambda c: vda[:, c * LANES:(c + 1) * LANES], lam)
    subg = subg_ref[0]
    for c, o in enumerate(da_o):
        mix_ref[0, :, c * LANES:(c + 1) * LANES] = _diff_subln(
            o, _chunk(subg, c), scal_ref[l, S_ONE_MINUS]).astype(BF16)

    q0 = qb * TQ
    start = pl.multiple_of(jnp.clip(q0 - WINDOW, 0, DEC_SEQ - SW_BAND), LANES)
    band = pl.ds(PAST_LEN + start, SW_BAND)
    k_sel = jnp.concatenate([ksw[0:PAST_LEN, :], ksw[band, :]], axis=0)
    v_sel = jnp.concatenate([vsw[0:PAST_LEN, :], vsw[band, :]], axis=0)
    n_sel = PAST_LEN + SW_BAND
    col = lax.broadcasted_iota(jnp.int32, (TQ, n_sel), 1)
    row = lax.broadcasted_iota(jnp.int32, (TQ, n_sel), 0)
    dist = (q0 + row) - (start + col - PAST_LEN)
    valid = (col < PAST_LEN) | ((dist <= WINDOW) & (dist >= -WINDOW))
    sinks = [scal_ref[l, h] for h in range(SW_HEADS)]
    sw_o = _swa_attention(q_ref[0, :, 256:768].astype(F32), k_sel, v_sel, sinks, valid)
    for c, o in enumerate(sw_o):
        mix_ref[0, :, 256 + c * LANES:256 + (c + 1) * LANES] = o.astype(BF16)

    mla_o = _mla_attention(lambda h: q_ref[0, :, O_MQ + h * LANES:O_MQ + (h + 1) * LANES],
                           lambda h: kml[:, h * LANES:(h + 1) * LANES],
                           lambda h: vml[:, h * LANES:(h + 1) * LANES])
    for c, o in enumerate(mla_o):
        mix_ref[0, :, 768 + c * LANES:768 + (c + 1) * LANES] = o.astype(BF16)


def _lat_attend(l_idx, q, k, v, caches, p, pad_eye):
    cdk, cdv, csk, csv, cckv, ckr = caches

    def cache_spec(a):
        return pl.BlockSpec((1, 1) + a.shape[2:], lambda b, j, l: (b, l[0], 0, 0))

    grid_spec = pltpu.PrefetchScalarGridSpec(
        num_scalar_prefetch=1,
        grid=(DEC_BATCH, DEC_SEQ // TQ),
        in_specs=[
            pl.BlockSpec((1, TQ, Q_W), lambda b, j, l: (b, j, 0)),
            pl.BlockSpec((1, DEC_SEQ, K_W), lambda b, j, l: (b, 0, 0)),
            pl.BlockSpec((1, DEC_SEQ, V_W), lambda b, j, l: (b, 0, 0)),
            cache_spec(cdk), cache_spec(cdv), cache_spec(csk), cache_spec(csv), cache_spec(cckv), cache_spec(ckr),
            _layer_spec(p["mla_w_kvb"].shape),
            pl.BlockSpec(pad_eye.shape, lambda b, j, l: (0, 0)),
            _layer_spec(p["lamv"].shape),
            _layer_spec(p["subg"].shape),
            pl.BlockSpec(memory_space=pltpu.SMEM),
        ],
        out_specs=pl.BlockSpec((1, TQ, D_MODEL), lambda b, j, l: (b, j, 0)),
        scratch_shapes=[
            pltpu.VMEM((N_KEYS, 256), BF16), pltpu.VMEM((N_KEYS, 256), BF16),
            pltpu.VMEM((N_KEYS, 128), BF16), pltpu.VMEM((N_KEYS, 256), BF16),
            pltpu.VMEM((N_KEYS, MLA_CHUNKS), BF16), pltpu.VMEM((N_KEYS, MLA_CHUNKS), BF16),
        ],
    )
    return pl.pallas_call(
        _lat_attn_kernel,
        grid_spec=grid_spec,
        out_shape=jax.ShapeDtypeStruct((DEC_BATCH, DEC_SEQ, D_MODEL), BF16),
        compiler_params=pltpu.CompilerParams(
            dimension_semantics=("arbitrary", "arbitrary"), vmem_limit_bytes=VMEM_LIMIT),
        name="latent_attend",
    )(l_idx, q.reshape(DEC_BATCH, DEC_SEQ, Q_W), k.reshape(DEC_BATCH, DEC_SEQ, K_W),
      v.reshape(DEC_BATCH, DEC_SEQ, V_W), cdk, cdv, csk, csv, cckv, ckr,
      p["mla_w_kvb"], pad_eye, p["lamv"], p["subg"], p["scal"])


def _mlp_kernel(l_ref, x_ref, mix_ref, mod_ref, n2g_ref, w_out_ref, w_up_ref, w_down_ref, fg_ref, o_ref,
                *, final_norm):
    m = mod_ref[0, 0]
    x = x_ref[...] + m[2:3] * jnp.dot(mix_ref[...], w_out_ref[0], preferred_element_type=F32)
    h = (_rms(x, n2g_ref[0]) * (1.0 + m[4:5]) + m[3:4]).astype(BF16)
    acc = jnp.zeros_like(x)
    for c in range(D_FF // FF_CHUNK):
        u = jnp.dot(h, w_up_ref[0, :, c * FF_CHUNK:(c + 1) * FF_CHUNK], preferred_element_type=F32)
        a = jnp.square(jnp.maximum(u, 0.0)).astype(BF16)
        acc = acc + jnp.dot(a, w_down_ref[0, c * FF_CHUNK:(c + 1) * FF_CHUNK, :], preferred_element_type=F32)
    x = x + m[5:6] * acc
    if final_norm:
        x = _rms(x, fg_ref[...])
    o_ref[...] = x


def _mlp_layer(l_idx, x, mix, mod, p, final_g, mod_row, final_norm):
    n_tok = x.shape[0]
    grid_spec = pltpu.PrefetchScalarGridSpec(
        num_scalar_prefetch=1,
        grid=(n_tok // TM_MLP,),
        in_specs=[
            pl.BlockSpec((TM_MLP, D_MODEL), lambda i, l: (i, 0)),
            pl.BlockSpec((TM_MLP, D_MODEL), lambda i, l: (i, 0)),
            pl.BlockSpec((1, 1, 6, D_MODEL), lambda i, l: (l[0], mod_row(i), 0, 0)),
            _layer_spec(p["norm2_g"].shape),
            _layer_spec(p["w_out"].shape),
            _layer_spec(p["w_up"].shape),
            _layer_spec(p["w_down"].shape),
            pl.BlockSpec((1, D_MODEL), lambda i, l: (0, 0)),
        ],
        out_specs=pl.BlockSpec((TM_MLP, D_MODEL), lambda i, l: (i, 0)),
    )
    return pl.pallas_call(
        functools.partial(_mlp_kernel, final_norm=final_norm),
        grid_spec=grid_spec,
        out_shape=jax.ShapeDtypeStruct((n_tok, D_MODEL), F32),
        compiler_params=pltpu.CompilerParams(
            dimension_semantics=("arbitrary",), vmem_limit_bytes=VMEM_LIMIT),
        name="out_proj_mlp",
    )(l_idx, x, mix, mod, p["norm2_g"], p["w_out"], p["w_up"], p["w_down"], final_g)


def _rope_tables(period, active_lanes=LANES):
    m = period // 2
    half = m // 2
    t = np.arange(DEC_SEQ)
    pos = np.stack([t // GRID_W, t % GRID_W], axis=0).astype(np.float64)
    lane = np.arange(LANES) % period
    axis = lane // m
    k = lane % m
    freq = ROPE_BASE ** (-(k % half).astype(np.float64) / half)
    ang = pos[axis, :].T * freq[None, :]
    first = (k < half)[None, :]
    active = (np.arange(LANES) < active_lanes)[None, :]
    cos, sin = np.where(active, np.cos(ang), 1.0), np.where(active, np.sin(ang), 0.0)
    return (jnp.asarray(cos, F32), jnp.asarray(np.where(first, -sin, 0.0), F32),
            jnp.asarray(np.where(first, 0.0, sin), F32))


def _paired_head_order():
    order = []
    for c in range(SW_HEADS // 2):
        order += [c, c + SW_HEADS // 2]
    return order


def _prepare_params(norm1_g, norm2_g, w_in, w_out, lq1, lk1, lq2, lk2, diff_subln_g, swa_sink,
                    mla_q_norm_g, mla_w_qb, mla_kv_norm_g, mla_w_kvb, w_up, w_down):
    heads = _paired_head_order()
    o_swq = 768
    swq_cols = [w_in[:, :, o_swq + h * SW_DIM:o_swq + (h + 1) * SW_DIM] for h in heads]
    kr_cols = [w_in[:, :, 1920:1952], jnp.zeros((DEPTH, D_MODEL, LANES - MLA_ROPE), w_in.dtype)]
    w_in_ext = jnp.concatenate([w_in[:, :, 0:768]] + swq_cols + [w_in[:, :, 1280:1920]] + kr_cols,
                               axis=-1).astype(BF16)
    o_swo = DA_HEADS * DA_V
    swo_rows = [w_out[:, o_swo + h * SW_DIM:o_swo + (h + 1) * SW_DIM, :] for h in heads]
    w_out_p = jnp.concatenate([w_out[:, 0:o_swo, :]] + swo_rows + [w_out[:, o_swo + SW_HEADS * SW_DIM:, :]],
                              axis=1).astype(BF16)
    dq = MLA_NOPE + MLA_ROPE
    gap = LANES - dq
    zq = jnp.zeros((DEPTH, MLA_Q_RANK, gap), mla_w_qb.dtype)
    wqb = jnp.concatenate(
        [blk for h in range(MLA_HEADS)
         for blk in (mla_w_qb[:, :, h * dq + MLA_NOPE:(h + 1) * dq], zq, mla_w_qb[:, :, h * dq:h * dq + MLA_NOPE])],
        axis=-1).astype(BF16)
    dkv = MLA_NOPE + MLA_V
    zk = jnp.zeros((DEPTH, MLA_KV_RANK, LANES - MLA_NOPE), mla_w_kvb.dtype)
    wkvb = jnp.concatenate(
        [blk for h in range(MLA_HEADS) for blk in (zk, mla_w_kvb[:, :, h * dkv:h * dkv + MLA_NOPE])]
        + [mla_w_kvb[:, :, h * dkv + MLA_NOPE:(h + 1) * dkv] for h in range(MLA_HEADS)],
        axis=-1).astype(BF16)
    lam_init = np.array([0.8 - 0.6 * math.exp(-0.3 * l) for l in range(DEPTH)], np.float32)
    scal = jnp.concatenate([
        swa_sink.astype(F32) * LOG2E,
        jnp.asarray(np.stack([lam_init, (1.0 - lam_init.astype(np.float64)).astype(np.float32)], axis=1)),
        jnp.zeros((DEPTH, S_COLS - SW_HEADS - 2), F32)], axis=1)
    return {
        "norm1_g": norm1_g.reshape(DEPTH, 1, D_MODEL),
        "norm2_g": norm2_g.reshape(DEPTH, 1, D_MODEL),
        "w_in": w_in_ext,
        "w_out": w_out_p,
        "mla_q_norm_g": mla_q_norm_g.reshape(DEPTH, 1, MLA_Q_RANK),
        "mla_w_qb": wqb,
        "mla_kv_norm_g": mla_kv_norm_g.reshape(DEPTH, 1, MLA_KV_RANK),
        "mla_w_kvb": wkvb,
        "lamv": jnp.stack([lq1, lk1, lq2, lk2], axis=1),
        "subg": jnp.tile(diff_subln_g, (1, DA_HEADS)).reshape(DEPTH, 1, DA_HEADS * DA_V),
        "scal": scal,
        "w_up": w_up.astype(BF16),
        "w_down": w_down.astype(BF16),
    }


def kernel(x_prompt, x_sample, c, cache_diff_k, cache_diff_v, cache_swa_k, cache_swa_v, cache_mla_ckv,
           cache_mla_krope, c_ctx, norm1_g, norm2_g, w_ada, b_ada, w_in, w_out, diff_lambda_q1,
           diff_lambda_k1, diff_lambda_q2, diff_lambda_k2, diff_subln_g, swa_sink, mla_q_norm_g,
           mla_w_qb, mla_kv_norm_g, mla_w_kvb, w_up, w_down, final_g):
    p = _prepare_params(norm1_g, norm2_g, w_in, w_out, diff_lambda_q1, diff_lambda_k1, diff_lambda_q2,
                        diff_lambda_k2, diff_subln_g, swa_sink, mla_q_norm_g, mla_w_qb, mla_kv_norm_g,
                        mla_w_kvb, w_up, w_down)
    cvecs = jnp.concatenate([c, c_ctx[None, :], jnp.zeros((8 - DEC_BATCH - 1, D_MODEL), F32)], axis=0)
    mod = _modulation(cvecs, w_ada, b_ada)
    tables = _rope_tables(DA_QK) + _rope_tables(SW_DIM) + _rope_tables(MLA_ROPE, MLA_ROPE)
    pad_eye = jnp.asarray(np.eye(MLA_ROPE, LANES, dtype=np.float32), BF16)
    final_g2 = final_g.reshape(1, D_MODEL)
    caches = (cache_diff_k.reshape(DEC_BATCH, DEPTH, PAST_LEN, DA_HEADS * 2 * DA_QK),
              cache_diff_v.reshape(DEC_BATCH, DEPTH, PAST_LEN, DA_HEADS * DA_V),
              cache_swa_k.reshape(DEC_BATCH, DEPTH, PAST_LEN, SW_KV_HEADS * SW_DIM),
              cache_swa_v.reshape(DEC_BATCH, DEPTH, PAST_LEN, SW_KV_HEADS * SW_DIM),
              cache_mla_ckv, cache_mla_krope)

    x_ctx = x_prompt.reshape(N_CTX_TOK, D_MODEL)
    x_lat = x_sample.reshape(N_LAT_TOK, D_MODEL)
    new = [jnp.zeros((BATCH, DEPTH, SEQ, w), F32) for w in CACHE_WIDTHS]
    lat_tiles_per_req = DEC_SEQ // TM_MLP
    for l in range(DEPTH):
        l_idx = jnp.full((1,), l, jnp.int32)
        last = l == DEPTH - 1
        *new, mix_ctx = _ctx_layer(l_idx, x_ctx, mod, p, new)
        x_ctx = _mlp_layer(l_idx, x_ctx, mix_ctx, mod, p, final_g2, lambda i: CTX_MOD_ROW, last)
        q, k, v = _lat_project(l_idx, x_lat, mod, p, tables)
        mix_lat = _lat_attend(l_idx, q, k, v, caches, p, pad_eye).reshape(N_LAT_TOK, D_MODEL)
        x_lat = _mlp_layer(l_idx, x_lat, mix_lat, mod, p, final_g2, lambda i: i // lat_tiles_per_req, last)

    y_prompt = x_ctx.reshape(BATCH, SEQ, D_MODEL)
    y_sample = x_lat.reshape(DEC_BATCH, DEC_SEQ, D_MODEL)
    new_dk, new_dv, new_sk, new_sv, new_ckv, new_kr = new
    return (y_prompt, y_sample,
            new_dk.reshape(BATCH, DEPTH, SEQ, DA_HEADS, 2 * DA_QK),
            new_dv.reshape(BATCH, DEPTH, SEQ, DA_HEADS, DA_V),
            new_sk.reshape(BATCH, DEPTH, SEQ, SW_KV_HEADS, SW_DIM),
            new_sv.reshape(BATCH, DEPTH, SEQ, SW_KV_HEADS, SW_DIM),
            new_ckv, new_kr)
```

```python
import functools
import math

import numpy as np
import jax
import jax.numpy as jnp
from jax import lax
from jax.experimental import pallas as pl
from jax.experimental.pallas import tpu as pltpu

F32 = jnp.float32
BF16 = jnp.bfloat16

D_MODEL = 1024
BATCH = 32
SEQ = 256
DEPTH = 4
DEC_BATCH = 4
DEC_SEQ = 2048
PAST_LEN = 256
GRID_W = 64
WINDOW = 128
ROPE_BASE = 10000.0
EPS = 1e-6
DA_HEADS = 4
DA_QK = 32
DA_V = 64
SW_HEADS = 8
SW_KV_HEADS = 2
SW_DIM = 64
MLA_HEADS = 4
MLA_Q_RANK = 256
MLA_KV_RANK = 128
MLA_NOPE = 64
MLA_ROPE = 32
MLA_V = 64
D_FF = 4 * D_MODEL

LANES = 128
N_CTX_TOK = BATCH * SEQ
N_LAT_TOK = DEC_BATCH * DEC_SEQ
N_KEYS = PAST_LEN + DEC_SEQ
CTX_MOD_ROW = DEC_BATCH

C_DAQ, C_DAK, C_DAV, C_SWQ, C_SWK, C_SWV, C_QA, C_CKV, C_KR, P_EXT = (
    0, 256, 512, 768, 1280, 1408, 1536, 1792, 1920, 2048)
MLA_CHUNKS = MLA_HEADS * 128
Q_W = 256 + 512 + MLA_CHUNKS
K_W = 256 + 128 + MLA_CHUNKS
V_W = 256 + 128 + 256
O_MQ = 768
O_MK = 384
LOG2E = 1.4426950408889634

DA_SCALE = DA_QK ** -0.5
SW_SCALE = SW_DIM ** -0.5
MLA_SCALE = (MLA_NOPE + MLA_ROPE) ** -0.5

TM_CTX = SEQ
TM_LAT = 512
TM_MLP = 512
TQ = 256
SW_BAND = TQ + 2 * WINDOW
FF_CHUNK = 1024
MOD_TILE = 1536
VMEM_LIMIT = 56 * 1024 * 1024

_NT = (((1,), (1,)), ((), ()))

S_LAM_INIT = SW_HEADS
S_ONE_MINUS = SW_HEADS + 1
S_SINK_ABSMAX = SW_HEADS + 2
S_COLS = 16

NOSHIFT_BOUND = 50.0
G_DIFF, G_SWA, G_MLA = 0, 8, 16


def _rms(x, g):
    ms = jnp.mean(x * x, axis=-1, keepdims=True)
    return x * lax.rsqrt(ms + EPS) * g


def _lane_mask(n, lo, width):
    lane = lax.broadcasted_iota(jnp.int32, (1, n), 1)
    return (lane >= lo) & (lane < lo + width)


def _exp_scores(s, shift, sink=None):
    if not shift:
        return jnp.exp2(s), jnp.zeros((1, 1), F32)
    m = jnp.max(s, axis=-1, keepdims=True)
    if sink is not None:
        m = jnp.maximum(m, sink)
    return jnp.exp2(s - m), m


def _sq_bf16(x):
    xf = x.astype(F32)
    return (xf * xf).astype(BF16)


def _logit_bound_sq(q_all, gq, kmax_sq):
    qn = jnp.dot(_sq_bf16(q_all), gq, preferred_element_type=F32)
    return jnp.max(qn, axis=0, keepdims=True) * kmax_sq * 1.03


def _chunk(a, j):
    return a[:, j * LANES:(j + 1) * LANES]


def _project(x_ref, mod_ref, n1g_ref, w_in_ref):
    m = mod_ref[0, 0]
    h = _rms(x_ref[...], n1g_ref[0]) * (1.0 + m[1:2]) + m[0:1]
    return jnp.dot(h.astype(BF16), w_in_ref[0], preferred_element_type=F32)


def _diff_lambda(lamv_ref, lam_init):
    lv = lamv_ref[0]
    a = jnp.sum(lv[0:1] * lv[1:2], axis=-1, keepdims=True)
    b = jnp.sum(lv[2:3] * lv[3:4], axis=-1, keepdims=True)
    return jnp.exp(a) - jnp.exp(b) + lam_init


def _half_mask(par):
    lane = lax.broadcasted_iota(jnp.int32, (1, LANES), 1)
    return (lane >= par * 64) & (lane < (par + 1) * 64)


def _diff_attention(qf, load_k, load_v, lam, shift):
    width = DA_HEADS * DA_V
    chunks = []
    for c in range(DA_HEADS // 2):
        o_par = []
        for par in range(2):
            lo = (2 * c + par) * DA_V
            q1 = jnp.where(_lane_mask(width, lo, DA_QK), qf, 0.0).astype(BF16)
            q2 = jnp.where(_lane_mask(width, lo + DA_QK, DA_QK), qf, 0.0).astype(BF16)
            e1, _ = _exp_scores(lax.dot_general(q1, load_k(), _NT, preferred_element_type=F32), shift)
            e2, _ = _exp_scores(lax.dot_general(q2, load_k(), _NT, preferred_element_type=F32), shift)
            d1 = jnp.sum(e1, axis=-1, keepdims=True)
            d2 = jnp.sum(e2, axis=-1, keepdims=True)
            w = (e1 - e2 * (lam * d1 * (1.0 / d2))).astype(BF16)
            o_par.append(jnp.dot(w, load_v(c), preferred_element_type=F32) * (1.0 / d1))
        chunks.append(jnp.where(_half_mask(0), o_par[0], o_par[1]))
    return chunks


def _diff_subln(o, subg, one_minus_lam_init):
    sq = o * o
    r = None
    for par in range(2):
        ms = jnp.sum(jnp.where(_half_mask(par), sq, 0.0), axis=-1, keepdims=True) * (1.0 / DA_V)
        rs = lax.rsqrt(ms + EPS)
        r = rs if r is None else jnp.where(_half_mask(0), r, rs)
    return o * r * subg * one_minus_lam_init


def _swa_attention(qf, k, v_ext, sinks, shift, valid=None):
    outs = []
    for c in range(SW_HEADS // 2):
        qc = _chunk(qf, c)
        o_par = []
        for par in range(2):
            h = c + (SW_HEADS // 2) * par
            qm = jnp.where(_half_mask(par), qc, 0.0).astype(BF16)
            s = lax.dot_general(qm, k, _NT, preferred_element_type=F32)
            if valid is not None:
                s = jnp.where(valid, s, -1e30)
            e, m = _exp_scores(s, shift, sink=sinks[h])
            o = jnp.dot(e.astype(BF16), v_ext, preferred_element_type=F32)
            d = _chunk(o, 1) + jnp.exp2(sinks[h] - m)
            o_par.append(_chunk(o, 0) * (1.0 / d))
        outs.append(jnp.where(_half_mask(0), o_par[0], o_par[1]))
    return outs


def _mla_attention(load_q, load_k, load_v, shift):
    chunks = []
    for c in range(MLA_HEADS // 2):
        o_par = []
        for par in range(2):
            h = 2 * c + par
            e, _ = _exp_scores(lax.dot_general(load_q(h), load_k(h), _NT, preferred_element_type=F32), shift)
            o = jnp.dot(e.astype(BF16), load_v(h), preferred_element_type=F32)
            o_par.append(o * (1.0 / pltpu.roll(o, 64, 1)))
        chunks.append(jnp.where(_half_mask(0), o_par[0], o_par[1]))
    return chunks


def _ones_in_other_half(v, par):
    return jnp.where(_half_mask(par), v, 1.0)


def _mod_kernel(c_ref, w_ref, b_ref, o_ref):
    c = c_ref[...]
    s = c * (1.0 / (1.0 + jnp.exp(-c)))
    o_ref[0] = jnp.dot(s.astype(BF16), w_ref[0].astype(BF16), preferred_element_type=F32) + b_ref[0]


def _modulation(cvecs, w_ada, b_ada):
    n_out = 6 * D_MODEL
    out = pl.pallas_call(
        _mod_kernel,
        grid=(DEPTH, n_out // MOD_TILE),
        in_specs=[
            pl.BlockSpec((8, D_MODEL), lambda l, j: (0, 0)),
            pl.BlockSpec((1, D_MODEL, MOD_TILE), lambda l, j: (l, 0, j)),
            pl.BlockSpec((1, 1, MOD_TILE), lambda l, j: (l, 0, j)),
        ],
        out_specs=pl.BlockSpec((1, 8, MOD_TILE), lambda l, j: (l, 0, j)),
        out_shape=jax.ShapeDtypeStruct((DEPTH, 8, n_out), F32),
        compiler_params=pltpu.CompilerParams(
            dimension_semantics=("arbitrary", "arbitrary"), vmem_limit_bytes=VMEM_LIMIT),
        name="adaln_modulation",
    )(cvecs, w_ada, b_ada.reshape(DEPTH, 1, n_out))
    return out.reshape(DEPTH, 8, 6, D_MODEL)


def _ctx_kernel(l_ref, x_ref, mod_ref, n1g_ref, w_in_ref, qg_ref, wqb_ref, kvg_ref, wkvb_ref,
                lamv_ref, subg_ref, gq_ref, gk_ref, scal_ref, *rest):
    dk_ref, dv_ref, sk_ref, sv_ref, ckv_ref, kr_ref, mix_ref = rest[-7:]
    l = l_ref[0]
    proj = _project(x_ref, mod_ref, n1g_ref, w_in_ref)
    da_k = proj[:, C_DAK:C_DAV]
    da_v = proj[:, C_DAV:C_SWQ]
    sw_k = proj[:, C_SWK:C_SWV]
    sw_v = proj[:, C_SWV:C_QA]
    ckv = _rms(proj[:, C_CKV:C_KR], kvg_ref[0])
    krz = proj[:, C_KR:P_EXT]
    dk_ref[0, 0] = da_k
    dv_ref[0, 0] = da_v
    sk_ref[0, 0] = sw_k
    sv_ref[0, 0] = sw_v
    ckv_ref[0, 0] = ckv
    kr_ref[0, 0] = krz[:, 0:MLA_ROPE]

    da_q = proj[:, C_DAQ:C_DAK] * (DA_SCALE * LOG2E)
    sw_q = proj[:, C_SWQ:C_SWK] * (SW_SCALE * LOG2E)
    mq = jnp.dot(_rms(proj[:, C_QA:C_CKV], qg_ref[0]).astype(BF16), wqb_ref[0], preferred_element_type=F32)
    mq = mq * (MLA_SCALE * LOG2E)
    kv = jnp.dot(ckv.astype(BF16), wkvb_ref[0], preferred_element_type=F32)
    mk = [_chunk(kv, h) + krz for h in range(MLA_HEADS)]

    k_all = jnp.concatenate([da_k, sw_k] + mk, axis=-1)
    kmax_sq = jnp.max(jnp.dot(_sq_bf16(k_all), gk_ref[...], preferred_element_type=F32), axis=0, keepdims=True)
    bound_sq = _logit_bound_sq(jnp.concatenate([da_q, sw_q, mq], axis=-1), gq_ref[...], kmax_sq)
    small = (jnp.max(bound_sq) <= NOSHIFT_BOUND ** 2) & (scal_ref[l, S_SINK_ABSMAX] <= NOSHIFT_BOUND)

    lam = _diff_lambda(lamv_ref, scal_ref[l, S_LAM_INIT])
    subg = subg_ref[0]
    sinks = [scal_ref[l, h] for h in range(SW_HEADS)]

    def attend(shift):
        da_kb = da_k.astype(BF16)
        da_o = _diff_attention(da_q, lambda: da_kb, lambda c: _chunk(da_v, c).astype(BF16), lam, shift)
        for c, o in enumerate(da_o):
            mix_ref[:, c * LANES:(c + 1) * LANES] = _diff_subln(
                o, _chunk(subg, c), scal_ref[l, S_ONE_MINUS]).astype(BF16)
        sw_v_ext = jnp.concatenate([sw_v, jnp.ones_like(sw_v)], axis=-1).astype(BF16)
        sw_o = _swa_attention(sw_q, sw_k.astype(BF16), sw_v_ext, sinks, shift)
        for c, o in enumerate(sw_o):
            mix_ref[:, 256 + c * LANES:256 + (c + 1) * LANES] = o.astype(BF16)
        mla_o = _mla_attention(
            lambda h: _chunk(mq, h).astype(BF16), lambda h: mk[h].astype(BF16),
            lambda h: _ones_in_other_half(_chunk(kv, MLA_HEADS + h // 2), h % 2).astype(BF16), shift)
        for c, o in enumerate(mla_o):
            mix_ref[:, 768 + c * LANES:768 + (c + 1) * LANES] = o.astype(BF16)

    pl.when(small)(functools.partial(attend, False))
    pl.when(jnp.logical_not(small))(functools.partial(attend, True))


def _layer_spec(shape):
    nd = len(shape)
    return pl.BlockSpec((1,) + tuple(shape[1:]), lambda *a: (a[-1][0],) + (0,) * (nd - 1))


CACHE_WIDTHS = (DA_HEADS * 2 * DA_QK, DA_HEADS * DA_V, SW_KV_HEADS * SW_DIM, SW_KV_HEADS * SW_DIM,
                MLA_KV_RANK, MLA_ROPE)


def _ctx_layer(l_idx, x, mod, p, new_caches):
    n_req = BATCH
    out_shape = [jax.ShapeDtypeStruct(c.shape, c.dtype) for c in new_caches]
    out_shape.append(jax.ShapeDtypeStruct((N_CTX_TOK, D_MODEL), BF16))
    out_specs = [pl.BlockSpec((1, 1, SEQ, w), lambda i, l: (i, l[0], 0, 0)) for w in CACHE_WIDTHS]
    out_specs.append(pl.BlockSpec((TM_CTX, D_MODEL), lambda i, l: (i, 0)))
    n_fixed_inputs = 14
    aliases = {n_fixed_inputs + j: j for j in range(len(new_caches))}
    grid_spec = pltpu.PrefetchScalarGridSpec(
        num_scalar_prefetch=1,
        grid=(n_req,),
        in_specs=[
            pl.BlockSpec((TM_CTX, D_MODEL), lambda i, l: (i, 0)),
            pl.BlockSpec((1, 1, 6, D_MODEL), lambda i, l: (l[0], CTX_MOD_ROW, 0, 0)),
            _layer_spec(p["norm1_g"].shape),
            _layer_spec(p["w_in"].shape),
            _layer_spec(p["mla_q_norm_g"].shape),
            _layer_spec(p["mla_w_qb"].shape),
            _layer_spec(p["mla_kv_norm_g"].shape),
            _layer_spec(p["mla_w_kvb"].shape),
            _layer_spec(p["lamv"].shape),
            _layer_spec(p["subg"].shape),
            pl.BlockSpec(p["gq"].shape, lambda i, l: (0, 0)),
            pl.BlockSpec(p["gk"].shape, lambda i, l: (0, 0)),
            pl.BlockSpec(memory_space=pltpu.SMEM),
        ] + [pl.BlockSpec(memory_space=pl.ANY)] * len(new_caches),
        out_specs=out_specs,
    )
    return pl.pallas_call(
        _ctx_kernel,
        grid_spec=grid_spec,
        out_shape=out_shape,
        input_output_aliases=aliases,
        compiler_params=pltpu.CompilerParams(
            dimension_semantics=("arbitrary",), vmem_limit_bytes=VMEM_LIMIT),
        name="ctx_project_attend",
    )(l_idx, x, mod, p["norm1_g"], p["w_in"], p["mla_q_norm_g"], p["mla_w_qb"], p["mla_kv_norm_g"],
      p["mla_w_kvb"], p["lamv"], p["subg"], p["gq"], p["gk"], p["scal"], *new_caches)


def _rope(x, cos, sa, sb, half):
    return x * cos + pltpu.roll(x, LANES - half, 1) * sa + pltpu.roll(x, half, 1) * sb


def _lat_proj_kernel(l_ref, x_ref, mod_ref, n1g_ref, w_in_ref, qg_ref, wqb_ref, kvg_ref, wkvb_ref,
                     c32_ref, a32_ref, b32_ref, c64_ref, a64_ref, b64_ref, cm_ref, am_ref, bm_ref,
                     q_ref, k_ref, v_ref):
    proj = _project(x_ref, mod_ref, n1g_ref, w_in_ref)
    t32 = (c32_ref[...], a32_ref[...], b32_ref[...], DA_QK // 4)
    t64 = (c64_ref[...], a64_ref[...], b64_ref[...], SW_DIM // 4)
    tml = (cm_ref[...], am_ref[...], bm_ref[...], MLA_ROPE // 4)

    for j in range(2):
        q = _rope(_chunk(proj, j), *t32) * (DA_SCALE * LOG2E)
        q_ref[:, j * LANES:(j + 1) * LANES] = q.astype(BF16)
    for j in range(4):
        q = _rope(_chunk(proj, C_SWQ // LANES + j), *t64) * (SW_SCALE * LOG2E)
        q_ref[:, 256 + j * LANES:256 + (j + 1) * LANES] = q.astype(BF16)
    mq = jnp.dot(_rms(proj[:, C_QA:C_CKV], qg_ref[0]).astype(BF16), wqb_ref[0], preferred_element_type=F32)
    for h in range(MLA_HEADS):
        q = _rope(_chunk(mq, h), *tml) * (MLA_SCALE * LOG2E)
        q_ref[:, O_MQ + h * LANES:O_MQ + (h + 1) * LANES] = q.astype(BF16)

    for j in range(2):
        k_ref[:, j * LANES:(j + 1) * LANES] = _rope(_chunk(proj, C_DAK // LANES + j), *t32).astype(BF16)
    k_ref[:, 256:384] = _rope(proj[:, C_SWK:C_SWV], *t64).astype(BF16)
    ckv = _rms(proj[:, C_CKV:C_KR], kvg_ref[0])
    kv = jnp.dot(ckv.astype(BF16), wkvb_ref[0], preferred_element_type=F32)
    krz = _rope(proj[:, C_KR:P_EXT], *tml)
    for h in range(MLA_HEADS):
        k_ref[:, O_MK + h * LANES:O_MK + (h + 1) * LANES] = (_chunk(kv, h) + krz).astype(BF16)

    v_ref[:, 0:256] = proj[:, C_DAV:C_SWQ].astype(BF16)
    v_ref[:, 256:384] = proj[:, C_SWV:C_QA].astype(BF16)
    v_ref[:, 384:640] = kv[:, MLA_CHUNKS:].astype(BF16)


def _lat_project(l_idx, x, mod, p, tables):
    tiles_per_req = DEC_SEQ // TM_LAT
    tab_spec = pl.BlockSpec((TM_LAT, LANES), lambda i, l: (i % tiles_per_req, 0))
    grid_spec = pltpu.PrefetchScalarGridSpec(
        num_scalar_prefetch=1,
        grid=(N_LAT_TOK // TM_LAT,),
        in_specs=[
            pl.BlockSpec((TM_LAT, D_MODEL), lambda i, l: (i, 0)),
            pl.BlockSpec((1, 1, 6, D_MODEL), lambda i, l: (l[0], i // tiles_per_req, 0, 0)),
            _layer_spec(p["norm1_g"].shape),
            _layer_spec(p["w_in"].shape),
            _layer_spec(p["mla_q_norm_g"].shape),
            _layer_spec(p["mla_w_qb"].shape),
            _layer_spec(p["mla_kv_norm_g"].shape),
            _layer_spec(p["mla_w_kvb"].shape),
        ] + [tab_spec] * len(tables),
        out_specs=[
            pl.BlockSpec((TM_LAT, Q_W), lambda i, l: (i, 0)),
            pl.BlockSpec((TM_LAT, K_W), lambda i, l: (i, 0)),
            pl.BlockSpec((TM_LAT, V_W), lambda i, l: (i, 0)),
        ],
    )
    return pl.pallas_call(
        _lat_proj_kernel,
        grid_spec=grid_spec,
        out_shape=[jax.ShapeDtypeStruct((N_LAT_TOK, Q_W), BF16),
                   jax.ShapeDtypeStruct((N_LAT_TOK, K_W), BF16),
                   jax.ShapeDtypeStruct((N_LAT_TOK, V_W), BF16)],
        compiler_params=pltpu.CompilerParams(
            dimension_semantics=("arbitrary",), vmem_limit_bytes=VMEM_LIMIT),
        name="latent_project",
    )(l_idx, x, mod, p["norm1_g"], p["w_in"], p["mla_q_norm_g"], p["mla_w_qb"], p["mla_kv_norm_g"],
      p["mla_w_kvb"], *tables)


def _lat_attn_kernel(l_ref, q_ref, k_ref, v_ref, cdk_ref, cdv_ref, csk_ref, csv_ref, cckv_ref, ckr_ref,
                     wkvb_ref, pad_ref, lamv_ref, subg_ref, gq_ref, gk_ref, scal_ref,
                     mix_ref,
                     kda, vda, ksw, vsw, kml, vml, kmax_sq):
    l = l_ref[0]
    qb = pl.program_id(1)

    @pl.when(qb == 0)
    def _stage_keys():
        kda[0:PAST_LEN, :] = cdk_ref[0, 0].astype(BF16)
        kda[PAST_LEN:N_KEYS, :] = k_ref[0, :, 0:256]
        vda[0:PAST_LEN, :] = cdv_ref[0, 0].astype(BF16)
        vda[PAST_LEN:N_KEYS, :] = v_ref[0, :, 0:256]
        ksw[0:PAST_LEN, :] = csk_ref[0, 0].astype(BF16)
        ksw[PAST_LEN:N_KEYS, :] = k_ref[0, :, 256:384]
        vsw[0:PAST_LEN, 0:LANES] = csv_ref[0, 0].astype(BF16)
        vsw[PAST_LEN:N_KEYS, 0:LANES] = v_ref[0, :, 256:384]
        vsw[:, LANES:2 * LANES] = jnp.ones((N_KEYS, LANES), BF16)
        kvc = jnp.dot(cckv_ref[0, 0].astype(BF16), wkvb_ref[0], preferred_element_type=F32)
        krz = jnp.dot(ckr_ref[0, 0].astype(BF16), pad_ref[...], preferred_element_type=F32)
        for h in range(MLA_HEADS):
            kml[0:PAST_LEN, h * LANES:(h + 1) * LANES] = (_chunk(kvc, h) + krz).astype(BF16)
            v_ctx = _chunk(kvc, MLA_HEADS + h // 2)
            v_lat = v_ref[0, :, 384 + (h // 2) * LANES:384 + (h // 2 + 1) * LANES].astype(F32)
            vml[0:PAST_LEN, h * LANES:(h + 1) * LANES] = _ones_in_other_half(v_ctx, h % 2).astype(BF16)
            vml[PAST_LEN:N_KEYS, h * LANES:(h + 1) * LANES] = _ones_in_other_half(v_lat, h % 2).astype(BF16)
        kml[PAST_LEN:N_KEYS, :] = k_ref[0, :, O_MK:K_W]
        gk = gk_ref[...]
        kn = (jnp.dot(_sq_bf16(kda[...]), gk[0:256], preferred_element_type=F32)
              + jnp.dot(_sq_bf16(ksw[...]), gk[256:O_MK], preferred_element_type=F32)
              + jnp.dot(_sq_bf16(kml[...]), gk[O_MK:K_W], preferred_element_type=F32))
        kmax_sq[...] = jnp.max(kn, axis=0, keepdims=True)

    bound_sq = _logit_bound_sq(q_ref[0], gq_ref[...], kmax_sq[...])
    small = (jnp.max(bound_sq) <= NOSHIFT_BOUND ** 2) & (scal_ref[l, S_SINK_ABSMAX] <= NOSHIFT_BOUND)

    lam = _diff_lambda(lamv_ref, scal_ref[l, S_LAM_INIT])
    subg = subg_ref[0]
    sinks = [scal_ref[l, h] for h in range(SW_HEADS)]
    q0 = qb * TQ
    start = pl.multiple_of(jnp.clip(q0 - WINDOW, 0, DEC_SEQ - SW_BAND), LANES)

    def attend(shift):
        da_o = _diff_attention(q_ref[0, :, 0:256].astype(F32), lambda: kda[...],
                               lambda c: vda[:, c * LANES:(c + 1) * LANES], lam, shift)
        for c, o in enumerate(da_o):
            mix_ref[0, :, c * LANES:(c + 1) * LANES] = _diff_subln(
                o, _chunk(subg, c), scal_ref[l, S_ONE_MINUS]).astype(BF16)

        band = pl.ds(PAST_LEN + start, SW_BAND)
        k_sel = jnp.concatenate([ksw[0:PAST_LEN, :], ksw[band, :]], axis=0)
        v_sel = jnp.concatenate([vsw[0:PAST_LEN, :], vsw[band, :]], axis=0)
        n_sel = PAST_LEN + SW_BAND
        col = lax.broadcasted_iota(jnp.int32, (TQ, n_sel), 1)
        row = lax.broadcasted_iota(jnp.int32, (TQ, n_sel), 0)
        dist = (q0 + row) - (start + col - PAST_LEN)
        valid = (col < PAST_LEN) | ((dist <= WINDOW) & (dist >= -WINDOW))
        sw_o = _swa_attention(q_ref[0, :, 256:768].astype(F32), k_sel, v_sel, sinks, shift, valid)
        for c, o in enumerate(sw_o):
            mix_ref[0, :, 256 + c * LANES:256 + (c + 1) * LANES] = o.astype(BF16)

        mla_o = _mla_attention(lambda h: q_ref[0, :, O_MQ + h * LANES:O_MQ + (h + 1) * LANES],
                               lambda h: kml[:, h * LANES:(h + 1) * LANES],
                               lambda h: vml[:, h * LANES:(h + 1) * LANES], shift)
        for c, o in enumerate(mla_o):
            mix_ref[0, :, 768 + c * LANES:768 + (c + 1) * LANES] = o.astype(BF16)

    pl.when(small)(functools.partial(attend, False))
    pl.when(jnp.logical_not(small))(functools.partial(attend, True))


def _lat_attend(l_idx, q, k, v, caches, p, pad_eye):
    cdk, cdv, csk, csv, cckv, ckr = caches

    def cache_spec(a):
        return pl.BlockSpec((1, 1) + a.shape[2:], lambda b, j, l: (b, l[0], 0, 0))

    grid_spec = pltpu.PrefetchScalarGridSpec(
        num_scalar_prefetch=1,
        grid=(DEC_BATCH, DEC_SEQ // TQ),
        in_specs=[
            pl.BlockSpec((1, TQ, Q_W), lambda b, j, l: (b, j, 0)),
            pl.BlockSpec((1, DEC_SEQ, K_W), lambda b, j, l: (b, 0, 0)),
            pl.BlockSpec((1, DEC_SEQ, V_W), lambda b, j, l: (b, 0, 0)),
            cache_spec(cdk), cache_spec(cdv), cache_spec(csk), cache_spec(csv), cache_spec(cckv), cache_spec(ckr),
            _layer_spec(p["mla_w_kvb"].shape),
            pl.BlockSpec(pad_eye.shape, lambda b, j, l: (0, 0)),
            _layer_spec(p["lamv"].shape),
            _layer_spec(p["subg"].shape),
            pl.BlockSpec(p["gq"].shape, lambda b, j, l: (0, 0)),
            pl.BlockSpec(p["gk"].shape, lambda b, j, l: (0, 0)),
            pl.BlockSpec(memory_space=pltpu.SMEM),
        ],
        out_specs=pl.BlockSpec((1, TQ, D_MODEL), lambda b, j, l: (b, j, 0)),
        scratch_shapes=[
            pltpu.VMEM((N_KEYS, 256), BF16), pltpu.VMEM((N_KEYS, 256), BF16),
            pltpu.VMEM((N_KEYS, 128), BF16), pltpu.VMEM((N_KEYS, 256), BF16),
            pltpu.VMEM((N_KEYS, MLA_CHUNKS), BF16), pltpu.VMEM((N_KEYS, MLA_CHUNKS), BF16),
            pltpu.VMEM((1, LANES), F32),
        ],
    )
    return pl.pallas_call(
        _lat_attn_kernel,
        grid_spec=grid_spec,
        out_shape=jax.ShapeDtypeStruct((DEC_BATCH, DEC_SEQ, D_MODEL), BF16),
        compiler_params=pltpu.CompilerParams(
            dimension_semantics=("arbitrary", "arbitrary"), vmem_limit_bytes=VMEM_LIMIT),
        name="latent_attend",
    )(l_idx, q.reshape(DEC_BATCH, DEC_SEQ, Q_W), k.reshape(DEC_BATCH, DEC_SEQ, K_W),
      v.reshape(DEC_BATCH, DEC_SEQ, V_W), cdk, cdv, csk, csv, cckv, ckr,
      p["mla_w_kvb"], pad_eye, p["lamv"], p["subg"], p["gq"], p["gk"], p["scal"])


def _mlp_kernel(l_ref, x_ref, mix_ref, mod_ref, n2g_ref, w_out_ref, w_up_ref, w_down_ref, fg_ref, o_ref,
                *, final_norm):
    m = mod_ref[0, 0]
    x = x_ref[...] + m[2:3] * jnp.dot(mix_ref[...], w_out_ref[0], preferred_element_type=F32)
    h = (_rms(x, n2g_ref[0]) * (1.0 + m[4:5]) + m[3:4]).astype(BF16)
    acc = jnp.zeros_like(x)
    for c in range(D_FF // FF_CHUNK):
        u = jnp.dot(h, w_up_ref[0, :, c * FF_CHUNK:(c + 1) * FF_CHUNK], preferred_element_type=F32)
        a = jnp.square(jnp.maximum(u, 0.0)).astype(BF16)
        acc = acc + jnp.dot(a, w_down_ref[0, c * FF_CHUNK:(c + 1) * FF_CHUNK, :], preferred_element_type=F32)
    x = x + m[5:6] * acc
    if final_norm:
        x = _rms(x, fg_ref[...])
    o_ref[...] = x


def _mlp_layer(l_idx, x, mix, mod, p, final_g, mod_row, final_norm):
    n_tok = x.shape[0]
    grid_spec = pltpu.PrefetchScalarGridSpec(
        num_scalar_prefetch=1,
        grid=(n_tok // TM_MLP,),
        in_specs=[
            pl.BlockSpec((TM_MLP, D_MODEL), lambda i, l: (i, 0)),
            pl.BlockSpec((TM_MLP, D_MODEL), lambda i, l: (i, 0)),
            pl.BlockSpec((1, 1, 6, D_MODEL), lambda i, l: (l[0], mod_row(i), 0, 0)),
            _layer_spec(p["norm2_g"].shape),
            _layer_spec(p["w_out"].shape),
            _layer_spec(p["w_up"].shape),
            _layer_spec(p["w_down"].shape),
            pl.BlockSpec((1, D_MODEL), lambda i, l: (0, 0)),
        ],
        out_specs=pl.BlockSpec((TM_MLP, D_MODEL), lambda i, l: (i, 0)),
    )
    return pl.pallas_call(
        functools.partial(_mlp_kernel, final_norm=final_norm),
        grid_spec=grid_spec,
        out_shape=jax.ShapeDtypeStruct((n_tok, D_MODEL), F32),
        compiler_params=pltpu.CompilerParams(
            dimension_semantics=("arbitrary",), vmem_limit_bytes=VMEM_LIMIT),
        name="out_proj_mlp",
    )(l_idx, x, mix, mod, p["norm2_g"], p["w_out"], p["w_up"], p["w_down"], final_g)


def _rope_tables(period, active_lanes=LANES):
    m = period // 2
    half = m // 2
    t = np.arange(DEC_SEQ)
    pos = np.stack([t // GRID_W, t % GRID_W], axis=0).astype(np.float64)
    lane = np.arange(LANES) % period
    axis = lane // m
    k = lane % m
    freq = ROPE_BASE ** (-(k % half).astype(np.float64) / half)
    ang = pos[axis, :].T * freq[None, :]
    first = (k < half)[None, :]
    active = (np.arange(LANES) < active_lanes)[None, :]
    cos, sin = np.where(active, np.cos(ang), 1.0), np.where(active, np.sin(ang), 0.0)
    return (jnp.asarray(cos, F32), jnp.asarray(np.where(first, -sin, 0.0), F32),
            jnp.asarray(np.where(first, 0.0, sin), F32))


def _group_matrices():
    gq = np.zeros((Q_W, LANES), np.float32)
    gk = np.zeros((K_W, LANES), np.float32)
    for d in range(DA_HEADS * 2 * DA_QK):
        gq[d, G_DIFF + d // DA_QK] = 1.0
        gk[d, G_DIFF + d // DA_QK] = 1.0
    for c in range(SW_HEADS // 2):
        for par in range(2):
            col = G_SWA + 2 * c + par
            gq[256 + c * LANES + par * SW_DIM:256 + c * LANES + (par + 1) * SW_DIM, col] = 1.0
            gk[256 + par * SW_DIM:256 + (par + 1) * SW_DIM, col] = 1.0
    for h in range(MLA_HEADS):
        gq[O_MQ + h * LANES:O_MQ + (h + 1) * LANES, G_MLA + h] = 1.0
        gk[O_MK + h * LANES:O_MK + (h + 1) * LANES, G_MLA + h] = 1.0
    return jnp.asarray(gq, BF16), jnp.asarray(gk, BF16)


def _paired_head_order():
    order = []
    for c in range(SW_HEADS // 2):
        order += [c, c + SW_HEADS // 2]
    return order


def _prepare_params(norm1_g, norm2_g, w_in, w_out, lq1, lk1, lq2, lk2, diff_subln_g, swa_sink,
                    mla_q_norm_g, mla_w_qb, mla_kv_norm_g, mla_w_kvb, w_up, w_down):
    heads = _paired_head_order()
    o_swq = 768
    swq_cols = [w_in[:, :, o_swq + h * SW_DIM:o_swq + (h + 1) * SW_DIM] for h in heads]
    kr_cols = [w_in[:, :, 1920:1952], jnp.zeros((DEPTH, D_MODEL, LANES - MLA_ROPE), w_in.dtype)]
    w_in_ext = jnp.concatenate([w_in[:, :, 0:768]] + swq_cols + [w_in[:, :, 1280:1920]] + kr_cols,
                               axis=-1).astype(BF16)
    o_swo = DA_HEADS * DA_V
    swo_rows = [w_out[:, o_swo + h * SW_DIM:o_swo + (h + 1) * SW_DIM, :] for h in heads]
    w_out_p = jnp.concatenate([w_out[:, 0:o_swo, :]] + swo_rows + [w_out[:, o_swo + SW_HEADS * SW_DIM:, :]],
                              axis=1).astype(BF16)
    dq = MLA_NOPE + MLA_ROPE
    gap = LANES - dq
    zq = jnp.zeros((DEPTH, MLA_Q_RANK, gap), mla_w_qb.dtype)
    wqb = jnp.concatenate(
        [blk for h in range(MLA_HEADS)
         for blk in (mla_w_qb[:, :, h * dq + MLA_NOPE:(h + 1) * dq], zq, mla_w_qb[:, :, h * dq:h * dq + MLA_NOPE])],
        axis=-1).astype(BF16)
    dkv = MLA_NOPE + MLA_V
    zk = jnp.zeros((DEPTH, MLA_KV_RANK, LANES - MLA_NOPE), mla_w_kvb.dtype)
    wkvb = jnp.concatenate(
        [blk for h in range(MLA_HEADS) for blk in (zk, mla_w_kvb[:, :, h * dkv:h * dkv + MLA_NOPE])]
        + [mla_w_kvb[:, :, h * dkv + MLA_NOPE:(h + 1) * dkv] for h in range(MLA_HEADS)],
        axis=-1).astype(BF16)
    lam_init = np.array([0.8 - 0.6 * math.exp(-0.3 * l) for l in range(DEPTH)], np.float32)
    sink_l2 = swa_sink.astype(F32) * LOG2E
    scal = jnp.concatenate([
        sink_l2,
        jnp.asarray(np.stack([lam_init, (1.0 - lam_init.astype(np.float64)).astype(np.float32)], axis=1)),
        jnp.max(jnp.abs(sink_l2), axis=1, keepdims=True),
        jnp.zeros((DEPTH, S_COLS - SW_HEADS - 3), F32)], axis=1)
    gq, gk = _group_matrices()
    return {
        "gq": gq,
        "gk": gk,
        "norm1_g": norm1_g.reshape(DEPTH, 1, D_MODEL),
        "norm2_g": norm2_g.reshape(DEPTH, 1, D_MODEL),
        "w_in": w_in_ext,
        "w_out": w_out_p,
        "mla_q_norm_g": mla_q_norm_g.reshape(DEPTH, 1, MLA_Q_RANK),
        "mla_w_qb": wqb,
        "mla_kv_norm_g": mla_kv_norm_g.reshape(DEPTH, 1, MLA_KV_RANK),
        "mla_w_kvb": wkvb,
        "lamv": jnp.stack([lq1, lk1, lq2, lk2], axis=1),
        "subg": jnp.tile(diff_subln_g, (1, DA_HEADS)).reshape(DEPTH, 1, DA_HEADS * DA_V),
        "scal": scal,
        "w_up": w_up.astype(BF16),
        "w_down": w_down.astype(BF16),
    }


def kernel(x_prompt, x_sample, c, cache_diff_k, cache_diff_v, cache_swa_k, cache_swa_v, cache_mla_ckv,
           cache_mla_krope, c_ctx, norm1_g, norm2_g, w_ada, b_ada, w_in, w_out, diff_lambda_q1,
           diff_lambda_k1, diff_lambda_q2, diff_lambda_k2, diff_subln_g, swa_sink, mla_q_norm_g,
           mla_w_qb, mla_kv_norm_g, mla_w_kvb, w_up, w_down, final_g):
    p = _prepare_params(norm1_g, norm2_g, w_in, w_out, diff_lambda_q1, diff_lambda_k1, diff_lambda_q2,
                        diff_lambda_k2, diff_subln_g, swa_sink, mla_q_norm_g, mla_w_qb, mla_kv_norm_g,
                        mla_w_kvb, w_up, w_down)
    cvecs = jnp.concatenate([c, c_ctx[None, :], jnp.zeros((8 - DEC_BATCH - 1, D_MODEL), F32)], axis=0)
    mod = _modulation(cvecs, w_ada, b_ada)
    tables = _rope_tables(DA_QK) + _rope_tables(SW_DIM) + _rope_tables(MLA_ROPE, MLA_ROPE)
    pad_eye = jnp.asarray(np.eye(MLA_ROPE, LANES, dtype=np.float32), BF16)
    final_g2 = final_g.reshape(1, D_MODEL)
    caches = (cache_diff_k.reshape(DEC_BATCH, DEPTH, PAST_LEN, DA_HEADS * 2 * DA_QK),
              cache_diff_v.reshape(DEC_BATCH, DEPTH, PAST_LEN, DA_HEADS * DA_V),
              cache_swa_k.reshape(DEC_BATCH, DEPTH, PAST_LEN, SW_KV_HEADS * SW_DIM),
              cache_swa_v.reshape(DEC_BATCH, DEPTH, PAST_LEN, SW_KV_HEADS * SW_DIM),
              cache_mla_ckv, cache_mla_krope)

    x_ctx = x_prompt.reshape(N_CTX_TOK, D_MODEL)
    x_lat = x_sample.reshape(N_LAT_TOK, D_MODEL)
    new = [jnp.zeros((BATCH, DEPTH, SEQ, w), F32) for w in CACHE_WIDTHS]
    lat_tiles_per_req = DEC_SEQ // TM_MLP
    for l in range(DEPTH):
        l_idx = jnp.full((1,), l, jnp.int32)
        last = l == DEPTH - 1
        *new, mix_ctx = _ctx_layer(l_idx, x_ctx, mod, p, new)
        x_ctx = _mlp_layer(l_idx, x_ctx, mix_ctx, mod, p, final_g2, lambda i: CTX_MOD_ROW, last)
        q, k, v = _lat_project(l_idx, x_lat, mod, p, tables)
        mix_lat = _lat_attend(l_idx, q, k, v, caches, p, pad_eye).reshape(N_LAT_TOK, D_MODEL)
        x_lat = _mlp_layer(l_idx, x_lat, mix_lat, mod, p, final_g2, lambda i: i // lat_tiles_per_req, last)

    y_prompt = x_ctx.reshape(BATCH, SEQ, D_MODEL)
    y_sample = x_lat.reshape(DEC_BATCH, DEC_SEQ, D_MODEL)
    new_dk, new_dv, new_sk, new_sv, new_ckv, new_kr = new
    return (y_prompt, y_sample,
            new_dk.reshape(BATCH, DEPTH, SEQ, DA_HEADS, 2 * DA_QK),
            new_dv.reshape(BATCH, DEPTH, SEQ, DA_HEADS, DA_V),
            new_sk.reshape(BATCH, DEPTH, SEQ, SW_KV_HEADS, SW_DIM),
            new_sv.reshape(BATCH, DEPTH, SEQ, SW_KV_HEADS, SW_DIM),
            new_ckv, new_kr)
```

```python
import functools
import math

import numpy as np
import jax
import jax.numpy as jnp
from jax import lax
from jax.experimental import pallas as pl
from jax.experimental.pallas import tpu as pltpu

F32 = jnp.float32
BF16 = jnp.bfloat16

D_MODEL = 1024
BATCH = 32
SEQ = 256
DEPTH = 4
DEC_BATCH = 4
DEC_SEQ = 2048
PAST_LEN = 256
GRID_W = 64
WINDOW = 128
ROPE_BASE = 10000.0
EPS = 1e-6
DA_HEADS = 4
DA_QK = 32
DA_V = 64
SW_HEADS = 8
SW_KV_HEADS = 2
SW_DIM = 64
MLA_HEADS = 4
MLA_Q_RANK = 256
MLA_KV_RANK = 128
MLA_NOPE = 64
MLA_ROPE = 32
MLA_V = 64
D_FF = 4 * D_MODEL

LANES = 128
N_CTX_TOK = BATCH * SEQ
N_LAT_TOK = DEC_BATCH * DEC_SEQ
N_KEYS = PAST_LEN + DEC_SEQ
CTX_MOD_ROW = DEC_BATCH

C_DAQ, C_DAK, C_DAV, C_SWQ, C_SWK, C_SWV, C_QA, C_CKV, C_KR, P_EXT = (
    0, 256, 512, 768, 1280, 1408, 1536, 1792, 1920, 2048)
MLA_CHUNKS = MLA_HEADS * 128
Q_W = 256 + 512 + MLA_CHUNKS
K_W = 256 + 128 + MLA_CHUNKS
V_W = 256 + 128 + 256
O_MQ = 768
O_MK = 384
LOG2E = 1.4426950408889634

DA_SCALE = DA_QK ** -0.5
SW_SCALE = SW_DIM ** -0.5
MLA_SCALE = (MLA_NOPE + MLA_ROPE) ** -0.5

CTX_REQ_PER_STEP = 1
TM_CTX = CTX_REQ_PER_STEP * SEQ
TM_LAT = 512
TM_MLP = 512
TQ = 256
SW_BAND = TQ + 2 * WINDOW
FF_CHUNK = 1024
MOD_TILE = 1536
VMEM_LIMIT = 56 * 1024 * 1024

_NT = (((1,), (1,)), ((), ()))


def _scores(q, k):
    return lax.dot_general(q, k, _NT, preferred_element_type=F32)

S_LAM_INIT = SW_HEADS
S_ONE_MINUS = SW_HEADS + 1
S_SINK_ABSMAX = SW_HEADS + 2
S_COLS = 16

NOSHIFT_BOUND = 50.0
G_DIFF, G_SWA, G_MLA = 0, 8, 16


def _rms(x, g):
    ms = jnp.mean(x * x, axis=-1, keepdims=True)
    return x * lax.rsqrt(ms + EPS) * g


def _lane_mask(n, lo, width):
    lane = lax.broadcasted_iota(jnp.int32, (1, n), 1)
    return (lane >= lo) & (lane < lo + width)


def _exp_scores(s, shift, sink=None):
    if not shift:
        return jnp.exp2(s), jnp.zeros((1, 1), F32)
    m = jnp.max(s, axis=-1, keepdims=True)
    if sink is not None:
        m = jnp.maximum(m, sink)
    return jnp.exp2(s - m), m


def _sq_bf16(x):
    xf = x.astype(F32)
    return (xf * xf).astype(BF16)


def _logit_bound_sq(q_all, gq, kmax_sq):
    qn = jnp.dot(_sq_bf16(q_all), gq, preferred_element_type=F32)
    return jnp.max(qn, axis=0, keepdims=True) * kmax_sq * 1.03


def _chunk(a, j):
    return a[:, j * LANES:(j + 1) * LANES]


def _project(x_ref, mod_ref, n1g_ref, w_in_ref):
    m = mod_ref[0, 0]
    h = _rms(x_ref[...], n1g_ref[0]) * (1.0 + m[1:2]) + m[0:1]
    return jnp.dot(h.astype(BF16), w_in_ref[0], preferred_element_type=F32)


def _diff_lambda(lamv_ref, lam_init):
    lv = lamv_ref[0]
    a = jnp.sum(lv[0:1] * lv[1:2], axis=-1, keepdims=True)
    b = jnp.sum(lv[2:3] * lv[3:4], axis=-1, keepdims=True)
    return jnp.exp(a) - jnp.exp(b) + lam_init


def _half_mask(par):
    lane = lax.broadcasted_iota(jnp.int32, (1, LANES), 1)
    return (lane >= par * 64) & (lane < (par + 1) * 64)


def _diff_attention(qf, load_k, load_v, lam, shift):
    width = DA_HEADS * DA_V
    chunks = []
    for c in range(DA_HEADS // 2):
        o_par = []
        for par in range(2):
            lo = (2 * c + par) * DA_V
            q1 = jnp.where(_lane_mask(width, lo, DA_QK), qf, 0.0).astype(BF16)
            q2 = jnp.where(_lane_mask(width, lo + DA_QK, DA_QK), qf, 0.0).astype(BF16)
            e1, _ = _exp_scores(_scores(q1, load_k()), shift)
            e2, _ = _exp_scores(_scores(q2, load_k()), shift)
            d1 = jnp.sum(e1, axis=-1, keepdims=True)
            d2 = jnp.sum(e2, axis=-1, keepdims=True)
            w = (e1 - e2 * (lam * d1 * (1.0 / d2))).astype(BF16)
            o_par.append(jnp.dot(w, load_v(c), preferred_element_type=F32) * (1.0 / d1))
        chunks.append(jnp.where(_half_mask(0), o_par[0], o_par[1]))
    return chunks


def _diff_subln(o, subg, one_minus_lam_init):
    sq = o * o
    r = None
    for par in range(2):
        ms = jnp.sum(jnp.where(_half_mask(par), sq, 0.0), axis=-1, keepdims=True) * (1.0 / DA_V)
        rs = lax.rsqrt(ms + EPS)
        r = rs if r is None else jnp.where(_half_mask(0), r, rs)
    return o * r * subg * one_minus_lam_init


def _swa_attention(qf, k, v_ext, sinks, shift, valid=None):
    outs = []
    for c in range(SW_HEADS // 2):
        qc = _chunk(qf, c)
        o_par = []
        for par in range(2):
            h = c + (SW_HEADS // 2) * par
            qm = jnp.where(_half_mask(par), qc, 0.0).astype(BF16)
            s = _scores(qm, k)
            if valid is not None:
                s = jnp.where(valid, s, -1e30)
            e, m = _exp_scores(s, shift, sink=sinks[h])
            o = jnp.dot(e.astype(BF16), v_ext, preferred_element_type=F32)
            d = _chunk(o, 1) + jnp.exp2(sinks[h] - m)
            o_par.append(_chunk(o, 0) * (1.0 / d))
        outs.append(jnp.where(_half_mask(0), o_par[0], o_par[1]))
    return outs


def _mla_attention(load_q, load_k, load_v, shift):
    chunks = []
    for c in range(MLA_HEADS // 2):
        o_par = []
        for par in range(2):
            h = 2 * c + par
            e, _ = _exp_scores(_scores(load_q(h), load_k(h)), shift)
            o = jnp.dot(e.astype(BF16), load_v(h), preferred_element_type=F32)
            o_par.append(o * (1.0 / pltpu.roll(o, 64, 1)))
        chunks.append(jnp.where(_half_mask(0), o_par[0], o_par[1]))
    return chunks


def _ones_in_other_half(v, par):
    return jnp.where(_half_mask(par), v, 1.0)


def _mod_kernel(c_ref, w_ref, b_ref, o_ref):
    c = c_ref[...]
    s = c * (1.0 / (1.0 + jnp.exp(-c)))
    o_ref[0] = jnp.dot(s.astype(BF16), w_ref[0].astype(BF16), preferred_element_type=F32) + b_ref[0]


def _modulation(cvecs, w_ada, b_ada):
    n_out = 6 * D_MODEL
    out = pl.pallas_call(
        _mod_kernel,
        grid=(DEPTH, n_out // MOD_TILE),
        in_specs=[
            pl.BlockSpec((8, D_MODEL), lambda l, j: (0, 0)),
            pl.BlockSpec((1, D_MODEL, MOD_TILE), lambda l, j: (l, 0, j)),
            pl.BlockSpec((1, 1, MOD_TILE), lambda l, j: (l, 0, j)),
        ],
        out_specs=pl.BlockSpec((1, 8, MOD_TILE), lambda l, j: (l, 0, j)),
        out_shape=jax.ShapeDtypeStruct((DEPTH, 8, n_out), F32),
        compiler_params=pltpu.CompilerParams(
            dimension_semantics=("arbitrary", "arbitrary"), vmem_limit_bytes=VMEM_LIMIT),
        name="adaln_modulation",
    )(cvecs, w_ada, b_ada.reshape(DEPTH, 1, n_out))
    return out.reshape(DEPTH, 8, 6, D_MODEL)


def _ctx_kernel(l_ref, x_ref, mod_ref, n1g_ref, w_in_ref, qg_ref, wqb_ref, kvg_ref, wkvb_ref,
                lamv_ref, subg_ref, scal_ref, *rest, first_layer):
    dk_ref, dv_ref, sk_ref, sv_ref, ckv_ref, kr_ref, mix_ref = rest[-7:]
    l = l_ref[0]
    if first_layer:
        for ref in (dk_ref, dv_ref, sk_ref, sv_ref, ckv_ref, kr_ref):
            ref[:, 1:] = jnp.zeros((CTX_REQ_PER_STEP, DEPTH - 1) + ref.shape[2:], F32)
    proj = _project(x_ref, mod_ref, n1g_ref, w_in_ref)
    ckv_all = _rms(proj[:, C_CKV:C_KR], kvg_ref[0])
    mq_all = jnp.dot(_rms(proj[:, C_QA:C_CKV], qg_ref[0]).astype(BF16), wqb_ref[0], preferred_element_type=F32)
    mq_all = mq_all * (MLA_SCALE * LOG2E)
    kv_all = jnp.dot(ckv_all.astype(BF16), wkvb_ref[0], preferred_element_type=F32)
    lam = _diff_lambda(lamv_ref, scal_ref[l, S_LAM_INIT])
    subg = subg_ref[0]
    sinks = [scal_ref[l, h] for h in range(SW_HEADS)]
    shift = True

    for r in range(CTX_REQ_PER_STEP):
        r0, r1 = r * SEQ, (r + 1) * SEQ
        pr, mq, kv = proj[r0:r1], mq_all[r0:r1], kv_all[r0:r1]
        da_k = pr[:, C_DAK:C_DAV]
        da_v = pr[:, C_DAV:C_SWQ]
        sw_k = pr[:, C_SWK:C_SWV]
        sw_v = pr[:, C_SWV:C_QA]
        krz = pr[:, C_KR:P_EXT]
        dk_ref[r, 0] = da_k
        dv_ref[r, 0] = da_v
        sk_ref[r, 0] = sw_k
        sv_ref[r, 0] = sw_v
        ckv_ref[r, 0] = ckv_all[r0:r1]
        kr_ref[r, 0] = krz[:, 0:MLA_ROPE]

        da_kb = da_k.astype(BF16)
        da_o = _diff_attention(pr[:, C_DAQ:C_DAK] * (DA_SCALE * LOG2E), lambda: da_kb,
                               lambda c: _chunk(da_v, c).astype(BF16), lam, shift)
        for c, o in enumerate(da_o):
            mix_ref[r0:r1, c * LANES:(c + 1) * LANES] = _diff_subln(
                o, _chunk(subg, c), scal_ref[l, S_ONE_MINUS]).astype(BF16)
        sw_v_ext = jnp.concatenate([sw_v, jnp.ones_like(sw_v)], axis=-1).astype(BF16)
        sw_o = _swa_attention(pr[:, C_SWQ:C_SWK] * (SW_SCALE * LOG2E), sw_k.astype(BF16), sw_v_ext, sinks, shift)
        for c, o in enumerate(sw_o):
            mix_ref[r0:r1, 256 + c * LANES:256 + (c + 1) * LANES] = o.astype(BF16)
        mla_o = _mla_attention(
            lambda h: _chunk(mq, h).astype(BF16), lambda h: (_chunk(kv, h) + krz).astype(BF16),
            lambda h: _ones_in_other_half(_chunk(kv, MLA_HEADS + h // 2), h % 2).astype(BF16), shift)
        for c, o in enumerate(mla_o):
            mix_ref[r0:r1, 768 + c * LANES:768 + (c + 1) * LANES] = o.astype(BF16)


def _layer_spec(shape):
    nd = len(shape)
    return pl.BlockSpec((1,) + tuple(shape[1:]), lambda *a: (a[-1][0],) + (0,) * (nd - 1))


CACHE_WIDTHS = (DA_HEADS * 2 * DA_QK, DA_HEADS * DA_V, SW_KV_HEADS * SW_DIM, SW_KV_HEADS * SW_DIM,
                MLA_KV_RANK, MLA_ROPE)


def _ctx_layer(l_idx, x, mod, p, new_caches):
    n_req = BATCH // CTX_REQ_PER_STEP
    first_layer = new_caches is None
    if first_layer:
        new_caches = []
        out_shape = [jax.ShapeDtypeStruct((BATCH, DEPTH, SEQ, w), F32) for w in CACHE_WIDTHS]
        out_specs = [pl.BlockSpec((CTX_REQ_PER_STEP, DEPTH, SEQ, w), lambda i, l: (i, 0, 0, 0)) for w in CACHE_WIDTHS]
    else:
        out_shape = [jax.ShapeDtypeStruct(c.shape, c.dtype) for c in new_caches]
        out_specs = [pl.BlockSpec((CTX_REQ_PER_STEP, 1, SEQ, w), lambda i, l: (i, l[0], 0, 0)) for w in CACHE_WIDTHS]
    out_shape.append(jax.ShapeDtypeStruct((N_CTX_TOK, D_MODEL), BF16))
    out_specs.append(pl.BlockSpec((TM_CTX, D_MODEL), lambda i, l: (i, 0)))
    n_fixed_inputs = 12
    aliases = {n_fixed_inputs + j: j for j in range(len(new_caches))}
    grid_spec = pltpu.PrefetchScalarGridSpec(
        num_scalar_prefetch=1,
        grid=(n_req,),
        in_specs=[
            pl.BlockSpec((TM_CTX, D_MODEL), lambda i, l: (i, 0)),
            pl.BlockSpec((1, 1, 6, D_MODEL), lambda i, l: (l[0], CTX_MOD_ROW, 0, 0)),
            _layer_spec(p["norm1_g"].shape),
            _layer_spec(p["w_in"].shape),
            _layer_spec(p["mla_q_norm_g"].shape),
            _layer_spec(p["mla_w_qb"].shape),
            _layer_spec(p["mla_kv_norm_g"].shape),
            _layer_spec(p["mla_w_kvb"].shape),
            _layer_spec(p["lamv"].shape),
            _layer_spec(p["subg"].shape),
            pl.BlockSpec(memory_space=pltpu.SMEM),
        ] + [pl.BlockSpec(memory_space=pl.ANY)] * len(new_caches),
        out_specs=out_specs,
    )
    return pl.pallas_call(
        functools.partial(_ctx_kernel, first_layer=first_layer),
        grid_spec=grid_spec,
        out_shape=out_shape,
        input_output_aliases=aliases,
        compiler_params=pltpu.CompilerParams(
            dimension_semantics=("arbitrary",), vmem_limit_bytes=VMEM_LIMIT),
        name="ctx_project_attend",
    )(l_idx, x, mod, p["norm1_g"], p["w_in"], p["mla_q_norm_g"], p["mla_w_qb"], p["mla_kv_norm_g"],
      p["mla_w_kvb"], p["lamv"], p["subg"], p["scal"], *new_caches)


def _rope(x, cos, sa, sb, half):
    return x * cos + pltpu.roll(x, LANES - half, 1) * sa + pltpu.roll(x, half, 1) * sb


def _lat_proj_kernel(l_ref, x_ref, mod_ref, n1g_ref, w_in_ref, qg_ref, wqb_ref, kvg_ref, wkvb_ref,
                     c32_ref, a32_ref, b32_ref, c64_ref, a64_ref, b64_ref, cm_ref, am_ref, bm_ref,
                     q_ref, k_ref, v_ref):
    proj = _project(x_ref, mod_ref, n1g_ref, w_in_ref)
    t32 = (c32_ref[...], a32_ref[...], b32_ref[...], DA_QK // 4)
    t64 = (c64_ref[...], a64_ref[...], b64_ref[...], SW_DIM // 4)
    tml = (cm_ref[...], am_ref[...], bm_ref[...], MLA_ROPE // 4)

    for j in range(2):
        q = _rope(_chunk(proj, j), *t32) * (DA_SCALE * LOG2E)
        q_ref[:, j * LANES:(j + 1) * LANES] = q.astype(BF16)
    for j in range(4):
        q = _rope(_chunk(proj, C_SWQ // LANES + j), *t64) * (SW_SCALE * LOG2E)
        q_ref[:, 256 + j * LANES:256 + (j + 1) * LANES] = q.astype(BF16)
    mq = jnp.dot(_rms(proj[:, C_QA:C_CKV], qg_ref[0]).astype(BF16), wqb_ref[0], preferred_element_type=F32)
    for h in range(MLA_HEADS):
        q = _rope(_chunk(mq, h), *tml) * (MLA_SCALE * LOG2E)
        q_ref[:, O_MQ + h * LANES:O_MQ + (h + 1) * LANES] = q.astype(BF16)

    for j in range(2):
        k_ref[:, j * LANES:(j + 1) * LANES] = _rope(_chunk(proj, C_DAK // LANES + j), *t32).astype(BF16)
    k_ref[:, 256:384] = _rope(proj[:, C_SWK:C_SWV], *t64).astype(BF16)
    ckv = _rms(proj[:, C_CKV:C_KR], kvg_ref[0])
    kv = jnp.dot(ckv.astype(BF16), wkvb_ref[0], preferred_element_type=F32)
    krz = _rope(proj[:, C_KR:P_EXT], *tml)
    for h in range(MLA_HEADS):
        k_ref[:, O_MK + h * LANES:O_MK + (h + 1) * LANES] = (_chunk(kv, h) + krz).astype(BF16)

    v_ref[:, 0:256] = proj[:, C_DAV:C_SWQ].astype(BF16)
    v_ref[:, 256:384] = proj[:, C_SWV:C_QA].astype(BF16)
    v_ref[:, 384:640] = kv[:, MLA_CHUNKS:].astype(BF16)


def _lat_project(l_idx, x, mod, p, tables):
    tiles_per_req = DEC_SEQ // TM_LAT
    tab_spec = pl.BlockSpec((TM_LAT, LANES), lambda i, l: (i % tiles_per_req, 0))
    grid_spec = pltpu.PrefetchScalarGridSpec(
        num_scalar_prefetch=1,
        grid=(N_LAT_TOK // TM_LAT,),
        in_specs=[
            pl.BlockSpec((TM_LAT, D_MODEL), lambda i, l: (i, 0)),
            pl.BlockSpec((1, 1, 6, D_MODEL), lambda i, l: (l[0], i // tiles_per_req, 0, 0)),
            _layer_spec(p["norm1_g"].shape),
            _layer_spec(p["w_in"].shape),
            _layer_spec(p["mla_q_norm_g"].shape),
            _layer_spec(p["mla_w_qb"].shape),
            _layer_spec(p["mla_kv_norm_g"].shape),
            _layer_spec(p["mla_w_kvb"].shape),
        ] + [tab_spec] * len(tables),
        out_specs=[
            pl.BlockSpec((TM_LAT, Q_W), lambda i, l: (i, 0)),
            pl.BlockSpec((TM_LAT, K_W), lambda i, l: (i, 0)),
            pl.BlockSpec((TM_LAT, V_W), lambda i, l: (i, 0)),
        ],
    )
    return pl.pallas_call(
        _lat_proj_kernel,
        grid_spec=grid_spec,
        out_shape=[jax.ShapeDtypeStruct((N_LAT_TOK, Q_W), BF16),
                   jax.ShapeDtypeStruct((N_LAT_TOK, K_W), BF16),
                   jax.ShapeDtypeStruct((N_LAT_TOK, V_W), BF16)],
        compiler_params=pltpu.CompilerParams(
            dimension_semantics=("arbitrary",), vmem_limit_bytes=VMEM_LIMIT),
        name="latent_project",
    )(l_idx, x, mod, p["norm1_g"], p["w_in"], p["mla_q_norm_g"], p["mla_w_qb"], p["mla_kv_norm_g"],
      p["mla_w_kvb"], *tables)


def _lat_attn_kernel(l_ref, q_ref, k_ref, v_ref, cdk_ref, cdv_ref, csk_ref, csv_ref, cckv_ref, ckr_ref,
                     wkvb_ref, pad_ref, lamv_ref, subg_ref, gq_ref, gk_ref, scal_ref,
                     mix_ref,
                     kda, vda, ksw, vsw, kml, vml, kmax_sq):
    l = l_ref[0]
    qb = pl.program_id(1)

    @pl.when(qb == 0)
    def _stage_keys():
        kda[0:PAST_LEN, :] = cdk_ref[0, 0].astype(BF16)
        kda[PAST_LEN:N_KEYS, :] = k_ref[0, :, 0:256]
        vda[0:PAST_LEN, :] = cdv_ref[0, 0].astype(BF16)
        vda[PAST_LEN:N_KEYS, :] = v_ref[0, :, 0:256]
        ksw[0:PAST_LEN, :] = csk_ref[0, 0].astype(BF16)
        ksw[PAST_LEN:N_KEYS, :] = k_ref[0, :, 256:384]
        vsw[0:PAST_LEN, 0:LANES] = csv_ref[0, 0].astype(BF16)
        vsw[PAST_LEN:N_KEYS, 0:LANES] = v_ref[0, :, 256:384]
        vsw[:, LANES:2 * LANES] = jnp.ones((N_KEYS, LANES), BF16)
        kvc = jnp.dot(cckv_ref[0, 0].astype(BF16), wkvb_ref[0], preferred_element_type=F32)
        krz = jnp.dot(ckr_ref[0, 0].astype(BF16), pad_ref[...], preferred_element_type=F32)
        mk_ctx = [_chunk(kvc, h) + krz for h in range(MLA_HEADS)]
        for h in range(MLA_HEADS):
            kml[0:PAST_LEN, h * LANES:(h + 1) * LANES] = mk_ctx[h].astype(BF16)
            v_ctx = _chunk(kvc, MLA_HEADS + h // 2)
            v_lat = v_ref[0, :, 384 + (h // 2) * LANES:384 + (h // 2 + 1) * LANES].astype(F32)
            vml[0:PAST_LEN, h * LANES:(h + 1) * LANES] = _ones_in_other_half(v_ctx, h % 2).astype(BF16)
            vml[PAST_LEN:N_KEYS, h * LANES:(h + 1) * LANES] = _ones_in_other_half(v_lat, h % 2).astype(BF16)
        kml[PAST_LEN:N_KEYS, :] = k_ref[0, :, O_MK:K_W]
        gk = gk_ref[...]
        k_ctx = jnp.concatenate([cdk_ref[0, 0], csk_ref[0, 0]] + mk_ctx, axis=-1)
        kn_ctx = jnp.dot(_sq_bf16(k_ctx), gk, preferred_element_type=F32)
        kn_lat = jnp.dot(_sq_bf16(k_ref[0]), gk, preferred_element_type=F32)
        kmax_sq[...] = jnp.maximum(jnp.max(kn_ctx, axis=0, keepdims=True), jnp.max(kn_lat, axis=0, keepdims=True))

    bound_sq = _logit_bound_sq(q_ref[0], gq_ref[...], kmax_sq[...])
    small = (jnp.max(bound_sq) <= NOSHIFT_BOUND ** 2) & (scal_ref[l, S_SINK_ABSMAX] <= NOSHIFT_BOUND)

    lam = _diff_lambda(lamv_ref, scal_ref[l, S_LAM_INIT])
    subg = subg_ref[0]
    sinks = [scal_ref[l, h] for h in range(SW_HEADS)]
    q0 = qb * TQ
    start = pl.multiple_of(jnp.clip(q0 - WINDOW, 0, DEC_SEQ - SW_BAND), LANES)

    def attend(shift):
        da_o = _diff_attention(q_ref[0, :, 0:256].astype(F32), lambda: kda[...],
                               lambda c: vda[:, c * LANES:(c + 1) * LANES], lam, shift)
        for c, o in enumerate(da_o):
            mix_ref[0, :, c * LANES:(c + 1) * LANES] = _diff_subln(
                o, _chunk(subg, c), scal_ref[l, S_ONE_MINUS]).astype(BF16)

        band = pl.ds(PAST_LEN + start, SW_BAND)
        k_sel = jnp.concatenate([ksw[0:PAST_LEN, :], ksw[band, :]], axis=0)
        v_sel = jnp.concatenate([vsw[0:PAST_LEN, :], vsw[band, :]], axis=0)
        n_sel = PAST_LEN + SW_BAND
        col = lax.broadcasted_iota(jnp.int32, (TQ, n_sel), 1)
        row = lax.broadcasted_iota(jnp.int32, (TQ, n_sel), 0)
        dist = (q0 + row) - (start + col - PAST_LEN)
        valid = (col < PAST_LEN) | ((dist <= WINDOW) & (dist >= -WINDOW))
        sw_o = _swa_attention(q_ref[0, :, 256:768].astype(F32), k_sel, v_sel, sinks, shift, valid)
        for c, o in enumerate(sw_o):
            mix_ref[0, :, 256 + c * LANES:256 + (c + 1) * LANES] = o.astype(BF16)

        mla_o = _mla_attention(lambda h: q_ref[0, :, O_MQ + h * LANES:O_MQ + (h + 1) * LANES],
                               lambda h: kml[:, h * LANES:(h + 1) * LANES],
                               lambda h: vml[:, h * LANES:(h + 1) * LANES], shift)
        for c, o in enumerate(mla_o):
            mix_ref[0, :, 768 + c * LANES:768 + (c + 1) * LANES] = o.astype(BF16)

    pl.when(small)(functools.partial(attend, False))
    pl.when(jnp.logical_not(small))(functools.partial(attend, True))


def _lat_attend(l_idx, q, k, v, caches, p, pad_eye):
    cdk, cdv, csk, csv, cckv, ckr = caches

    def cache_spec(a):
        return pl.BlockSpec((1, 1) + a.shape[2:], lambda b, j, l: (b, l[0], 0, 0))

    grid_spec = pltpu.PrefetchScalarGridSpec(
        num_scalar_prefetch=1,
        grid=(DEC_BATCH, DEC_SEQ // TQ),
        in_specs=[
            pl.BlockSpec((1, TQ, Q_W), lambda b, j, l: (b, j, 0)),
            pl.BlockSpec((1, DEC_SEQ, K_W), lambda b, j, l: (b, 0, 0)),
            pl.BlockSpec((1, DEC_SEQ, V_W), lambda b, j, l: (b, 0, 0)),
            cache_spec(cdk), cache_spec(cdv), cache_spec(csk), cache_spec(csv), cache_spec(cckv), cache_spec(ckr),
            _layer_spec(p["mla_w_kvb"].shape),
            pl.BlockSpec(pad_eye.shape, lambda b, j, l: (0, 0)),
            _layer_spec(p["lamv"].shape),
            _layer_spec(p["subg"].shape),
            pl.BlockSpec(p["gq"].shape, lambda b, j, l: (0, 0)),
            pl.BlockSpec(p["gk"].shape, lambda b, j, l: (0, 0)),
            pl.BlockSpec(memory_space=pltpu.SMEM),
        ],
        out_specs=pl.BlockSpec((1, TQ, D_MODEL), lambda b, j, l: (b, j, 0)),
        scratch_shapes=[
            pltpu.VMEM((N_KEYS, 256), BF16), pltpu.VMEM((N_KEYS, 256), BF16),
            pltpu.VMEM((N_KEYS, 128), BF16), pltpu.VMEM((N_KEYS, 256), BF16),
            pltpu.VMEM((N_KEYS, MLA_CHUNKS), BF16), pltpu.VMEM((N_KEYS, MLA_CHUNKS), BF16),
            pltpu.VMEM((1, LANES), F32),
        ],
    )
    return pl.pallas_call(
        _lat_attn_kernel,
        grid_spec=grid_spec,
        out_shape=jax.ShapeDtypeStruct((DEC_BATCH, DEC_SEQ, D_MODEL), BF16),
        compiler_params=pltpu.CompilerParams(
            dimension_semantics=("arbitrary", "arbitrary"), vmem_limit_bytes=VMEM_LIMIT),
        name="latent_attend",
    )(l_idx, q.reshape(DEC_BATCH, DEC_SEQ, Q_W), k.reshape(DEC_BATCH, DEC_SEQ, K_W),
      v.reshape(DEC_BATCH, DEC_SEQ, V_W), cdk, cdv, csk, csv, cckv, ckr,
      p["mla_w_kvb"], pad_eye, p["lamv"], p["subg"], p["gq"], p["gk"], p["scal"])


def _mlp_kernel(l_ref, x_ref, mix_ref, mod_ref, n2g_ref, w_out_ref, w_up_ref, w_down_ref, fg_ref, o_ref,
                *, final_norm):
    m = mod_ref[0, 0]
    x = x_ref[...] + m[2:3] * jnp.dot(mix_ref[...], w_out_ref[0], preferred_element_type=F32)
    h = (_rms(x, n2g_ref[0]) * (1.0 + m[4:5]) + m[3:4]).astype(BF16)
    acc = jnp.zeros_like(x)
    for c in range(D_FF // FF_CHUNK):
        u = jnp.dot(h, w_up_ref[0, :, c * FF_CHUNK:(c + 1) * FF_CHUNK], preferred_element_type=F32)
        a = jnp.square(jnp.maximum(u, 0.0)).astype(BF16)
        acc = acc + jnp.dot(a, w_down_ref[0, c * FF_CHUNK:(c + 1) * FF_CHUNK, :], preferred_element_type=F32)
    x = x + m[5:6] * acc
    if final_norm:
        x = _rms(x, fg_ref[...])
    o_ref[...] = x


def _mlp_layer(l_idx, x, mix, mod, p, final_g, mod_row, final_norm):
    n_tok = x.shape[0]
    grid_spec = pltpu.PrefetchScalarGridSpec(
        num_scalar_prefetch=1,
        grid=(n_tok // TM_MLP,),
        in_specs=[
            pl.BlockSpec((TM_MLP, D_MODEL), lambda i, l: (i, 0)),
            pl.BlockSpec((TM_MLP, D_MODEL), lambda i, l: (i, 0)),
            pl.BlockSpec((1, 1, 6, D_MODEL), lambda i, l: (l[0], mod_row(i), 0, 0)),
            _layer_spec(p["norm2_g"].shape),
            _layer_spec(p["w_out"].shape),
            _layer_spec(p["w_up"].shape),
            _layer_spec(p["w_down"].shape),
            pl.BlockSpec((1, D_MODEL), lambda i, l: (0, 0)),
        ],
        out_specs=pl.BlockSpec((TM_MLP, D_MODEL), lambda i, l: (i, 0)),
    )
    return pl.pallas_call(
        functools.partial(_mlp_kernel, final_norm=final_norm),
        grid_spec=grid_spec,
        out_shape=jax.ShapeDtypeStruct((n_tok, D_MODEL), F32),
        compiler_params=pltpu.CompilerParams(
            dimension_semantics=("arbitrary",), vmem_limit_bytes=VMEM_LIMIT),
        name="out_proj_mlp",
    )(l_idx, x, mix, mod, p["norm2_g"], p["w_out"], p["w_up"], p["w_down"], final_g)


def _rope_tables(period, active_lanes=LANES):
    m = period // 2
    half = m // 2
    t = np.arange(DEC_SEQ)
    pos = np.stack([t // GRID_W, t % GRID_W], axis=0).astype(np.float64)
    lane = np.arange(LANES) % period
    axis = lane // m
    k = lane % m
    freq = ROPE_BASE ** (-(k % half).astype(np.float64) / half)
    ang = pos[axis, :].T * freq[None, :]
    first = (k < half)[None, :]
    active = (np.arange(LANES) < active_lanes)[None, :]
    cos, sin = np.where(active, np.cos(ang), 1.0), np.where(active, np.sin(ang), 0.0)
    return (jnp.asarray(cos, F32), jnp.asarray(np.where(first, -sin, 0.0), F32),
            jnp.asarray(np.where(first, 0.0, sin), F32))


def _group_matrices():
    gq = np.zeros((Q_W, LANES), np.float32)
    gk = np.zeros((K_W, LANES), np.float32)
    for d in range(DA_HEADS * 2 * DA_QK):
        gq[d, G_DIFF + d // DA_QK] = 1.0
        gk[d, G_DIFF + d // DA_QK] = 1.0
    for c in range(SW_HEADS // 2):
        for par in range(2):
            col = G_SWA + 2 * c + par
            gq[256 + c * LANES + par * SW_DIM:256 + c * LANES + (par + 1) * SW_DIM, col] = 1.0
            gk[256 + par * SW_DIM:256 + (par + 1) * SW_DIM, col] = 1.0
    for h in range(MLA_HEADS):
        gq[O_MQ + h * LANES:O_MQ + (h + 1) * LANES, G_MLA + h] = 1.0
        gk[O_MK + h * LANES:O_MK + (h + 1) * LANES, G_MLA + h] = 1.0
    return jnp.asarray(gq, BF16), jnp.asarray(gk, BF16)


def _paired_head_order():
    order = []
    for c in range(SW_HEADS // 2):
        order += [c, c + SW_HEADS // 2]
    return order


def _prepare_params(norm1_g, norm2_g, w_in, w_out, lq1, lk1, lq2, lk2, diff_subln_g, swa_sink,
                    mla_q_norm_g, mla_w_qb, mla_kv_norm_g, mla_w_kvb, w_up, w_down):
    heads = _paired_head_order()
    o_swq = 768
    swq_cols = [w_in[:, :, o_swq + h * SW_DIM:o_swq + (h + 1) * SW_DIM] for h in heads]
    kr_cols = [w_in[:, :, 1920:1952], jnp.zeros((DEPTH, D_MODEL, LANES - MLA_ROPE), w_in.dtype)]
    w_in_ext = jnp.concatenate([w_in[:, :, 0:768]] + swq_cols + [w_in[:, :, 1280:1920]] + kr_cols,
                               axis=-1).astype(BF16)
    o_swo = DA_HEADS * DA_V
    swo_rows = [w_out[:, o_swo + h * SW_DIM:o_swo + (h + 1) * SW_DIM, :] for h in heads]
    w_out_p = jnp.concatenate([w_out[:, 0:o_swo, :]] + swo_rows + [w_out[:, o_swo + SW_HEADS * SW_DIM:, :]],
                              axis=1).astype(BF16)
    dq = MLA_NOPE + MLA_ROPE
    gap = LANES - dq
    zq = jnp.zeros((DEPTH, MLA_Q_RANK, gap), mla_w_qb.dtype)
    wqb = jnp.concatenate(
        [blk for h in range(MLA_HEADS)
         for blk in (mla_w_qb[:, :, h * dq + MLA_NOPE:(h + 1) * dq], zq, mla_w_qb[:, :, h * dq:h * dq + MLA_NOPE])],
        axis=-1).astype(BF16)
    dkv = MLA_NOPE + MLA_V
    zk = jnp.zeros((DEPTH, MLA_KV_RANK, LANES - MLA_NOPE), mla_w_kvb.dtype)
    wkvb = jnp.concatenate(
        [blk for h in range(MLA_HEADS) for blk in (zk, mla_w_kvb[:, :, h * dkv:h * dkv + MLA_NOPE])]
        + [mla_w_kvb[:, :, h * dkv + MLA_NOPE:(h + 1) * dkv] for h in range(MLA_HEADS)],
        axis=-1).astype(BF16)
    lam_init = np.array([0.8 - 0.6 * math.exp(-0.3 * l) for l in range(DEPTH)], np.float32)
    sink_l2 = swa_sink.astype(F32) * LOG2E
    scal = jnp.concatenate([
        sink_l2,
        jnp.asarray(np.stack([lam_init, (1.0 - lam_init.astype(np.float64)).astype(np.float32)], axis=1)),
        jnp.max(jnp.abs(sink_l2), axis=1, keepdims=True),
        jnp.zeros((DEPTH, S_COLS - SW_HEADS - 3), F32)], axis=1)
    gq, gk = _group_matrices()
    return {
        "gq": gq,
        "gk": gk,
        "norm1_g": norm1_g.reshape(DEPTH, 1, D_MODEL),
        "norm2_g": norm2_g.reshape(DEPTH, 1, D_MODEL),
        "w_in": w_in_ext,
        "w_out": w_out_p,
        "mla_q_norm_g": mla_q_norm_g.reshape(DEPTH, 1, MLA_Q_RANK),
        "mla_w_qb": wqb,
        "mla_kv_norm_g": mla_kv_norm_g.reshape(DEPTH, 1, MLA_KV_RANK),
        "mla_w_kvb": wkvb,
        "lamv": jnp.stack([lq1, lk1, lq2, lk2], axis=1),
        "subg": jnp.tile(diff_subln_g, (1, DA_HEADS)).reshape(DEPTH, 1, DA_HEADS * DA_V),
        "scal": scal,
        "w_up": w_up.astype(BF16),
        "w_down": w_down.astype(BF16),
    }


def kernel(x_prompt, x_sample, c, cache_diff_k, cache_diff_v, cache_swa_k, cache_swa_v, cache_mla_ckv,
           cache_mla_krope, c_ctx, norm1_g, norm2_g, w_ada, b_ada, w_in, w_out, diff_lambda_q1,
           diff_lambda_k1, diff_lambda_q2, diff_lambda_k2, diff_subln_g, swa_sink, mla_q_norm_g,
           mla_w_qb, mla_kv_norm_g, mla_w_kvb, w_up, w_down, final_g):
    p = _prepare_params(norm1_g, norm2_g, w_in, w_out, diff_lambda_q1, diff_lambda_k1, diff_lambda_q2,
                        diff_lambda_k2, diff_subln_g, swa_sink, mla_q_norm_g, mla_w_qb, mla_kv_norm_g,
                        mla_w_kvb, w_up, w_down)
    cvecs = jnp.concatenate([c, c_ctx[None, :], jnp.zeros((8 - DEC_BATCH - 1, D_MODEL), F32)], axis=0)
    mod = _modulation(cvecs, w_ada, b_ada)
    tables = _rope_tables(DA_QK) + _rope_tables(SW_DIM) + _rope_tables(MLA_ROPE, MLA_ROPE)
    pad_eye = jnp.asarray(np.eye(MLA_ROPE, LANES, dtype=np.float32), BF16)
    final_g2 = final_g.reshape(1, D_MODEL)
    caches = (cache_diff_k.reshape(DEC_BATCH, DEPTH, PAST_LEN, DA_HEADS * 2 * DA_QK),
              cache_diff_v.reshape(DEC_BATCH, DEPTH, PAST_LEN, DA_HEADS * DA_V),
              cache_swa_k.reshape(DEC_BATCH, DEPTH, PAST_LEN, SW_KV_HEADS * SW_DIM),
              cache_swa_v.reshape(DEC_BATCH, DEPTH, PAST_LEN, SW_KV_HEADS * SW_DIM),
              cache_mla_ckv, cache_mla_krope)

    x_ctx = x_prompt.reshape(N_CTX_TOK, D_MODEL)
    x_lat = x_sample.reshape(N_LAT_TOK, D_MODEL)
    new = None
    lat_tiles_per_req = DEC_SEQ // TM_MLP
    for l in range(DEPTH):
        l_idx = jnp.full((1,), l, jnp.int32)
        last = l == DEPTH - 1
        *new, mix_ctx = _ctx_layer(l_idx, x_ctx, mod, p, new)
        x_ctx = _mlp_layer(l_idx, x_ctx, mix_ctx, mod, p, final_g2, lambda i: CTX_MOD_ROW, last)
        q, k, v = _lat_project(l_idx, x_lat, mod, p, tables)
        mix_lat = _lat_attend(l_idx, q, k, v, caches, p, pad_eye).reshape(N_LAT_TOK, D_MODEL)
        x_lat = _mlp_layer(l_idx, x_lat, mix_lat, mod, p, final_g2, lambda i: i // lat_tiles_per_req, last)

    y_prompt = x_ctx.reshape(BATCH, SEQ, D_MODEL)
    y_sample = x_lat.reshape(DEC_BATCH, DEC_SEQ, D_MODEL)
    new_dk, new_dv, new_sk, new_sv, new_ckv, new_kr = new
    return (y_prompt, y_sample,
            new_dk.reshape(BATCH, DEPTH, SEQ, DA_HEADS, 2 * DA_QK),
            new_dv.reshape(BATCH, DEPTH, SEQ, DA_HEADS, DA_V),
            new_sk.reshape(BATCH, DEPTH, SEQ, SW_KV_HEADS, SW_DIM),
            new_sv.reshape(BATCH, DEPTH, SEQ, SW_KV_HEADS, SW_DIM),
            new_ckv, new_kr)
```

```python
import functools
import math

import numpy as np
import jax
import jax.numpy as jnp
from jax import lax
from jax.experimental import pallas as pl
from jax.experimental.pallas import tpu as pltpu

F32 = jnp.float32
BF16 = jnp.bfloat16

D_MODEL = 1024
BATCH = 32
SEQ = 256
DEPTH = 4
DEC_BATCH = 4
DEC_SEQ = 2048
PAST_LEN = 256
GRID_W = 64
WINDOW = 128
ROPE_BASE = 10000.0
EPS = 1e-6
DA_HEADS = 4
DA_QK = 32
DA_V = 64
SW_HEADS = 8
SW_KV_HEADS = 2
SW_DIM = 64
MLA_HEADS = 4
MLA_Q_RANK = 256
MLA_KV_RANK = 128
MLA_NOPE = 64
MLA_ROPE = 32
MLA_V = 64
D_FF = 4 * D_MODEL

LANES = 128
N_CTX_TOK = BATCH * SEQ
N_LAT_TOK = DEC_BATCH * DEC_SEQ
N_KEYS = PAST_LEN + DEC_SEQ
CTX_MOD_ROW = DEC_BATCH

C_DAQ, C_DAK, C_DAV, C_SWQ, C_SWK, C_SWV, C_QA, C_CKV, C_KR, P_EXT = (
    0, 256, 512, 768, 1280, 1408, 1536, 1792, 1920, 2048)
MLA_CHUNKS = MLA_HEADS * 128
Q_W = 256 + 512 + MLA_CHUNKS
K_W = 256 + 128 + MLA_CHUNKS
V_W = 256 + 128 + 256
O_MQ = 768
O_MK = 384
LOG2E = 1.4426950408889634

DA_SCALE = DA_QK ** -0.5
SW_SCALE = SW_DIM ** -0.5
MLA_SCALE = (MLA_NOPE + MLA_ROPE) ** -0.5

CTX_REQ_PER_STEP = 1
TM_CTX = CTX_REQ_PER_STEP * SEQ
TM_LAT = 512
TM_MLP = 512
TQ = 256
SW_BAND = TQ + 2 * WINDOW
FF_CHUNK = 1024
MOD_TILE = 1536
VMEM_LIMIT = 56 * 1024 * 1024

_NT = (((1,), (1,)), ((), ()))


def _scores(q, k):
    return lax.dot_general(q, k, _NT, preferred_element_type=F32)

S_LAM_INIT = SW_HEADS
S_ONE_MINUS = SW_HEADS + 1
S_SINK_ABSMAX = SW_HEADS + 2
S_COLS = 16

NOSHIFT_BOUND = 50.0
G_DIFF, G_SWA, G_MLA = 0, 8, 16


def _rms(x, g):
    ms = jnp.mean(x * x, axis=-1, keepdims=True)
    return x * lax.rsqrt(ms + EPS) * g


def _lane_mask(n, lo, width):
    lane = lax.broadcasted_iota(jnp.int32, (1, n), 1)
    return (lane >= lo) & (lane < lo + width)


def _exp_scores(s, shift, sink=None):
    if not shift:
        return jnp.exp2(s), jnp.zeros((1, 1), F32)
    m = jnp.max(s, axis=-1, keepdims=True)
    if sink is not None:
        m = jnp.maximum(m, sink)
    return jnp.exp2(s - m), m


def _sq_bf16(x):
    xf = x.astype(F32)
    return (xf * xf).astype(BF16)


def _logit_bound_sq(q_all, gq, kmax_sq):
    qn = jnp.dot(_sq_bf16(q_all), gq, preferred_element_type=F32)
    return jnp.max(qn, axis=0, keepdims=True) * kmax_sq * 1.03


def _chunk(a, j):
    return a[:, j * LANES:(j + 1) * LANES]


def _project(x_ref, mod_ref, n1g_ref, w_in_ref):
    m = mod_ref[0, 0]
    h = _rms(x_ref[...], n1g_ref[0]) * (1.0 + m[1:2]) + m[0:1]
    return jnp.dot(h.astype(BF16), w_in_ref[0], preferred_element_type=F32)


def _diff_lambda(lamv_ref, lam_init):
    lv = lamv_ref[0]
    a = jnp.sum(lv[0:1] * lv[1:2], axis=-1, keepdims=True)
    b = jnp.sum(lv[2:3] * lv[3:4], axis=-1, keepdims=True)
    return jnp.exp(a) - jnp.exp(b) + lam_init


def _half_mask(par):
    lane = lax.broadcasted_iota(jnp.int32, (1, LANES), 1)
    return (lane >= par * 64) & (lane < (par + 1) * 64)


def _diff_attention(qf, load_k, load_v, lam, shift):
    width = DA_HEADS * DA_V
    chunks = []
    for c in range(DA_HEADS // 2):
        o_par = []
        for par in range(2):
            lo = (2 * c + par) * DA_V
            q1 = jnp.where(_lane_mask(width, lo, DA_QK), qf, 0.0).astype(BF16)
            q2 = jnp.where(_lane_mask(width, lo + DA_QK, DA_QK), qf, 0.0).astype(BF16)
            e1, _ = _exp_scores(_scores(q1, load_k()), shift)
            e2, _ = _exp_scores(_scores(q2, load_k()), shift)
            d1 = jnp.sum(e1, axis=-1, keepdims=True)
            d2 = jnp.sum(e2, axis=-1, keepdims=True)
            w = (e1 - e2 * (lam * d1 * (1.0 / d2))).astype(BF16)
            o_par.append(jnp.dot(w, load_v(c), preferred_element_type=F32) * (1.0 / d1))
        chunks.append(jnp.where(_half_mask(0), o_par[0], o_par[1]))
    return chunks


def _diff_subln(o, subg, one_minus_lam_init):
    sq = o * o
    r = None
    for par in range(2):
        ms = jnp.sum(jnp.where(_half_mask(par), sq, 0.0), axis=-1, keepdims=True) * (1.0 / DA_V)
        rs = lax.rsqrt(ms + EPS)
        r = rs if r is None else jnp.where(_half_mask(0), r, rs)
    return o * r * subg * one_minus_lam_init


def _both_halves(x, g):
    swapped = pltpu.roll(x, 64, 1)
    return jnp.where(_half_mask(g), x, swapped)


def _swa_attention(qf, load_k, load_v, sinks, shift, valid=None):
    heads_per_group = SW_HEADS // SW_KV_HEADS
    outs = []
    for c in range(SW_HEADS // 2):
        qc = _chunk(qf, c)
        g = (2 * c) // heads_per_group
        o_par = []
        for par in range(2):
            h = 2 * c + par
            qm = jnp.where(_half_mask(par), qc, 0.0).astype(BF16)
            s = _scores(qm, load_k(g))
            if valid is not None:
                s = jnp.where(valid, s, -1e30)
            e, m = _exp_scores(s, shift, sink=sinks[h])
            o = jnp.dot(e.astype(BF16), load_v(g), preferred_element_type=F32)
            d = _chunk(o, 1) + jnp.exp2(sinks[h] - m)
            o_par.append(_chunk(o, 0) * (1.0 / d))
        outs.append(jnp.where(_half_mask(0), o_par[0], o_par[1]))
    return outs


def _mla_attention(load_q, load_k, load_v, shift):
    chunks = []
    for c in range(MLA_HEADS // 2):
        o_par = []
        for par in range(2):
            h = 2 * c + par
            e, _ = _exp_scores(_scores(load_q(h), load_k(h)), shift)
            o = jnp.dot(e.astype(BF16), load_v(h), preferred_element_type=F32)
            o_par.append(o * (1.0 / pltpu.roll(o, 64, 1)))
        chunks.append(jnp.where(_half_mask(0), o_par[0], o_par[1]))
    return chunks


def _ones_in_other_half(v, par):
    return jnp.where(_half_mask(par), v, 1.0)


def _mod_kernel(c_ref, w_ref, b_ref, o_ref):
    c = c_ref[...]
    s = c * (1.0 / (1.0 + jnp.exp(-c)))
    o_ref[0] = jnp.dot(s.astype(BF16), w_ref[0].astype(BF16), preferred_element_type=F32) + b_ref[0]


def _modulation(cvecs, w_ada, b_ada):
    n_out = 6 * D_MODEL
    out = pl.pallas_call(
        _mod_kernel,
        grid=(DEPTH, n_out // MOD_TILE),
        in_specs=[
            pl.BlockSpec((8, D_MODEL), lambda l, j: (0, 0)),
            pl.BlockSpec((1, D_MODEL, MOD_TILE), lambda l, j: (l, 0, j)),
            pl.BlockSpec((1, 1, MOD_TILE), lambda l, j: (l, 0, j)),
        ],
        out_specs=pl.BlockSpec((1, 8, MOD_TILE), lambda l, j: (l, 0, j)),
        out_shape=jax.ShapeDtypeStruct((DEPTH, 8, n_out), F32),
        compiler_params=pltpu.CompilerParams(
            dimension_semantics=("arbitrary", "arbitrary"), vmem_limit_bytes=VMEM_LIMIT),
        name="adaln_modulation",
    )(cvecs, w_ada, b_ada.reshape(DEPTH, 1, n_out))
    return out.reshape(DEPTH, 8, 6, D_MODEL)


def _ctx_kernel(l_ref, x_ref, mod_ref, n1g_ref, w_in_ref, qg_ref, wqb_ref, kvg_ref, wkvb_ref,
                lamv_ref, subg_ref, scal_ref, *rest, first_layer):
    dk_ref, dv_ref, sk_ref, sv_ref, ckv_ref, kr_ref, mix_ref = rest[-7:]
    l = l_ref[0]
    if first_layer:
        for ref in (dk_ref, dv_ref, sk_ref, sv_ref, ckv_ref, kr_ref):
            ref[:, 1:] = jnp.zeros((CTX_REQ_PER_STEP, DEPTH - 1) + ref.shape[2:], F32)
    proj = _project(x_ref, mod_ref, n1g_ref, w_in_ref)
    ckv_all = _rms(proj[:, C_CKV:C_KR], kvg_ref[0])
    mq_all = jnp.dot(_rms(proj[:, C_QA:C_CKV], qg_ref[0]).astype(BF16), wqb_ref[0], preferred_element_type=F32)
    mq_all = mq_all * (MLA_SCALE * LOG2E)
    kv_all = jnp.dot(ckv_all.astype(BF16), wkvb_ref[0], preferred_element_type=F32)
    lam = _diff_lambda(lamv_ref, scal_ref[l, S_LAM_INIT])
    subg = subg_ref[0]
    sinks = [scal_ref[l, h] for h in range(SW_HEADS)]
    shift = True

    for r in range(CTX_REQ_PER_STEP):
        r0, r1 = r * SEQ, (r + 1) * SEQ
        pr, mq, kv = proj[r0:r1], mq_all[r0:r1], kv_all[r0:r1]
        da_k = pr[:, C_DAK:C_DAV]
        da_v = pr[:, C_DAV:C_SWQ]
        sw_k = pr[:, C_SWK:C_SWV]
        sw_v = pr[:, C_SWV:C_QA]
        krz = pr[:, C_KR:P_EXT]
        dk_ref[r, 0] = da_k
        dv_ref[r, 0] = da_v
        sk_ref[r, 0] = sw_k
        sv_ref[r, 0] = sw_v
        ckv_ref[r, 0] = ckv_all[r0:r1]
        kr_ref[r, 0] = krz[:, 0:MLA_ROPE]

        da_kb = da_k.astype(BF16)
        da_o = _diff_attention(pr[:, C_DAQ:C_DAK] * (DA_SCALE * LOG2E), lambda: da_kb,
                               lambda c: _chunk(da_v, c).astype(BF16), lam, shift)
        for c, o in enumerate(da_o):
            mix_ref[r0:r1, c * LANES:(c + 1) * LANES] = _diff_subln(
                o, _chunk(subg, c), scal_ref[l, S_ONE_MINUS]).astype(BF16)
        sw_kg = [_both_halves(sw_k, g).astype(BF16) for g in range(SW_KV_HEADS)]
        sw_vg = [jnp.concatenate([_both_halves(sw_v, g), jnp.ones_like(sw_v)], axis=-1).astype(BF16)
                 for g in range(SW_KV_HEADS)]
        sw_o = _swa_attention(pr[:, C_SWQ:C_SWK] * (SW_SCALE * LOG2E), lambda g: sw_kg[g], lambda g: sw_vg[g],
                              sinks, shift)
        for c, o in enumerate(sw_o):
            mix_ref[r0:r1, 256 + c * LANES:256 + (c + 1) * LANES] = o.astype(BF16)
        mla_o = _mla_attention(
            lambda h: _chunk(mq, h).astype(BF16), lambda h: (_chunk(kv, h) + krz).astype(BF16),
            lambda h: _ones_in_other_half(_chunk(kv, MLA_HEADS + h // 2), h % 2).astype(BF16), shift)
        for c, o in enumerate(mla_o):
            mix_ref[r0:r1, 768 + c * LANES:768 + (c + 1) * LANES] = o.astype(BF16)


def _layer_spec(shape):
    nd = len(shape)
    return pl.BlockSpec((1,) + tuple(shape[1:]), lambda *a: (a[-1][0],) + (0,) * (nd - 1))


CACHE_WIDTHS = (DA_HEADS * 2 * DA_QK, DA_HEADS * DA_V, SW_KV_HEADS * SW_DIM, SW_KV_HEADS * SW_DIM,
                MLA_KV_RANK, MLA_ROPE)


def _ctx_layer(l_idx, x, mod, p, new_caches):
    n_req = BATCH // CTX_REQ_PER_STEP
    first_layer = new_caches is None
    if first_layer:
        new_caches = []
        out_shape = [jax.ShapeDtypeStruct((BATCH, DEPTH, SEQ, w), F32) for w in CACHE_WIDTHS]
        out_specs = [pl.BlockSpec((CTX_REQ_PER_STEP, DEPTH, SEQ, w), lambda i, l: (i, 0, 0, 0)) for w in CACHE_WIDTHS]
    else:
        out_shape = [jax.ShapeDtypeStruct(c.shape, c.dtype) for c in new_caches]
        out_specs = [pl.BlockSpec((CTX_REQ_PER_STEP, 1, SEQ, w), lambda i, l: (i, l[0], 0, 0)) for w in CACHE_WIDTHS]
    out_shape.append(jax.ShapeDtypeStruct((N_CTX_TOK, D_MODEL), BF16))
    out_specs.append(pl.BlockSpec((TM_CTX, D_MODEL), lambda i, l: (i, 0)))
    n_fixed_inputs = 12
    aliases = {n_fixed_inputs + j: j for j in range(len(new_caches))}
    grid_spec = pltpu.PrefetchScalarGridSpec(
        num_scalar_prefetch=1,
        grid=(n_req,),
        in_specs=[
            pl.BlockSpec((TM_CTX, D_MODEL), lambda i, l: (i, 0)),
            pl.BlockSpec((1, 1, 6, D_MODEL), lambda i, l: (l[0], CTX_MOD_ROW, 0, 0)),
            _layer_spec(p["norm1_g"].shape),
            _layer_spec(p["w_in"].shape),
            _layer_spec(p["mla_q_norm_g"].shape),
            _layer_spec(p["mla_w_qb"].shape),
            _layer_spec(p["mla_kv_norm_g"].shape),
            _layer_spec(p["mla_w_kvb"].shape),
            _layer_spec(p["lamv"].shape),
            _layer_spec(p["subg"].shape),
            pl.BlockSpec(memory_space=pltpu.SMEM),
        ] + [pl.BlockSpec(memory_space=pl.ANY)] * len(new_caches),
        out_specs=out_specs,
    )
    return pl.pallas_call(
        functools.partial(_ctx_kernel, first_layer=first_layer),
        grid_spec=grid_spec,
        out_shape=out_shape,
        input_output_aliases=aliases,
        compiler_params=pltpu.CompilerParams(
            dimension_semantics=("arbitrary",), vmem_limit_bytes=VMEM_LIMIT),
        name="ctx_project_attend",
    )(l_idx, x, mod, p["norm1_g"], p["w_in"], p["mla_q_norm_g"], p["mla_w_qb"], p["mla_kv_norm_g"],
      p["mla_w_kvb"], p["lamv"], p["subg"], p["scal"], *new_caches)


def _rope(x, cos, sa, sb, half):
    return x * cos + pltpu.roll(x, LANES - half, 1) * sa + pltpu.roll(x, half, 1) * sb


def _lat_proj_kernel(l_ref, x_ref, mod_ref, n1g_ref, w_in_ref, qg_ref, wqb_ref, kvg_ref, wkvb_ref,
                     c32_ref, a32_ref, b32_ref, c64_ref, a64_ref, b64_ref, cm_ref, am_ref, bm_ref,
                     q_ref, k_ref, v_ref):
    proj = _project(x_ref, mod_ref, n1g_ref, w_in_ref)
    t32 = (c32_ref[...], a32_ref[...], b32_ref[...], DA_QK // 4)
    t64 = (c64_ref[...], a64_ref[...], b64_ref[...], SW_DIM // 4)
    tml = (cm_ref[...], am_ref[...], bm_ref[...], MLA_ROPE // 4)

    for j in range(2):
        q = _rope(_chunk(proj, j), *t32) * (DA_SCALE * LOG2E)
        q_ref[:, j * LANES:(j + 1) * LANES] = q.astype(BF16)
    for j in range(4):
        q = _rope(_chunk(proj, C_SWQ // LANES + j), *t64) * (SW_SCALE * LOG2E)
        q_ref[:, 256 + j * LANES:256 + (j + 1) * LANES] = q.astype(BF16)
    mq = jnp.dot(_rms(proj[:, C_QA:C_CKV], qg_ref[0]).astype(BF16), wqb_ref[0], preferred_element_type=F32)
    for h in range(MLA_HEADS):
        q = _rope(_chunk(mq, h), *tml) * (MLA_SCALE * LOG2E)
        q_ref[:, O_MQ + h * LANES:O_MQ + (h + 1) * LANES] = q.astype(BF16)

    for j in range(2):
        k_ref[:, j * LANES:(j + 1) * LANES] = _rope(_chunk(proj, C_DAK // LANES + j), *t32).astype(BF16)
    k_ref[:, 256:384] = _rope(proj[:, C_SWK:C_SWV], *t64).astype(BF16)
    ckv = _rms(proj[:, C_CKV:C_KR], kvg_ref[0])
    kv = jnp.dot(ckv.astype(BF16), wkvb_ref[0], preferred_element_type=F32)
    krz = _rope(proj[:, C_KR:P_EXT], *tml)
    for h in range(MLA_HEADS):
        k_ref[:, O_MK + h * LANES:O_MK + (h + 1) * LANES] = (_chunk(kv, h) + krz).astype(BF16)

    v_ref[:, 0:256] = proj[:, C_DAV:C_SWQ].astype(BF16)
    v_ref[:, 256:384] = proj[:, C_SWV:C_QA].astype(BF16)
    v_ref[:, 384:640] = kv[:, MLA_CHUNKS:].astype(BF16)


def _lat_project(l_idx, x, mod, p, tables):
    tiles_per_req = DEC_SEQ // TM_LAT
    tab_spec = pl.BlockSpec((TM_LAT, LANES), lambda i, l: (i % tiles_per_req, 0))
    grid_spec = pltpu.PrefetchScalarGridSpec(
        num_scalar_prefetch=1,
        grid=(N_LAT_TOK // TM_LAT,),
        in_specs=[
            pl.BlockSpec((TM_LAT, D_MODEL), lambda i, l: (i, 0)),
            pl.BlockSpec((1, 1, 6, D_MODEL), lambda i, l: (l[0], i // tiles_per_req, 0, 0)),
            _layer_spec(p["norm1_g"].shape),
            _layer_spec(p["w_in"].shape),
            _layer_spec(p["mla_q_norm_g"].shape),
            _layer_spec(p["mla_w_qb"].shape),
            _layer_spec(p["mla_kv_norm_g"].shape),
            _layer_spec(p["mla_w_kvb"].shape),
        ] + [tab_spec] * len(tables),
        out_specs=[
            pl.BlockSpec((TM_LAT, Q_W), lambda i, l: (i, 0)),
            pl.BlockSpec((TM_LAT, K_W), lambda i, l: (i, 0)),
            pl.BlockSpec((TM_LAT, V_W), lambda i, l: (i, 0)),
        ],
    )
    return pl.pallas_call(
        _lat_proj_kernel,
        grid_spec=grid_spec,
        out_shape=[jax.ShapeDtypeStruct((N_LAT_TOK, Q_W), BF16),
                   jax.ShapeDtypeStruct((N_LAT_TOK, K_W), BF16),
                   jax.ShapeDtypeStruct((N_LAT_TOK, V_W), BF16)],
        compiler_params=pltpu.CompilerParams(
            dimension_semantics=("arbitrary",), vmem_limit_bytes=VMEM_LIMIT),
        name="latent_project",
    )(l_idx, x, mod, p["norm1_g"], p["w_in"], p["mla_q_norm_g"], p["mla_w_qb"], p["mla_kv_norm_g"],
      p["mla_w_kvb"], *tables)


def _lat_attn_kernel(l_ref, q_ref, k_ref, v_ref, cdk_ref, cdv_ref, csk_ref, csv_ref, cckv_ref, ckr_ref,
                     wkvb_ref, pad_ref, lamv_ref, subg_ref, gq_ref, gk_ref, scal_ref,
                     mix_ref,
                     kda, vda, ksw, vsw, kml, vml, kmax_sq):
    l = l_ref[0]
    qb = pl.program_id(1)

    @pl.when(qb == 0)
    def _stage_keys():
        kda[0:PAST_LEN, :] = cdk_ref[0, 0].astype(BF16)
        kda[PAST_LEN:N_KEYS, :] = k_ref[0, :, 0:256]
        vda[0:PAST_LEN, :] = cdv_ref[0, 0].astype(BF16)
        vda[PAST_LEN:N_KEYS, :] = v_ref[0, :, 0:256]
        sk_lat = k_ref[0, :, 256:384].astype(F32)
        sv_lat = v_ref[0, :, 256:384].astype(F32)
        for g in range(SW_KV_HEADS):
            ksw[0:PAST_LEN, g * LANES:(g + 1) * LANES] = _both_halves(csk_ref[0, 0], g).astype(BF16)
            ksw[PAST_LEN:N_KEYS, g * LANES:(g + 1) * LANES] = _both_halves(sk_lat, g).astype(BF16)
            vsw[0:PAST_LEN, 2 * g * LANES:(2 * g + 1) * LANES] = _both_halves(csv_ref[0, 0], g).astype(BF16)
            vsw[PAST_LEN:N_KEYS, 2 * g * LANES:(2 * g + 1) * LANES] = _both_halves(sv_lat, g).astype(BF16)
            vsw[:, (2 * g + 1) * LANES:(2 * g + 2) * LANES] = jnp.ones((N_KEYS, LANES), BF16)
        kvc = jnp.dot(cckv_ref[0, 0].astype(BF16), wkvb_ref[0], preferred_element_type=F32)
        krz = jnp.dot(ckr_ref[0, 0].astype(BF16), pad_ref[...], preferred_element_type=F32)
        mk_ctx = [_chunk(kvc, h) + krz for h in range(MLA_HEADS)]
        for h in range(MLA_HEADS):
            kml[0:PAST_LEN, h * LANES:(h + 1) * LANES] = mk_ctx[h].astype(BF16)
            v_ctx = _chunk(kvc, MLA_HEADS + h // 2)
            v_lat = v_ref[0, :, 384 + (h // 2) * LANES:384 + (h // 2 + 1) * LANES].astype(F32)
            vml[0:PAST_LEN, h * LANES:(h + 1) * LANES] = _ones_in_other_half(v_ctx, h % 2).astype(BF16)
            vml[PAST_LEN:N_KEYS, h * LANES:(h + 1) * LANES] = _ones_in_other_half(v_lat, h % 2).astype(BF16)
        kml[PAST_LEN:N_KEYS, :] = k_ref[0, :, O_MK:K_W]
        gk = gk_ref[...]
        k_ctx = jnp.concatenate([cdk_ref[0, 0], csk_ref[0, 0]] + mk_ctx, axis=-1)
        kn_ctx = jnp.dot(_sq_bf16(k_ctx), gk, preferred_element_type=F32)
        kn_lat = jnp.dot(_sq_bf16(k_ref[0]), gk, preferred_element_type=F32)
        kmax_sq[...] = jnp.maximum(jnp.max(kn_ctx, axis=0, keepdims=True), jnp.max(kn_lat, axis=0, keepdims=True))

    bound_sq = _logit_bound_sq(q_ref[0], gq_ref[...], kmax_sq[...])
    small = (jnp.max(bound_sq) <= NOSHIFT_BOUND ** 2) & (scal_ref[l, S_SINK_ABSMAX] <= NOSHIFT_BOUND)

    lam = _diff_lambda(lamv_ref, scal_ref[l, S_LAM_INIT])
    subg = subg_ref[0]
    sinks = [scal_ref[l, h] for h in range(SW_HEADS)]
    q0 = qb * TQ
    start = pl.multiple_of(jnp.clip(q0 - WINDOW, 0, DEC_SEQ - SW_BAND), LANES)

    def attend(shift):
        da_o = _diff_attention(q_ref[0, :, 0:256].astype(F32), lambda: kda[...],
                               lambda c: vda[:, c * LANES:(c + 1) * LANES], lam, shift)
        for c, o in enumerate(da_o):
            mix_ref[0, :, c * LANES:(c + 1) * LANES] = _diff_subln(
                o, _chunk(subg, c), scal_ref[l, S_ONE_MINUS]).astype(BF16)

        band = pl.ds(PAST_LEN + start, SW_BAND)
        k_sel = jnp.concatenate([ksw[0:PAST_LEN, :], ksw[band, :]], axis=0)
        v_sel = jnp.concatenate([vsw[0:PAST_LEN, :], vsw[band, :]], axis=0)
        n_sel = PAST_LEN + SW_BAND
        col = lax.broadcasted_iota(jnp.int32, (TQ, n_sel), 1)
        row = lax.broadcasted_iota(jnp.int32, (TQ, n_sel), 0)
        dist = (q0 + row) - (start + col - PAST_LEN)
        valid = (col < PAST_LEN) | ((dist <= WINDOW) & (dist >= -WINDOW))
        sw_o = _swa_attention(q_ref[0, :, 256:768].astype(F32), lambda g: _chunk(k_sel, g),
                              lambda g: v_sel[:, 2 * g * LANES:(2 * g + 2) * LANES], sinks, shift, valid)
        for c, o in enumerate(sw_o):
            mix_ref[0, :, 256 + c * LANES:256 + (c + 1) * LANES] = o.astype(BF16)

        mla_o = _mla_attention(lambda h: q_ref[0, :, O_MQ + h * LANES:O_MQ + (h + 1) * LANES],
                               lambda h: kml[:, h * LANES:(h + 1) * LANES],
                               lambda h: vml[:, h * LANES:(h + 1) * LANES], shift)
        for c, o in enumerate(mla_o):
            mix_ref[0, :, 768 + c * LANES:768 + (c + 1) * LANES] = o.astype(BF16)

    pl.when(small)(functools.partial(attend, False))
    pl.when(jnp.logical_not(small))(functools.partial(attend, True))


def _lat_attend(l_idx, q, k, v, caches, p, pad_eye):
    cdk, cdv, csk, csv, cckv, ckr = caches

    def cache_spec(a):
        return pl.BlockSpec((1, 1) + a.shape[2:], lambda b, j, l: (b, l[0], 0, 0))

    grid_spec = pltpu.PrefetchScalarGridSpec(
        num_scalar_prefetch=1,
        grid=(DEC_BATCH, DEC_SEQ // TQ),
        in_specs=[
            pl.BlockSpec((1, TQ, Q_W), lambda b, j, l: (b, j, 0)),
            pl.BlockSpec((1, DEC_SEQ, K_W), lambda b, j, l: (b, 0, 0)),
            pl.BlockSpec((1, DEC_SEQ, V_W), lambda b, j, l: (b, 0, 0)),
            cache_spec(cdk), cache_spec(cdv), cache_spec(csk), cache_spec(csv), cache_spec(cckv), cache_spec(ckr),
            _layer_spec(p["mla_w_kvb"].shape),
            pl.BlockSpec(pad_eye.shape, lambda b, j, l: (0, 0)),
            _layer_spec(p["lamv"].shape),
            _layer_spec(p["subg"].shape),
            pl.BlockSpec(p["gq"].shape, lambda b, j, l: (0, 0)),
            pl.BlockSpec(p["gk"].shape, lambda b, j, l: (0, 0)),
            pl.BlockSpec(memory_space=pltpu.SMEM),
        ],
        out_specs=pl.BlockSpec((1, TQ, D_MODEL), lambda b, j, l: (b, j, 0)),
        scratch_shapes=[
            pltpu.VMEM((N_KEYS, 256), BF16), pltpu.VMEM((N_KEYS, 256), BF16),
            pltpu.VMEM((N_KEYS, SW_KV_HEADS * LANES), BF16), pltpu.VMEM((N_KEYS, SW_KV_HEADS * 2 * LANES), BF16),
            pltpu.VMEM((N_KEYS, MLA_CHUNKS), BF16), pltpu.VMEM((N_KEYS, MLA_CHUNKS), BF16),
            pltpu.VMEM((1, LANES), F32),
        ],
    )
    return pl.pallas_call(
        _lat_attn_kernel,
        grid_spec=grid_spec,
        out_shape=jax.ShapeDtypeStruct((DEC_BATCH, DEC_SEQ, D_MODEL), BF16),
        compiler_params=pltpu.CompilerParams(
            dimension_semantics=("arbitrary", "arbitrary"), vmem_limit_bytes=VMEM_LIMIT),
        name="latent_attend",
    )(l_idx, q.reshape(DEC_BATCH, DEC_SEQ, Q_W), k.reshape(DEC_BATCH, DEC_SEQ, K_W),
      v.reshape(DEC_BATCH, DEC_SEQ, V_W), cdk, cdv, csk, csv, cckv, ckr,
      p["mla_w_kvb"], pad_eye, p["lamv"], p["subg"], p["gq"], p["gk"], p["scal"])


def _mlp_kernel(l_ref, x_ref, mix_ref, mod_ref, n2g_ref, w_out_ref, w_up_ref, w_down_ref, fg_ref, o_ref,
                *, final_norm):
    m = mod_ref[0, 0]
    x = x_ref[...] + m[2:3] * jnp.dot(mix_ref[...], w_out_ref[0], preferred_element_type=F32)
    h = (_rms(x, n2g_ref[0]) * (1.0 + m[4:5]) + m[3:4]).astype(BF16)
    acc = jnp.zeros_like(x)
    for c in range(D_FF // FF_CHUNK):
        u = jnp.dot(h, w_up_ref[0, :, c * FF_CHUNK:(c + 1) * FF_CHUNK], preferred_element_type=F32)
        a = jnp.square(jnp.maximum(u, 0.0)).astype(BF16)
        acc = acc + jnp.dot(a, w_down_ref[0, c * FF_CHUNK:(c + 1) * FF_CHUNK, :], preferred_element_type=F32)
    x = x + m[5:6] * acc
    if final_norm:
        x = _rms(x, fg_ref[...])
    o_ref[...] = x


def _mlp_layer(l_idx, x, mix, mod, p, final_g, mod_row, final_norm):
    n_tok = x.shape[0]
    grid_spec = pltpu.PrefetchScalarGridSpec(
        num_scalar_prefetch=1,
        grid=(n_tok // TM_MLP,),
        in_specs=[
            pl.BlockSpec((TM_MLP, D_MODEL), lambda i, l: (i, 0)),
            pl.BlockSpec((TM_MLP, D_MODEL), lambda i, l: (i, 0)),
            pl.BlockSpec((1, 1, 6, D_MODEL), lambda i, l: (l[0], mod_row(i), 0, 0)),
            _layer_spec(p["norm2_g"].shape),
            _layer_spec(p["w_out"].shape),
            _layer_spec(p["w_up"].shape),
            _layer_spec(p["w_down"].shape),
            pl.BlockSpec((1, D_MODEL), lambda i, l: (0, 0)),
        ],
        out_specs=pl.BlockSpec((TM_MLP, D_MODEL), lambda i, l: (i, 0)),
    )
    return pl.pallas_call(
        functools.partial(_mlp_kernel, final_norm=final_norm),
        grid_spec=grid_spec,
        out_shape=jax.ShapeDtypeStruct((n_tok, D_MODEL), F32),
        compiler_params=pltpu.CompilerParams(
            dimension_semantics=("arbitrary",), vmem_limit_bytes=VMEM_LIMIT),
        name="out_proj_mlp",
    )(l_idx, x, mix, mod, p["norm2_g"], p["w_out"], p["w_up"], p["w_down"], final_g)


def _rope_tables(period, active_lanes=LANES):
    m = period // 2
    half = m // 2
    t = np.arange(DEC_SEQ)
    pos = np.stack([t // GRID_W, t % GRID_W], axis=0).astype(np.float64)
    lane = np.arange(LANES) % period
    axis = lane // m
    k = lane % m
    freq = ROPE_BASE ** (-(k % half).astype(np.float64) / half)
    ang = pos[axis, :].T * freq[None, :]
    first = (k < half)[None, :]
    active = (np.arange(LANES) < active_lanes)[None, :]
    cos, sin = np.where(active, np.cos(ang), 1.0), np.where(active, np.sin(ang), 0.0)
    return (jnp.asarray(cos, F32), jnp.asarray(np.where(first, -sin, 0.0), F32),
            jnp.asarray(np.where(first, 0.0, sin), F32))


def _group_matrices():
    gq = np.zeros((Q_W, LANES), np.float32)
    gk = np.zeros((K_W, LANES), np.float32)
    for d in range(DA_HEADS * 2 * DA_QK):
        gq[d, G_DIFF + d // DA_QK] = 1.0
        gk[d, G_DIFF + d // DA_QK] = 1.0
    for h in range(SW_HEADS):
        g = h // (SW_HEADS // SW_KV_HEADS)
        gq[256 + h * SW_DIM:256 + (h + 1) * SW_DIM, G_SWA + h] = 1.0
        gk[256 + g * SW_DIM:256 + (g + 1) * SW_DIM, G_SWA + h] = 1.0
    for h in range(MLA_HEADS):
        gq[O_MQ + h * LANES:O_MQ + (h + 1) * LANES, G_MLA + h] = 1.0
        gk[O_MK + h * LANES:O_MK + (h + 1) * LANES, G_MLA + h] = 1.0
    return jnp.asarray(gq, BF16), jnp.asarray(gk, BF16)


def _prepare_params(norm1_g, norm2_g, w_in, w_out, lq1, lk1, lq2, lk2, diff_subln_g, swa_sink,
                    mla_q_norm_g, mla_w_qb, mla_kv_norm_g, mla_w_kvb, w_up, w_down):
    w_in_ext = jnp.pad(w_in.astype(BF16), ((0, 0), (0, 0), (0, P_EXT - w_in.shape[-1])))
    w_out_p = w_out.astype(BF16)
    dq = MLA_NOPE + MLA_ROPE
    gap = LANES - dq
    zq = jnp.zeros((DEPTH, MLA_Q_RANK, gap), mla_w_qb.dtype)
    wqb = jnp.concatenate(
        [blk for h in range(MLA_HEADS)
         for blk in (mla_w_qb[:, :, h * dq + MLA_NOPE:(h + 1) * dq], zq, mla_w_qb[:, :, h * dq:h * dq + MLA_NOPE])],
        axis=-1).astype(BF16)
    dkv = MLA_NOPE + MLA_V
    zk = jnp.zeros((DEPTH, MLA_KV_RANK, LANES - MLA_NOPE), mla_w_kvb.dtype)
    wkvb = jnp.concatenate(
        [blk for h in range(MLA_HEADS) for blk in (zk, mla_w_kvb[:, :, h * dkv:h * dkv + MLA_NOPE])]
        + [mla_w_kvb[:, :, h * dkv + MLA_NOPE:(h + 1) * dkv] for h in range(MLA_HEADS)],
        axis=-1).astype(BF16)
    lam_init = np.array([0.8 - 0.6 * math.exp(-0.3 * l) for l in range(DEPTH)], np.float32)
    sink_l2 = swa_sink.astype(F32) * LOG2E
    scal = jnp.concatenate([
        sink_l2,
        jnp.asarray(np.stack([lam_init, (1.0 - lam_init.astype(np.float64)).astype(np.float32)], axis=1)),
        jnp.max(jnp.abs(sink_l2), axis=1, keepdims=True),
        jnp.zeros((DEPTH, S_COLS - SW_HEADS - 3), F32)], axis=1)
    gq, gk = _group_matrices()
    return {
        "gq": gq,
        "gk": gk,
        "norm1_g": norm1_g.reshape(DEPTH, 1, D_MODEL),
        "norm2_g": norm2_g.reshape(DEPTH, 1, D_MODEL),
        "w_in": w_in_ext,
        "w_out": w_out_p,
        "mla_q_norm_g": mla_q_norm_g.reshape(DEPTH, 1, MLA_Q_RANK),
        "mla_w_qb": wqb,
        "mla_kv_norm_g": mla_kv_norm_g.reshape(DEPTH, 1, MLA_KV_RANK),
        "mla_w_kvb": wkvb,
        "lamv": jnp.stack([lq1, lk1, lq2, lk2], axis=1),
        "subg": jnp.tile(diff_subln_g, (1, DA_HEADS)).reshape(DEPTH, 1, DA_HEADS * DA_V),
        "scal": scal,
        "w_up": w_up.astype(BF16),
        "w_down": w_down.astype(BF16),
    }


def kernel(x_prompt, x_sample, c, cache_diff_k, cache_diff_v, cache_swa_k, cache_swa_v, cache_mla_ckv,
           cache_mla_krope, c_ctx, norm1_g, norm2_g, w_ada, b_ada, w_in, w_out, diff_lambda_q1,
           diff_lambda_k1, diff_lambda_q2, diff_lambda_k2, diff_subln_g, swa_sink, mla_q_norm_g,
           mla_w_qb, mla_kv_norm_g, mla_w_kvb, w_up, w_down, final_g):
    p = _prepare_params(norm1_g, norm2_g, w_in, w_out, diff_lambda_q1, diff_lambda_k1, diff_lambda_q2,
                        diff_lambda_k2, diff_subln_g, swa_sink, mla_q_norm_g, mla_w_qb, mla_kv_norm_g,
                        mla_w_kvb, w_up, w_down)
    cvecs = jnp.concatenate([c, c_ctx[None, :], jnp.zeros((8 - DEC_BATCH - 1, D_MODEL), F32)], axis=0)
    mod = _modulation(cvecs, w_ada, b_ada)
    tables = _rope_tables(DA_QK) + _rope_tables(SW_DIM) + _rope_tables(MLA_ROPE, MLA_ROPE)
    pad_eye = jnp.asarray(np.eye(MLA_ROPE, LANES, dtype=np.float32), BF16)
    final_g2 = final_g.reshape(1, D_MODEL)
    caches = (cache_diff_k.reshape(DEC_BATCH, DEPTH, PAST_LEN, DA_HEADS * 2 * DA_QK),
              cache_diff_v.reshape(DEC_BATCH, DEPTH, PAST_LEN, DA_HEADS * DA_V),
              cache_swa_k.reshape(DEC_BATCH, DEPTH, PAST_LEN, SW_KV_HEADS * SW_DIM),
              cache_swa_v.reshape(DEC_BATCH, DEPTH, PAST_LEN, SW_KV_HEADS * SW_DIM),
              cache_mla_ckv, cache_mla_krope)

    x_ctx = x_prompt.reshape(N_CTX_TOK, D_MODEL)
    x_lat = x_sample.reshape(N_LAT_TOK, D_MODEL)
    new = None
    lat_tiles_per_req = DEC_SEQ // TM_MLP
    for l in range(DEPTH):
        l_idx = jnp.full((1,), l, jnp.int32)
        last = l == DEPTH - 1
        *new, mix_ctx = _ctx_layer(l_idx, x_ctx, mod, p, new)
        x_ctx = _mlp_layer(l_idx, x_ctx, mix_ctx, mod, p, final_g2, lambda i: CTX_MOD_ROW, last)
        q, k, v = _lat_project(l_idx, x_lat, mod, p, tables)
        mix_lat = _lat_attend(l_idx, q, k, v, caches, p, pad_eye).reshape(N_LAT_TOK, D_MODEL)
        x_lat = _mlp_layer(l_idx, x_lat, mix_lat, mod, p, final_g2, lambda i: i // lat_tiles_per_req, last)

    y_prompt = x_ctx.reshape(BATCH, SEQ, D_MODEL)
    y_sample = x_lat.reshape(DEC_BATCH, DEC_SEQ, D_MODEL)
    new_dk, new_dv, new_sk, new_sv, new_ckv, new_kr = new
    return (y_prompt, y_sample,
            new_dk.reshape(BATCH, DEPTH, SEQ, DA_HEADS, 2 * DA_QK),
            new_dv.reshape(BATCH, DEPTH, SEQ, DA_HEADS, DA_V),
            new_sk.reshape(BATCH, DEPTH, SEQ, SW_KV_HEADS, SW_DIM),
            new_sv.reshape(BATCH, DEPTH, SEQ, SW_KV_HEADS, SW_DIM),
            new_ckv, new_kr)
```

```python
import functools
import math

import numpy as np
import jax
import jax.numpy as jnp
from jax import lax
from jax.experimental import pallas as pl
from jax.experimental.pallas import tpu as pltpu

F32 = jnp.float32
BF16 = jnp.bfloat16

D_MODEL = 1024
BATCH = 32
SEQ = 256
DEPTH = 4
DEC_BATCH = 4
DEC_SEQ = 2048
PAST_LEN = 256
GRID_W = 64
WINDOW = 128
ROPE_BASE = 10000.0
EPS = 1e-6
DA_HEADS = 4
DA_QK = 32
DA_V = 64
SW_HEADS = 8
SW_KV_HEADS = 2
SW_DIM = 64
MLA_HEADS = 4
MLA_Q_RANK = 256
MLA_KV_RANK = 128
MLA_NOPE = 64
MLA_ROPE = 32
MLA_V = 64
D_FF = 4 * D_MODEL

LANES = 128
N_CTX_TOK = BATCH * SEQ
N_LAT_TOK = DEC_BATCH * DEC_SEQ
N_KEYS = PAST_LEN + DEC_SEQ
CTX_MOD_ROW = DEC_BATCH

C_DAQ, C_DAK, C_DAV, C_SWQ, C_SWK, C_SWV, C_QA, C_CKV, C_KR, P_EXT = (
    0, 256, 512, 768, 1280, 1408, 1536, 1792, 1920, 2048)
MLA_CHUNKS = MLA_HEADS * 128
Q_W = 256 + 512 + MLA_CHUNKS
K_W = 256 + 128 + MLA_CHUNKS
V_W = 256 + 128 + 256
O_MQ = 768
O_MK = 384
LOG2E = 1.4426950408889634

DA_SCALE = DA_QK ** -0.5
SW_SCALE = SW_DIM ** -0.5
MLA_SCALE = (MLA_NOPE + MLA_ROPE) ** -0.5

CTX_REQ_PER_STEP = 1
TM_CTX = CTX_REQ_PER_STEP * SEQ
TM_LAT = 1024
TM_MLP = 1024
TQ = 256
SW_BAND = TQ + 2 * WINDOW
FF_CHUNK = 1024
MOD_TILE = 1536
VMEM_LIMIT = 56 * 1024 * 1024

_NT = (((1,), (1,)), ((), ()))


def _scores(q, k):
    return lax.dot_general(q, k, _NT, preferred_element_type=F32)

S_LAM_INIT = SW_HEADS
S_ONE_MINUS = SW_HEADS + 1
S_SINK_ABSMAX = SW_HEADS + 2
S_COLS = 16

NOSHIFT_BOUND = 50.0
G_DIFF, G_SWA, G_MLA = 0, 8, 16


def _rms(x, g):
    ms = jnp.mean(x * x, axis=-1, keepdims=True)
    return x * lax.rsqrt(ms + EPS) * g


def _lane_mask(n, lo, width):
    lane = lax.broadcasted_iota(jnp.int32, (1, n), 1)
    return (lane >= lo) & (lane < lo + width)


def _exp_scores(s, shift, sink=None):
    if not shift:
        return jnp.exp2(s), jnp.zeros((1, 1), F32)
    m = jnp.max(s, axis=-1, keepdims=True)
    if sink is not None:
        m = jnp.maximum(m, sink)
    return jnp.exp2(s - m), m


def _sq_bf16(x):
    xf = x.astype(F32)
    return (xf * xf).astype(BF16)


def _logit_bound_sq(q_all, gq, kmax_sq):
    qn = jnp.dot(_sq_bf16(q_all), gq, preferred_element_type=F32)
    return jnp.max(qn, axis=0, keepdims=True) * kmax_sq * 1.03


def _chunk(a, j):
    return a[:, j * LANES:(j + 1) * LANES]


def _project(x_ref, mod_ref, n1g_ref, w_in_ref):
    m = mod_ref[0, 0]
    h = _rms(x_ref[...], n1g_ref[0]) * (1.0 + m[1:2]) + m[0:1]
    return jnp.dot(h.astype(BF16), w_in_ref[0], preferred_element_type=F32)


def _diff_lambda(lamv_ref, lam_init):
    lv = lamv_ref[0]
    a = jnp.sum(lv[0:1] * lv[1:2], axis=-1, keepdims=True)
    b = jnp.sum(lv[2:3] * lv[3:4], axis=-1, keepdims=True)
    return jnp.exp(a) - jnp.exp(b) + lam_init


def _half_mask(par):
    lane = lax.broadcasted_iota(jnp.int32, (1, LANES), 1)
    return (lane >= par * 64) & (lane < (par + 1) * 64)


def _diff_attention(qf, load_k, load_v, lam, shift):
    width = DA_HEADS * DA_V
    chunks = []
    for c in range(DA_HEADS // 2):
        o_par = []
        for par in range(2):
            lo = (2 * c + par) * DA_V
            q1 = jnp.where(_lane_mask(width, lo, DA_QK), qf, 0.0).astype(BF16)
            q2 = jnp.where(_lane_mask(width, lo + DA_QK, DA_QK), qf, 0.0).astype(BF16)
            e1, _ = _exp_scores(_scores(q1, load_k()), shift)
            e2, _ = _exp_scores(_scores(q2, load_k()), shift)
            d1 = jnp.sum(e1, axis=-1, keepdims=True)
            d2 = jnp.sum(e2, axis=-1, keepdims=True)
            w = (e1 - e2 * (lam * d1 * (1.0 / d2))).astype(BF16)
            o_par.append(jnp.dot(w, load_v(c), preferred_element_type=F32) * (1.0 / d1))
        chunks.append(jnp.where(_half_mask(0), o_par[0], o_par[1]))
    return chunks


def _diff_subln(o, subg, one_minus_lam_init):
    sq = o * o
    r = None
    for par in range(2):
        ms = jnp.sum(jnp.where(_half_mask(par), sq, 0.0), axis=-1, keepdims=True) * (1.0 / DA_V)
        rs = lax.rsqrt(ms + EPS)
        r = rs if r is None else jnp.where(_half_mask(0), r, rs)
    return o * r * subg * one_minus_lam_init


def _both_halves(x, g):
    swapped = pltpu.roll(x, 64, 1)
    return jnp.where(_half_mask(g), x, swapped)


def _swa_attention(qf, load_k, load_v, sinks, shift, valid=None):
    heads_per_group = SW_HEADS // SW_KV_HEADS
    outs = []
    for c in range(SW_HEADS // 2):
        qc = _chunk(qf, c)
        g = (2 * c) // heads_per_group
        o_par = []
        for par in range(2):
            h = 2 * c + par
            qm = jnp.where(_half_mask(par), qc, 0.0).astype(BF16)
            s = _scores(qm, load_k(g))
            if valid is not None:
                s = jnp.where(valid, s, -1e30)
            e, m = _exp_scores(s, shift, sink=sinks[h])
            o = jnp.dot(e.astype(BF16), load_v(g), preferred_element_type=F32)
            d = _chunk(o, 1) + jnp.exp2(sinks[h] - m)
            o_par.append(_chunk(o, 0) * (1.0 / d))
        outs.append(jnp.where(_half_mask(0), o_par[0], o_par[1]))
    return outs


def _mla_attention(load_q, load_k, load_v, shift):
    chunks = []
    for c in range(MLA_HEADS // 2):
        o_par = []
        for par in range(2):
            h = 2 * c + par
            e, _ = _exp_scores(_scores(load_q(h), load_k(h)), shift)
            o = jnp.dot(e.astype(BF16), load_v(h), preferred_element_type=F32)
            o_par.append(o * (1.0 / pltpu.roll(o, 64, 1)))
        chunks.append(jnp.where(_half_mask(0), o_par[0], o_par[1]))
    return chunks


def _ones_in_other_half(v, par):
    return jnp.where(_half_mask(par), v, 1.0)


def _mod_kernel(c_ref, w_ref, b_ref, o_ref):
    c = c_ref[...]
    s = c * (1.0 / (1.0 + jnp.exp(-c)))
    o_ref[0] = jnp.dot(s.astype(BF16), w_ref[0].astype(BF16), preferred_element_type=F32) + b_ref[0]


def _modulation(cvecs, w_ada, b_ada):
    n_out = 6 * D_MODEL
    out = pl.pallas_call(
        _mod_kernel,
        grid=(DEPTH, n_out // MOD_TILE),
        in_specs=[
            pl.BlockSpec((8, D_MODEL), lambda l, j: (0, 0)),
            pl.BlockSpec((1, D_MODEL, MOD_TILE), lambda l, j: (l, 0, j)),
            pl.BlockSpec((1, 1, MOD_TILE), lambda l, j: (l, 0, j)),
        ],
        out_specs=pl.BlockSpec((1, 8, MOD_TILE), lambda l, j: (l, 0, j)),
        out_shape=jax.ShapeDtypeStruct((DEPTH, 8, n_out), F32),
        compiler_params=pltpu.CompilerParams(
            dimension_semantics=("arbitrary", "arbitrary"), vmem_limit_bytes=VMEM_LIMIT),
        name="adaln_modulation",
    )(cvecs, w_ada, b_ada.reshape(DEPTH, 1, n_out))
    return out.reshape(DEPTH, 8, 6, D_MODEL)


def _ctx_kernel(l_ref, x_ref, mod_ref, n1g_ref, w_in_ref, qg_ref, wqb_ref, kvg_ref, wkvb_ref,
                lamv_ref, subg_ref, scal_ref, *rest, first_layer):
    dk_ref, dv_ref, sk_ref, sv_ref, ckv_ref, kr_ref, mix_ref = rest[-7:]
    l = l_ref[0]
    if first_layer:
        for ref in (dk_ref, dv_ref, sk_ref, sv_ref, ckv_ref, kr_ref):
            ref[:, 1:] = jnp.zeros((CTX_REQ_PER_STEP, DEPTH - 1) + ref.shape[2:], F32)
    proj = _project(x_ref, mod_ref, n1g_ref, w_in_ref)
    ckv_all = _rms(proj[:, C_CKV:C_KR], kvg_ref[0])
    mq_all = jnp.dot(_rms(proj[:, C_QA:C_CKV], qg_ref[0]).astype(BF16), wqb_ref[0], preferred_element_type=F32)
    mq_all = mq_all * (MLA_SCALE * LOG2E)
    kv_all = jnp.dot(ckv_all.astype(BF16), wkvb_ref[0], preferred_element_type=F32)
    lam = _diff_lambda(lamv_ref, scal_ref[l, S_LAM_INIT])
    subg = subg_ref[0]
    sinks = [scal_ref[l, h] for h in range(SW_HEADS)]
    shift = True

    for r in range(CTX_REQ_PER_STEP):
        r0, r1 = r * SEQ, (r + 1) * SEQ
        pr, mq, kv = proj[r0:r1], mq_all[r0:r1], kv_all[r0:r1]
        da_k = pr[:, C_DAK:C_DAV]
        da_v = pr[:, C_DAV:C_SWQ]
        sw_k = pr[:, C_SWK:C_SWV]
        sw_v = pr[:, C_SWV:C_QA]
        krz = pr[:, C_KR:P_EXT]
        dk_ref[r, 0] = da_k
        dv_ref[r, 0] = da_v
        sk_ref[r, 0] = sw_k
        sv_ref[r, 0] = sw_v
        ckv_ref[r, 0] = ckv_all[r0:r1]
        kr_ref[r, 0] = krz[:, 0:MLA_ROPE]

        da_kb = da_k.astype(BF16)
        da_o = _diff_attention(pr[:, C_DAQ:C_DAK] * (DA_SCALE * LOG2E), lambda: da_kb,
                               lambda c: _chunk(da_v, c).astype(BF16), lam, shift)
        for c, o in enumerate(da_o):
            mix_ref[r0:r1, c * LANES:(c + 1) * LANES] = _diff_subln(
                o, _chunk(subg, c), scal_ref[l, S_ONE_MINUS]).astype(BF16)
        sw_kg = [_both_halves(sw_k, g).astype(BF16) for g in range(SW_KV_HEADS)]
        sw_vg = [jnp.concatenate([_both_halves(sw_v, g), jnp.ones_like(sw_v)], axis=-1).astype(BF16)
                 for g in range(SW_KV_HEADS)]
        sw_o = _swa_attention(pr[:, C_SWQ:C_SWK] * (SW_SCALE * LOG2E), lambda g: sw_kg[g], lambda g: sw_vg[g],
                              sinks, shift)
        for c, o in enumerate(sw_o):
            mix_ref[r0:r1, 256 + c * LANES:256 + (c + 1) * LANES] = o.astype(BF16)
        mla_o = _mla_attention(
            lambda h: _chunk(mq, h).astype(BF16), lambda h: (_chunk(kv, h) + krz).astype(BF16),
            lambda h: _ones_in_other_half(_chunk(kv, MLA_HEADS + h // 2), h % 2).astype(BF16), shift)
        for c, o in enumerate(mla_o):
            mix_ref[r0:r1, 768 + c * LANES:768 + (c + 1) * LANES] = o.astype(BF16)


def _layer_spec(shape):
    nd = len(shape)
    return pl.BlockSpec((1,) + tuple(shape[1:]), lambda *a: (a[-1][0],) + (0,) * (nd - 1))


CACHE_WIDTHS = (DA_HEADS * 2 * DA_QK, DA_HEADS * DA_V, SW_KV_HEADS * SW_DIM, SW_KV_HEADS * SW_DIM,
                MLA_KV_RANK, MLA_ROPE)


def _ctx_layer(l_idx, x, mod, p, new_caches):
    n_req = BATCH // CTX_REQ_PER_STEP
    first_layer = new_caches is None
    if first_layer:
        new_caches = []
        out_shape = [jax.ShapeDtypeStruct((BATCH, DEPTH, SEQ, w), F32) for w in CACHE_WIDTHS]
        out_specs = [pl.BlockSpec((CTX_REQ_PER_STEP, DEPTH, SEQ, w), lambda i, l: (i, 0, 0, 0)) for w in CACHE_WIDTHS]
    else:
        out_shape = [jax.ShapeDtypeStruct(c.shape, c.dtype) for c in new_caches]
        out_specs = [pl.BlockSpec((CTX_REQ_PER_STEP, 1, SEQ, w), lambda i, l: (i, l[0], 0, 0)) for w in CACHE_WIDTHS]
    out_shape.append(jax.ShapeDtypeStruct((N_CTX_TOK, D_MODEL), BF16))
    out_specs.append(pl.BlockSpec((TM_CTX, D_MODEL), lambda i, l: (i, 0)))
    n_fixed_inputs = 12
    aliases = {n_fixed_inputs + j: j for j in range(len(new_caches))}
    grid_spec = pltpu.PrefetchScalarGridSpec(
        num_scalar_prefetch=1,
        grid=(n_req,),
        in_specs=[
            pl.BlockSpec((TM_CTX, D_MODEL), lambda i, l: (i, 0)),
            pl.BlockSpec((1, 1, 6, D_MODEL), lambda i, l: (l[0], CTX_MOD_ROW, 0, 0)),
            _layer_spec(p["norm1_g"].shape),
            _layer_spec(p["w_in"].shape),
            _layer_spec(p["mla_q_norm_g"].shape),
            _layer_spec(p["mla_w_qb"].shape),
            _layer_spec(p["mla_kv_norm_g"].shape),
            _layer_spec(p["mla_w_kvb"].shape),
            _layer_spec(p["lamv"].shape),
            _layer_spec(p["subg"].shape),
            pl.BlockSpec(memory_space=pltpu.SMEM),
        ] + [pl.BlockSpec(memory_space=pl.ANY)] * len(new_caches),
        out_specs=out_specs,
    )
    return pl.pallas_call(
        functools.partial(_ctx_kernel, first_layer=first_layer),
        grid_spec=grid_spec,
        out_shape=out_shape,
        input_output_aliases=aliases,
        compiler_params=pltpu.CompilerParams(
            dimension_semantics=("arbitrary",), vmem_limit_bytes=VMEM_LIMIT),
        name="ctx_project_attend",
    )(l_idx, x, mod, p["norm1_g"], p["w_in"], p["mla_q_norm_g"], p["mla_w_qb"], p["mla_kv_norm_g"],
      p["mla_w_kvb"], p["lamv"], p["subg"], p["scal"], *new_caches)


def _rope(x, cos, sa, sb, half):
    return x * cos + pltpu.roll(x, LANES - half, 1) * sa + pltpu.roll(x, half, 1) * sb


def _lat_proj_kernel(l_ref, x_ref, mod_ref, n1g_ref, w_in_ref, qg_ref, wqb_ref, kvg_ref, wkvb_ref,
                     c32_ref, a32_ref, b32_ref, c64_ref, a64_ref, b64_ref, cm_ref, am_ref, bm_ref,
                     q_ref, k_ref, v_ref):
    proj = _project(x_ref, mod_ref, n1g_ref, w_in_ref)
    t32 = (c32_ref[...], a32_ref[...], b32_ref[...], DA_QK // 4)
    t64 = (c64_ref[...], a64_ref[...], b64_ref[...], SW_DIM // 4)
    tml = (cm_ref[...], am_ref[...], bm_ref[...], MLA_ROPE // 4)

    for j in range(2):
        q = _rope(_chunk(proj, j), *t32) * (DA_SCALE * LOG2E)
        q_ref[:, j * LANES:(j + 1) * LANES] = q.astype(BF16)
    for j in range(4):
        q = _rope(_chunk(proj, C_SWQ // LANES + j), *t64) * (SW_SCALE * LOG2E)
        q_ref[:, 256 + j * LANES:256 + (j + 1) * LANES] = q.astype(BF16)
    mq = jnp.dot(_rms(proj[:, C_QA:C_CKV], qg_ref[0]).astype(BF16), wqb_ref[0], preferred_element_type=F32)
    for h in range(MLA_HEADS):
        q = _rope(_chunk(mq, h), *tml) * (MLA_SCALE * LOG2E)
        q_ref[:, O_MQ + h * LANES:O_MQ + (h + 1) * LANES] = q.astype(BF16)

    for j in range(2):
        k_ref[:, j * LANES:(j + 1) * LANES] = _rope(_chunk(proj, C_DAK // LANES + j), *t32).astype(BF16)
    k_ref[:, 256:384] = _rope(proj[:, C_SWK:C_SWV], *t64).astype(BF16)
    ckv = _rms(proj[:, C_CKV:C_KR], kvg_ref[0])
    kv = jnp.dot(ckv.astype(BF16), wkvb_ref[0], preferred_element_type=F32)
    krz = _rope(proj[:, C_KR:P_EXT], *tml)
    for h in range(MLA_HEADS):
        k_ref[:, O_MK + h * LANES:O_MK + (h + 1) * LANES] = (_chunk(kv, h) + krz).astype(BF16)

    v_ref[:, 0:256] = proj[:, C_DAV:C_SWQ].astype(BF16)
    v_ref[:, 256:384] = proj[:, C_SWV:C_QA].astype(BF16)
    v_ref[:, 384:640] = kv[:, MLA_CHUNKS:].astype(BF16)


def _lat_project(l_idx, x, mod, p, tables):
    tiles_per_req = DEC_SEQ // TM_LAT
    tab_spec = pl.BlockSpec((TM_LAT, LANES), lambda i, l: (i % tiles_per_req, 0))
    grid_spec = pltpu.PrefetchScalarGridSpec(
        num_scalar_prefetch=1,
        grid=(N_LAT_TOK // TM_LAT,),
        in_specs=[
            pl.BlockSpec((TM_LAT, D_MODEL), lambda i, l: (i, 0)),
            pl.BlockSpec((1, 1, 6, D_MODEL), lambda i, l: (l[0], i // tiles_per_req, 0, 0)),
            _layer_spec(p["norm1_g"].shape),
            _layer_spec(p["w_in"].shape),
            _layer_spec(p["mla_q_norm_g"].shape),
            _layer_spec(p["mla_w_qb"].shape),
            _layer_spec(p["mla_kv_norm_g"].shape),
            _layer_spec(p["mla_w_kvb"].shape),
        ] + [tab_spec] * len(tables),
        out_specs=[
            pl.BlockSpec((TM_LAT, Q_W), lambda i, l: (i, 0)),
            pl.BlockSpec((TM_LAT, K_W), lambda i, l: (i, 0)),
            pl.BlockSpec((TM_LAT, V_W), lambda i, l: (i, 0)),
        ],
    )
    return pl.pallas_call(
        _lat_proj_kernel,
        grid_spec=grid_spec,
        out_shape=[jax.ShapeDtypeStruct((N_LAT_TOK, Q_W), BF16),
                   jax.ShapeDtypeStruct((N_LAT_TOK, K_W), BF16),
                   jax.ShapeDtypeStruct((N_LAT_TOK, V_W), BF16)],
        compiler_params=pltpu.CompilerParams(
            dimension_semantics=("arbitrary",), vmem_limit_bytes=VMEM_LIMIT),
        name="latent_project",
    )(l_idx, x, mod, p["norm1_g"], p["w_in"], p["mla_q_norm_g"], p["mla_w_qb"], p["mla_kv_norm_g"],
      p["mla_w_kvb"], *tables)


def _lat_attn_kernel(l_ref, q_ref, k_ref, v_ref, cdk_ref, cdv_ref, csk_ref, csv_ref, cckv_ref, ckr_ref,
                     wkvb_ref, pad_ref, lamv_ref, subg_ref, gq_ref, gk_ref, scal_ref,
                     mix_ref,
                     kda, vda, ksw, vsw, kml, vml, kmax_sq):
    l = l_ref[0]
    qb = pl.program_id(1)

    @pl.when(qb == 0)
    def _stage_keys():
        kda[0:PAST_LEN, :] = cdk_ref[0, 0].astype(BF16)
        kda[PAST_LEN:N_KEYS, :] = k_ref[0, :, 0:256]
        vda[0:PAST_LEN, :] = cdv_ref[0, 0].astype(BF16)
        vda[PAST_LEN:N_KEYS, :] = v_ref[0, :, 0:256]
        sk_lat = k_ref[0, :, 256:384].astype(F32)
        sv_lat = v_ref[0, :, 256:384].astype(F32)
        for g in range(SW_KV_HEADS):
            ksw[0:PAST_LEN, g * LANES:(g + 1) * LANES] = _both_halves(csk_ref[0, 0], g).astype(BF16)
            ksw[PAST_LEN:N_KEYS, g * LANES:(g + 1) * LANES] = _both_halves(sk_lat, g).astype(BF16)
            vsw[0:PAST_LEN, 2 * g * LANES:(2 * g + 1) * LANES] = _both_halves(csv_ref[0, 0], g).astype(BF16)
            vsw[PAST_LEN:N_KEYS, 2 * g * LANES:(2 * g + 1) * LANES] = _both_halves(sv_lat, g).astype(BF16)
            vsw[:, (2 * g + 1) * LANES:(2 * g + 2) * LANES] = jnp.ones((N_KEYS, LANES), BF16)
        kvc = jnp.dot(cckv_ref[0, 0].astype(BF16), wkvb_ref[0], preferred_element_type=F32)
        krz = jnp.dot(ckr_ref[0, 0].astype(BF16), pad_ref[...], preferred_element_type=F32)
        mk_ctx = [_chunk(kvc, h) + krz for h in range(MLA_HEADS)]
        for h in range(MLA_HEADS):
            kml[0:PAST_LEN, h * LANES:(h + 1) * LANES] = mk_ctx[h].astype(BF16)
            v_ctx = _chunk(kvc, MLA_HEADS + h // 2)
            v_lat = v_ref[0, :, 384 + (h // 2) * LANES:384 + (h // 2 + 1) * LANES].astype(F32)
            vml[0:PAST_LEN, h * LANES:(h + 1) * LANES] = _ones_in_other_half(v_ctx, h % 2).astype(BF16)
            vml[PAST_LEN:N_KEYS, h * LANES:(h + 1) * LANES] = _ones_in_other_half(v_lat, h % 2).astype(BF16)
        kml[PAST_LEN:N_KEYS, :] = k_ref[0, :, O_MK:K_W]
        gk = gk_ref[...]
        k_ctx = jnp.concatenate([cdk_ref[0, 0], csk_ref[0, 0]] + mk_ctx, axis=-1)
        kn_ctx = jnp.dot(_sq_bf16(k_ctx), gk, preferred_element_type=F32)
        kn_lat = jnp.dot(_sq_bf16(k_ref[0]), gk, preferred_element_type=F32)
        kmax_sq[...] = jnp.maximum(jnp.max(kn_ctx, axis=0, keepdims=True), jnp.max(kn_lat, axis=0, keepdims=True))

    bound_sq = _logit_bound_sq(q_ref[0], gq_ref[...], kmax_sq[...])
    small = (jnp.max(bound_sq) <= NOSHIFT_BOUND ** 2) & (scal_ref[l, S_SINK_ABSMAX] <= NOSHIFT_BOUND)

    lam = _diff_lambda(lamv_ref, scal_ref[l, S_LAM_INIT])
    subg = subg_ref[0]
    sinks = [scal_ref[l, h] for h in range(SW_HEADS)]
    q0 = qb * TQ
    start = pl.multiple_of(jnp.clip(q0 - WINDOW, 0, DEC_SEQ - SW_BAND), LANES)

    def attend(shift):
        da_o = _diff_attention(q_ref[0, :, 0:256].astype(F32), lambda: kda[...],
                               lambda c: vda[:, c * LANES:(c + 1) * LANES], lam, shift)
        for c, o in enumerate(da_o):
            mix_ref[0, :, c * LANES:(c + 1) * LANES] = _diff_subln(
                o, _chunk(subg, c), scal_ref[l, S_ONE_MINUS]).astype(BF16)

        band = pl.ds(PAST_LEN + start, SW_BAND)
        k_sel = jnp.concatenate([ksw[0:PAST_LEN, :], ksw[band, :]], axis=0)
        v_sel = jnp.concatenate([vsw[0:PAST_LEN, :], vsw[band, :]], axis=0)
        n_sel = PAST_LEN + SW_BAND
        col = lax.broadcasted_iota(jnp.int32, (TQ, n_sel), 1)
        row = lax.broadcasted_iota(jnp.int32, (TQ, n_sel), 0)
        dist = (q0 + row) - (start + col - PAST_LEN)
        valid = (col < PAST_LEN) | ((dist <= WINDOW) & (dist >= -WINDOW))
        sw_o = _swa_attention(q_ref[0, :, 256:768].astype(F32), lambda g: _chunk(k_sel, g),
                              lambda g: v_sel[:, 2 * g * LANES:(2 * g + 2) * LANES], sinks, shift, valid)
        for c, o in enumerate(sw_o):
            mix_ref[0, :, 256 + c * LANES:256 + (c + 1) * LANES] = o.astype(BF16)

        mla_o = _mla_attention(lambda h: q_ref[0, :, O_MQ + h * LANES:O_MQ + (h + 1) * LANES],
                               lambda h: kml[:, h * LANES:(h + 1) * LANES],
                               lambda h: vml[:, h * LANES:(h + 1) * LANES], shift)
        for c, o in enumerate(mla_o):
            mix_ref[0, :, 768 + c * LANES:768 + (c + 1) * LANES] = o.astype(BF16)

    pl.when(small)(functools.partial(attend, False))
    pl.when(jnp.logical_not(small))(functools.partial(attend, True))


def _lat_attend(l_idx, q, k, v, caches, p, pad_eye):
    cdk, cdv, csk, csv, cckv, ckr = caches

    def cache_spec(a):
        return pl.BlockSpec((1, 1) + a.shape[2:], lambda b, j, l: (b, l[0], 0, 0))

    grid_spec = pltpu.PrefetchScalarGridSpec(
        num_scalar_prefetch=1,
        grid=(DEC_BATCH, DEC_SEQ // TQ),
        in_specs=[
            pl.BlockSpec((1, TQ, Q_W), lambda b, j, l: (b, j, 0)),
            pl.BlockSpec((1, DEC_SEQ, K_W), lambda b, j, l: (b, 0, 0)),
            pl.BlockSpec((1, DEC_SEQ, V_W), lambda b, j, l: (b, 0, 0)),
            cache_spec(cdk), cache_spec(cdv), cache_spec(csk), cache_spec(csv), cache_spec(cckv), cache_spec(ckr),
            _layer_spec(p["mla_w_kvb"].shape),
            pl.BlockSpec(pad_eye.shape, lambda b, j, l: (0, 0)),
            _layer_spec(p["lamv"].shape),
            _layer_spec(p["subg"].shape),
            pl.BlockSpec(p["gq"].shape, lambda b, j, l: (0, 0)),
            pl.BlockSpec(p["gk"].shape, lambda b, j, l: (0, 0)),
            pl.BlockSpec(memory_space=pltpu.SMEM),
        ],
        out_specs=pl.BlockSpec((1, TQ, D_MODEL), lambda b, j, l: (b, j, 0)),
        scratch_shapes=[
            pltpu.VMEM((N_KEYS, 256), BF16), pltpu.VMEM((N_KEYS, 256), BF16),
            pltpu.VMEM((N_KEYS, SW_KV_HEADS * LANES), BF16), pltpu.VMEM((N_KEYS, SW_KV_HEADS * 2 * LANES), BF16),
            pltpu.VMEM((N_KEYS, MLA_CHUNKS), BF16), pltpu.VMEM((N_KEYS, MLA_CHUNKS), BF16),
            pltpu.VMEM((1, LANES), F32),
        ],
    )
    return pl.pallas_call(
        _lat_attn_kernel,
        grid_spec=grid_spec,
        out_shape=jax.ShapeDtypeStruct((DEC_BATCH, DEC_SEQ, D_MODEL), BF16),
        compiler_params=pltpu.CompilerParams(
            dimension_semantics=("arbitrary", "arbitrary"), vmem_limit_bytes=VMEM_LIMIT),
        name="latent_attend",
    )(l_idx, q.reshape(DEC_BATCH, DEC_SEQ, Q_W), k.reshape(DEC_BATCH, DEC_SEQ, K_W),
      v.reshape(DEC_BATCH, DEC_SEQ, V_W), cdk, cdv, csk, csv, cckv, ckr,
      p["mla_w_kvb"], pad_eye, p["lamv"], p["subg"], p["gq"], p["gk"], p["scal"])


def _mlp_kernel(l_ref, x_ref, mix_ref, mod_ref, n2g_ref, w_out_ref, w_up_ref, w_down_ref, fg_ref, o_ref,
                *, final_norm):
    m = mod_ref[0, 0]
    x = x_ref[...] + m[2:3] * jnp.dot(mix_ref[...], w_out_ref[0], preferred_element_type=F32)
    h = (_rms(x, n2g_ref[0]) * (1.0 + m[4:5]) + m[3:4]).astype(BF16)
    acc = jnp.zeros_like(x)
    for c in range(D_FF // FF_CHUNK):
        u = jnp.dot(h, w_up_ref[0, :, c * FF_CHUNK:(c + 1) * FF_CHUNK], preferred_element_type=F32)
        a = jnp.square(jnp.maximum(u, 0.0)).astype(BF16)
        acc = acc + jnp.dot(a, w_down_ref[0, c * FF_CHUNK:(c + 1) * FF_CHUNK, :], preferred_element_type=F32)
    x = x + m[5:6] * acc
    if final_norm:
        x = _rms(x, fg_ref[...])
    o_ref[...] = x


def _mlp_layer(l_idx, x, mix, mod, p, final_g, mod_row, final_norm):
    n_tok = x.shape[0]
    grid_spec = pltpu.PrefetchScalarGridSpec(
        num_scalar_prefetch=1,
        grid=(n_tok // TM_MLP,),
        in_specs=[
            pl.BlockSpec((TM_MLP, D_MODEL), lambda i, l: (i, 0)),
            pl.BlockSpec((TM_MLP, D_MODEL), lambda i, l: (i, 0)),
            pl.BlockSpec((1, 1, 6, D_MODEL), lambda i, l: (l[0], mod_row(i), 0, 0)),
            _layer_spec(p["norm2_g"].shape),
            _layer_spec(p["w_out"].shape),
            _layer_spec(p["w_up"].shape),
            _layer_spec(p["w_down"].shape),
            pl.BlockSpec((1, D_MODEL), lambda i, l: (0, 0)),
        ],
        out_specs=pl.BlockSpec((TM_MLP, D_MODEL), lambda i, l: (i, 0)),
    )
    return pl.pallas_call(
        functools.partial(_mlp_kernel, final_norm=final_norm),
        grid_spec=grid_spec,
        out_shape=jax.ShapeDtypeStruct((n_tok, D_MODEL), F32),
        compiler_params=pltpu.CompilerParams(
            dimension_semantics=("arbitrary",), vmem_limit_bytes=VMEM_LIMIT),
        name="out_proj_mlp",
    )(l_idx, x, mix, mod, p["norm2_g"], p["w_out"], p["w_up"], p["w_down"], final_g)


def _rope_tables(period, active_lanes=LANES):
    m = period // 2
    half = m // 2
    t = np.arange(DEC_SEQ)
    pos = np.stack([t // GRID_W, t % GRID_W], axis=0).astype(np.float64)
    lane = np.arange(LANES) % period
    axis = lane // m
    k = lane % m
    freq = ROPE_BASE ** (-(k % half).astype(np.float64) / half)
    ang = pos[axis, :].T * freq[None, :]
    first = (k < half)[None, :]
    active = (np.arange(LANES) < active_lanes)[None, :]
    cos, sin = np.where(active, np.cos(ang), 1.0), np.where(active, np.sin(ang), 0.0)
    return (jnp.asarray(cos, F32), jnp.asarray(np.where(first, -sin, 0.0), F32),
            jnp.asarray(np.where(first, 0.0, sin), F32))


def _group_matrices():
    gq = np.zeros((Q_W, LANES), np.float32)
    gk = np.zeros((K_W, LANES), np.float32)
    for d in range(DA_HEADS * 2 * DA_QK):
        gq[d, G_DIFF + d // DA_QK] = 1.0
        gk[d, G_DIFF + d // DA_QK] = 1.0
    for h in range(SW_HEADS):
        g = h // (SW_HEADS // SW_KV_HEADS)
        gq[256 + h * SW_DIM:256 + (h + 1) * SW_DIM, G_SWA + h] = 1.0
        gk[256 + g * SW_DIM:256 + (g + 1) * SW_DIM, G_SWA + h] = 1.0
    for h in range(MLA_HEADS):
        gq[O_MQ + h * LANES:O_MQ + (h + 1) * LANES, G_MLA + h] = 1.0
        gk[O_MK + h * LANES:O_MK + (h + 1) * LANES, G_MLA + h] = 1.0
    return jnp.asarray(gq, BF16), jnp.asarray(gk, BF16)


def _prepare_params(norm1_g, norm2_g, w_in, w_out, lq1, lk1, lq2, lk2, diff_subln_g, swa_sink,
                    mla_q_norm_g, mla_w_qb, mla_kv_norm_g, mla_w_kvb, w_up, w_down):
    w_in_ext = jnp.pad(w_in.astype(BF16), ((0, 0), (0, 0), (0, P_EXT - w_in.shape[-1])))
    w_out_p = w_out.astype(BF16)
    dq = MLA_NOPE + MLA_ROPE
    gap = LANES - dq
    zq = jnp.zeros((DEPTH, MLA_Q_RANK, gap), mla_w_qb.dtype)
    wqb = jnp.concatenate(
        [blk for h in range(MLA_HEADS)
         for blk in (mla_w_qb[:, :, h * dq + MLA_NOPE:(h + 1) * dq], zq, mla_w_qb[:, :, h * dq:h * dq + MLA_NOPE])],
        axis=-1).astype(BF16)
    dkv = MLA_NOPE + MLA_V
    zk = jnp.zeros((DEPTH, MLA_KV_RANK, LANES - MLA_NOPE), mla_w_kvb.dtype)
    wkvb = jnp.concatenate(
        [blk for h in range(MLA_HEADS) for blk in (zk, mla_w_kvb[:, :, h * dkv:h * dkv + MLA_NOPE])]
        + [mla_w_kvb[:, :, h * dkv + MLA_NOPE:(h + 1) * dkv] for h in range(MLA_HEADS)],
        axis=-1).astype(BF16)
    lam_init = np.array([0.8 - 0.6 * math.exp(-0.3 * l) for l in range(DEPTH)], np.float32)
    sink_l2 = swa_sink.astype(F32) * LOG2E
    scal = jnp.concatenate([
        sink_l2,
        jnp.asarray(np.stack([lam_init, (1.0 - lam_init.astype(np.float64)).astype(np.float32)], axis=1)),
        jnp.max(jnp.abs(sink_l2), axis=1, keepdims=True),
        jnp.zeros((DEPTH, S_COLS - SW_HEADS - 3), F32)], axis=1)
    gq, gk = _group_matrices()
    return {
        "gq": gq,
        "gk": gk,
        "norm1_g": norm1_g.reshape(DEPTH, 1, D_MODEL),
        "norm2_g": norm2_g.reshape(DEPTH, 1, D_MODEL),
        "w_in": w_in_ext,
        "w_out": w_out_p,
        "mla_q_norm_g": mla_q_norm_g.reshape(DEPTH, 1, MLA_Q_RANK),
        "mla_w_qb": wqb,
        "mla_kv_norm_g": mla_kv_norm_g.reshape(DEPTH, 1, MLA_KV_RANK),
        "mla_w_kvb": wkvb,
        "lamv": jnp.stack([lq1, lk1, lq2, lk2], axis=1),
        "subg": jnp.tile(diff_subln_g, (1, DA_HEADS)).reshape(DEPTH, 1, DA_HEADS * DA_V),
        "scal": scal,
        "w_up": w_up.astype(BF16),
        "w_down": w_down.astype(BF16),
    }


def kernel(x_prompt, x_sample, c, cache_diff_k, cache_diff_v, cache_swa_k, cache_swa_v, cache_mla_ckv,
           cache_mla_krope, c_ctx, norm1_g, norm2_g, w_ada, b_ada, w_in, w_out, diff_lambda_q1,
           diff_lambda_k1, diff_lambda_q2, diff_lambda_k2, diff_subln_g, swa_sink, mla_q_norm_g,
           mla_w_qb, mla_kv_norm_g, mla_w_kvb, w_up, w_down, final_g):
    p = _prepare_params(norm1_g, norm2_g, w_in, w_out, diff_lambda_q1, diff_lambda_k1, diff_lambda_q2,
                        diff_lambda_k2, diff_subln_g, swa_sink, mla_q_norm_g, mla_w_qb, mla_kv_norm_g,
                        mla_w_kvb, w_up, w_down)
    cvecs = jnp.concatenate([c, c_ctx[None, :], jnp.zeros((8 - DEC_BATCH - 1, D_MODEL), F32)], axis=0)
    mod = _modulation(cvecs, w_ada, b_ada)
    tables = _rope_tables(DA_QK) + _rope_tables(SW_DIM) + _rope_tables(MLA_ROPE, MLA_ROPE)
    pad_eye = jnp.asarray(np.eye(MLA_ROPE, LANES, dtype=np.float32), BF16)
    final_g2 = final_g.reshape(1, D_MODEL)
    caches = (cache_diff_k.reshape(DEC_BATCH, DEPTH, PAST_LEN, DA_HEADS * 2 * DA_QK),
              cache_diff_v.reshape(DEC_BATCH, DEPTH, PAST_LEN, DA_HEADS * DA_V),
              cache_swa_k.reshape(DEC_BATCH, DEPTH, PAST_LEN, SW_KV_HEADS * SW_DIM),
              cache_swa_v.reshape(DEC_BATCH, DEPTH, PAST_LEN, SW_KV_HEADS * SW_DIM),
              cache_mla_ckv, cache_mla_krope)

    x_ctx = x_prompt.reshape(N_CTX_TOK, D_MODEL)
    x_lat = x_sample.reshape(N_LAT_TOK, D_MODEL)
    new = None
    lat_tiles_per_req = DEC_SEQ // TM_MLP
    for l in range(DEPTH):
        l_idx = jnp.full((1,), l, jnp.int32)
        last = l == DEPTH - 1
        *new, mix_ctx = _ctx_layer(l_idx, x_ctx, mod, p, new)
        x_ctx = _mlp_layer(l_idx, x_ctx, mix_ctx, mod, p, final_g2, lambda i: CTX_MOD_ROW, last)
        q, k, v = _lat_project(l_idx, x_lat, mod, p, tables)
        mix_lat = _lat_attend(l_idx, q, k, v, caches, p, pad_eye).reshape(N_LAT_TOK, D_MODEL)
        x_lat = _mlp_layer(l_idx, x_lat, mix_lat, mod, p, final_g2, lambda i: i // lat_tiles_per_req, last)

    y_prompt = x_ctx.reshape(BATCH, SEQ, D_MODEL)
    y_sample = x_lat.reshape(DEC_BATCH, DEC_SEQ, D_MODEL)
    new_dk, new_dv, new_sk, new_sv, new_ckv, new_kr = new
    return (y_prompt, y_sample,
            new_dk.reshape(BATCH, DEPTH, SEQ, DA_HEADS, 2 * DA_QK),
            new_dv.reshape(BATCH, DEPTH, SEQ, DA_HEADS, DA_V),
            new_sk.reshape(BATCH, DEPTH, SEQ, SW_KV_HEADS, SW_DIM),
            new_sv.reshape(BATCH, DEPTH, SEQ, SW_KV_HEADS, SW_DIM),
            new_ckv, new_kr)
```

```python
import functools
import math

import numpy as np
import jax
import jax.numpy as jnp
from jax import lax
from jax.experimental import pallas as pl
from jax.experimental.pallas import tpu as pltpu

F32 = jnp.float32
BF16 = jnp.bfloat16

D_MODEL = 1024
BATCH = 32
SEQ = 256
DEPTH = 4
DEC_BATCH = 4
DEC_SEQ = 2048
PAST_LEN = 256
GRID_W = 64
WINDOW = 128
ROPE_BASE = 10000.0
EPS = 1e-6
DA_HEADS = 4
DA_QK = 32
DA_V = 64
SW_HEADS = 8
SW_KV_HEADS = 2
SW_DIM = 64
MLA_HEADS = 4
MLA_Q_RANK = 256
MLA_KV_RANK = 128
MLA_NOPE = 64
MLA_ROPE = 32
MLA_V = 64
D_FF = 4 * D_MODEL

LANES = 128
N_CTX_TOK = BATCH * SEQ
N_LAT_TOK = DEC_BATCH * DEC_SEQ
N_KEYS = PAST_LEN + DEC_SEQ
CTX_MOD_ROW = DEC_BATCH

C_DAQ, C_DAK, C_DAV, C_SWQ, C_SWK, C_SWV, C_QA, C_CKV, C_KR, P_EXT = (
    0, 256, 512, 768, 1280, 1408, 1536, 1792, 1920, 2048)
MLA_CHUNKS = MLA_HEADS * 128
Q_W = 256 + 512 + MLA_CHUNKS
K_W = 256 + 128 + MLA_CHUNKS
V_W = 256 + 128 + 256
O_MQ = 768
O_MK = 384
LOG2E = 1.4426950408889634

DA_SCALE = DA_QK ** -0.5
SW_SCALE = SW_DIM ** -0.5
MLA_SCALE = (MLA_NOPE + MLA_ROPE) ** -0.5

CTX_REQ_PER_STEP = 1
TM_CTX = CTX_REQ_PER_STEP * SEQ
TM_LAT = 512
TM_MLP = 512
TQ = 256
SW_BAND = TQ + 2 * WINDOW
FF_CHUNK = 1024
MOD_TILE = 1536
VMEM_LIMIT = 56 * 1024 * 1024

_NT = (((1,), (1,)), ((), ()))


def _scores(q, k):
    return lax.dot_general(q, k, _NT, preferred_element_type=F32)

S_LAM_INIT = SW_HEADS
S_ONE_MINUS = SW_HEADS + 1
S_SINK_ABSMAX = SW_HEADS + 2
S_COLS = 16

NOSHIFT_BOUND = 50.0
G_DIFF, G_SWA, G_MLA = 0, 8, 16


def _rms(x, g):
    ms = jnp.mean(x * x, axis=-1, keepdims=True)
    return x * lax.rsqrt(ms + EPS) * g


def _lane_mask(n, lo, width):
    lane = lax.broadcasted_iota(jnp.int32, (1, n), 1)
    return (lane >= lo) & (lane < lo + width)


def _exp_scores(s, shift, sink=None):
    if not shift:
        return jnp.exp2(s), jnp.zeros((1, 1), F32)
    m = jnp.max(s, axis=-1, keepdims=True)
    if sink is not None:
        m = jnp.maximum(m, sink)
    return jnp.exp2(s - m), m


def _sq_bf16(x):
    xf = x.astype(F32)
    return (xf * xf).astype(BF16)


def _max_group_sq_norm(x, g):
    return jnp.max(jnp.dot(_sq_bf16(x), g, preferred_element_type=F32), axis=0, keepdims=True)


BOUND_MARGIN = 1.03


def _chunk(a, j):
    return a[:, j * LANES:(j + 1) * LANES]


def _project(x_ref, mod_ref, n1g_ref, w_in_ref):
    m = mod_ref[0, 0]
    h = _rms(x_ref[...], n1g_ref[0]) * (1.0 + m[1:2]) + m[0:1]
    return jnp.dot(h.astype(BF16), w_in_ref[0], preferred_element_type=F32)


def _diff_lambda(lamv_ref, lam_init):
    lv = lamv_ref[0]
    a = jnp.sum(lv[0:1] * lv[1:2], axis=-1, keepdims=True)
    b = jnp.sum(lv[2:3] * lv[3:4], axis=-1, keepdims=True)
    return jnp.exp(a) - jnp.exp(b) + lam_init


def _half_mask(par):
    lane = lax.broadcasted_iota(jnp.int32, (1, LANES), 1)
    return (lane >= par * 64) & (lane < (par + 1) * 64)


def _diff_attention(qf, load_k, load_v, lam, shift):
    width = DA_HEADS * DA_V
    chunks = []
    for c in range(DA_HEADS // 2):
        o_par = []
        for par in range(2):
            lo = (2 * c + par) * DA_V
            q1 = jnp.where(_lane_mask(width, lo, DA_QK), qf, 0.0).astype(BF16)
            q2 = jnp.where(_lane_mask(width, lo + DA_QK, DA_QK), qf, 0.0).astype(BF16)
            e1, _ = _exp_scores(_scores(q1, load_k()), shift)
            e2, _ = _exp_scores(_scores(q2, load_k()), shift)
            d1 = jnp.sum(e1, axis=-1, keepdims=True)
            d2 = jnp.sum(e2, axis=-1, keepdims=True)
            w = (e1 - e2 * (lam * d1 * (1.0 / d2))).astype(BF16)
            o_par.append(jnp.dot(w, load_v(c), preferred_element_type=F32) * (1.0 / d1))
        chunks.append(jnp.where(_half_mask(0), o_par[0], o_par[1]))
    return chunks


def _diff_subln(o, subg, one_minus_lam_init):
    sq = o * o
    r = None
    for par in range(2):
        ms = jnp.sum(jnp.where(_half_mask(par), sq, 0.0), axis=-1, keepdims=True) * (1.0 / DA_V)
        rs = lax.rsqrt(ms + EPS)
        r = rs if r is None else jnp.where(_half_mask(0), r, rs)
    return o * r * subg * one_minus_lam_init


def _both_halves(x, g):
    swapped = pltpu.roll(x, 64, 1)
    return jnp.where(_half_mask(g), x, swapped)


def _swa_attention(qf, load_k, load_v, sinks, shift, valid=None):
    heads_per_group = SW_HEADS // SW_KV_HEADS
    outs = []
    for c in range(SW_HEADS // 2):
        qc = _chunk(qf, c)
        g = (2 * c) // heads_per_group
        o_par = []
        for par in range(2):
            h = 2 * c + par
            qm = jnp.where(_half_mask(par), qc, 0.0).astype(BF16)
            s = _scores(qm, load_k(g))
            if valid is not None:
                s = jnp.where(valid, s, -1e30)
            e, m = _exp_scores(s, shift, sink=sinks[h])
            o = jnp.dot(e.astype(BF16), load_v(g), preferred_element_type=F32)
            d = _chunk(o, 1) + jnp.exp2(sinks[h] - m)
            o_par.append(_chunk(o, 0) * (1.0 / d))
        outs.append(jnp.where(_half_mask(0), o_par[0], o_par[1]))
    return outs


def _mla_attention(load_q, load_k, load_v, shift):
    chunks = []
    for c in range(MLA_HEADS // 2):
        o_par = []
        for par in range(2):
            h = 2 * c + par
            e, _ = _exp_scores(_scores(load_q(h), load_k(h)), shift)
            o = jnp.dot(e.astype(BF16), load_v(h), preferred_element_type=F32)
            o_par.append(o * (1.0 / pltpu.roll(o, 64, 1)))
        chunks.append(jnp.where(_half_mask(0), o_par[0], o_par[1]))
    return chunks


def _ones_in_other_half(v, par):
    return jnp.where(_half_mask(par), v, 1.0)


def _mod_kernel(c_ref, w_ref, b_ref, o_ref):
    c = c_ref[...]
    s = c * (1.0 / (1.0 + jnp.exp(-c)))
    o_ref[0] = jnp.dot(s.astype(BF16), w_ref[0].astype(BF16), preferred_element_type=F32) + b_ref[0]


def _modulation(cvecs, w_ada, b_ada):
    n_out = 6 * D_MODEL
    out = pl.pallas_call(
        _mod_kernel,
        grid=(DEPTH, n_out // MOD_TILE),
        in_specs=[
            pl.BlockSpec((8, D_MODEL), lambda l, j: (0, 0)),
            pl.BlockSpec((1, D_MODEL, MOD_TILE), lambda l, j: (l, 0, j)),
            pl.BlockSpec((1, 1, MOD_TILE), lambda l, j: (l, 0, j)),
        ],
        out_specs=pl.BlockSpec((1, 8, MOD_TILE), lambda l, j: (l, 0, j)),
        out_shape=jax.ShapeDtypeStruct((DEPTH, 8, n_out), F32),
        compiler_params=pltpu.CompilerParams(
            dimension_semantics=("arbitrary", "arbitrary"), vmem_limit_bytes=VMEM_LIMIT),
        name="adaln_modulation",
    )(cvecs, w_ada, b_ada.reshape(DEPTH, 1, n_out))
    return out.reshape(DEPTH, 8, 6, D_MODEL)


def _ctx_kernel(l_ref, x_ref, mod_ref, n1g_ref, w_in_ref, qg_ref, wqb_ref, kvg_ref, wkvb_ref,
                lamv_ref, subg_ref, scal_ref, *rest, first_layer):
    dk_ref, dv_ref, sk_ref, sv_ref, ckv_ref, kr_ref, mix_ref = rest[-7:]
    l = l_ref[0]
    if first_layer:
        for ref in (dk_ref, dv_ref, sk_ref, sv_ref, ckv_ref, kr_ref):
            ref[:, 1:] = jnp.zeros((CTX_REQ_PER_STEP, DEPTH - 1) + ref.shape[2:], F32)
    proj = _project(x_ref, mod_ref, n1g_ref, w_in_ref)
    ckv_all = _rms(proj[:, C_CKV:C_KR], kvg_ref[0])
    mq_all = jnp.dot(_rms(proj[:, C_QA:C_CKV], qg_ref[0]).astype(BF16), wqb_ref[0], preferred_element_type=F32)
    mq_all = mq_all * (MLA_SCALE * LOG2E)
    kv_all = jnp.dot(ckv_all.astype(BF16), wkvb_ref[0], preferred_element_type=F32)
    lam = _diff_lambda(lamv_ref, scal_ref[l, S_LAM_INIT])
    subg = subg_ref[0]
    sinks = [scal_ref[l, h] for h in range(SW_HEADS)]
    shift = True

    for r in range(CTX_REQ_PER_STEP):
        r0, r1 = r * SEQ, (r + 1) * SEQ
        pr, mq, kv = proj[r0:r1], mq_all[r0:r1], kv_all[r0:r1]
        da_k = pr[:, C_DAK:C_DAV]
        da_v = pr[:, C_DAV:C_SWQ]
        sw_k = pr[:, C_SWK:C_SWV]
        sw_v = pr[:, C_SWV:C_QA]
        krz = pr[:, C_KR:P_EXT]
        dk_ref[r, 0] = da_k
        dv_ref[r, 0] = da_v
        sk_ref[r, 0] = sw_k
        sv_ref[r, 0] = sw_v
        ckv_ref[r, 0] = ckv_all[r0:r1]
        kr_ref[r, 0] = krz[:, 0:MLA_ROPE]

        da_kb = da_k.astype(BF16)
        da_o = _diff_attention(pr[:, C_DAQ:C_DAK] * (DA_SCALE * LOG2E), lambda: da_kb,
                               lambda c: _chunk(da_v, c).astype(BF16), lam, shift)
        for c, o in enumerate(da_o):
            mix_ref[r0:r1, c * LANES:(c + 1) * LANES] = _diff_subln(
                o, _chunk(subg, c), scal_ref[l, S_ONE_MINUS]).astype(BF16)
        sw_kg = [_both_halves(sw_k, g).astype(BF16) for g in range(SW_KV_HEADS)]
        sw_vg = [jnp.concatenate([_both_halves(sw_v, g), jnp.ones_like(sw_v)], axis=-1).astype(BF16)
                 for g in range(SW_KV_HEADS)]
        sw_o = _swa_attention(pr[:, C_SWQ:C_SWK] * (SW_SCALE * LOG2E), lambda g: sw_kg[g], lambda g: sw_vg[g],
                              sinks, shift)
        for c, o in enumerate(sw_o):
            mix_ref[r0:r1, 256 + c * LANES:256 + (c + 1) * LANES] = o.astype(BF16)
        mla_o = _mla_attention(
            lambda h: _chunk(mq, h).astype(BF16), lambda h: (_chunk(kv, h) + krz).astype(BF16),
            lambda h: _ones_in_other_half(_chunk(kv, MLA_HEADS + h // 2), h % 2).astype(BF16), shift)
        for c, o in enumerate(mla_o):
            mix_ref[r0:r1, 768 + c * LANES:768 + (c + 1) * LANES] = o.astype(BF16)


def _layer_spec(shape):
    nd = len(shape)
    return pl.BlockSpec((1,) + tuple(shape[1:]), lambda *a: (a[-1][0],) + (0,) * (nd - 1))


CACHE_WIDTHS = (DA_HEADS * 2 * DA_QK, DA_HEADS * DA_V, SW_KV_HEADS * SW_DIM, SW_KV_HEADS * SW_DIM,
                MLA_KV_RANK, MLA_ROPE)


def _ctx_layer(l_idx, x, mod, p, new_caches):
    n_req = BATCH // CTX_REQ_PER_STEP
    first_layer = new_caches is None
    if first_layer:
        new_caches = []
        out_shape = [jax.ShapeDtypeStruct((BATCH, DEPTH, SEQ, w), F32) for w in CACHE_WIDTHS]
        out_specs = [pl.BlockSpec((CTX_REQ_PER_STEP, DEPTH, SEQ, w), lambda i, l: (i, 0, 0, 0)) for w in CACHE_WIDTHS]
    else:
        out_shape = [jax.ShapeDtypeStruct(c.shape, c.dtype) for c in new_caches]
        out_specs = [pl.BlockSpec((CTX_REQ_PER_STEP, 1, SEQ, w), lambda i, l: (i, l[0], 0, 0)) for w in CACHE_WIDTHS]
    out_shape.append(jax.ShapeDtypeStruct((N_CTX_TOK, D_MODEL), BF16))
    out_specs.append(pl.BlockSpec((TM_CTX, D_MODEL), lambda i, l: (i, 0)))
    n_fixed_inputs = 12
    aliases = {n_fixed_inputs + j: j for j in range(len(new_caches))}
    grid_spec = pltpu.PrefetchScalarGridSpec(
        num_scalar_prefetch=1,
        grid=(n_req,),
        in_specs=[
            pl.BlockSpec((TM_CTX, D_MODEL), lambda i, l: (i, 0)),
            pl.BlockSpec((1, 1, 6, D_MODEL), lambda i, l: (l[0], CTX_MOD_ROW, 0, 0)),
            _layer_spec(p["norm1_g"].shape),
            _layer_spec(p["w_in"].shape),
            _layer_spec(p["mla_q_norm_g"].shape),
            _layer_spec(p["mla_w_qb"].shape),
            _layer_spec(p["mla_kv_norm_g"].shape),
            _layer_spec(p["mla_w_kvb"].shape),
            _layer_spec(p["lamv"].shape),
            _layer_spec(p["subg"].shape),
            pl.BlockSpec(memory_space=pltpu.SMEM),
        ] + [pl.BlockSpec(memory_space=pl.ANY)] * len(new_caches),
        out_specs=out_specs,
    )
    return pl.pallas_call(
        functools.partial(_ctx_kernel, first_layer=first_layer),
        grid_spec=grid_spec,
        out_shape=out_shape,
        input_output_aliases=aliases,
        compiler_params=pltpu.CompilerParams(
            dimension_semantics=("arbitrary",), vmem_limit_bytes=VMEM_LIMIT),
        name="ctx_project_attend",
    )(l_idx, x, mod, p["norm1_g"], p["w_in"], p["mla_q_norm_g"], p["mla_w_qb"], p["mla_kv_norm_g"],
      p["mla_w_kvb"], p["lamv"], p["subg"], p["scal"], *new_caches)


def _rope(x, cos, sa, sb, half):
    return x * cos + pltpu.roll(x, LANES - half, 1) * sa + pltpu.roll(x, half, 1) * sb


def _lat_proj_kernel(l_ref, x_ref, mod_ref, n1g_ref, w_in_ref, qg_ref, wqb_ref, kvg_ref, wkvb_ref,
                     c32_ref, a32_ref, b32_ref, c64_ref, a64_ref, b64_ref, cm_ref, am_ref, bm_ref, gq_ref,
                     q_ref, k_ref, v_ref, qmax_ref):
    proj = _project(x_ref, mod_ref, n1g_ref, w_in_ref)
    t32 = (c32_ref[...], a32_ref[...], b32_ref[...], DA_QK // 4)
    t64 = (c64_ref[...], a64_ref[...], b64_ref[...], SW_DIM // 4)
    tml = (cm_ref[...], am_ref[...], bm_ref[...], MLA_ROPE // 4)

    for j in range(2):
        q = _rope(_chunk(proj, j), *t32) * (DA_SCALE * LOG2E)
        q_ref[:, j * LANES:(j + 1) * LANES] = q.astype(BF16)
    for j in range(4):
        q = _rope(_chunk(proj, C_SWQ // LANES + j), *t64) * (SW_SCALE * LOG2E)
        q_ref[:, 256 + j * LANES:256 + (j + 1) * LANES] = q.astype(BF16)
    mq = jnp.dot(_rms(proj[:, C_QA:C_CKV], qg_ref[0]).astype(BF16), wqb_ref[0], preferred_element_type=F32)
    for h in range(MLA_HEADS):
        q = _rope(_chunk(mq, h), *tml) * (MLA_SCALE * LOG2E)
        q_ref[:, O_MQ + h * LANES:O_MQ + (h + 1) * LANES] = q.astype(BF16)
    qmax_ref[0] = _max_group_sq_norm(q_ref[...], gq_ref[...])

    for j in range(2):
        k_ref[:, j * LANES:(j + 1) * LANES] = _rope(_chunk(proj, C_DAK // LANES + j), *t32).astype(BF16)
    k_ref[:, 256:384] = _rope(proj[:, C_SWK:C_SWV], *t64).astype(BF16)
    ckv = _rms(proj[:, C_CKV:C_KR], kvg_ref[0])
    kv = jnp.dot(ckv.astype(BF16), wkvb_ref[0], preferred_element_type=F32)
    krz = _rope(proj[:, C_KR:P_EXT], *tml)
    for h in range(MLA_HEADS):
        k_ref[:, O_MK + h * LANES:O_MK + (h + 1) * LANES] = (_chunk(kv, h) + krz).astype(BF16)

    v_ref[:, 0:256] = proj[:, C_DAV:C_SWQ].astype(BF16)
    v_ref[:, 256:384] = proj[:, C_SWV:C_QA].astype(BF16)
    v_ref[:, 384:640] = kv[:, MLA_CHUNKS:].astype(BF16)


def _lat_project(l_idx, x, mod, p, tables):
    tiles_per_req = DEC_SEQ // TM_LAT
    tab_spec = pl.BlockSpec((TM_LAT, LANES), lambda i, l: (i % tiles_per_req, 0))
    grid_spec = pltpu.PrefetchScalarGridSpec(
        num_scalar_prefetch=1,
        grid=(N_LAT_TOK // TM_LAT,),
        in_specs=[
            pl.BlockSpec((TM_LAT, D_MODEL), lambda i, l: (i, 0)),
            pl.BlockSpec((1, 1, 6, D_MODEL), lambda i, l: (l[0], i // tiles_per_req, 0, 0)),
            _layer_spec(p["norm1_g"].shape),
            _layer_spec(p["w_in"].shape),
            _layer_spec(p["mla_q_norm_g"].shape),
            _layer_spec(p["mla_w_qb"].shape),
            _layer_spec(p["mla_kv_norm_g"].shape),
            _layer_spec(p["mla_w_kvb"].shape),
        ] + [tab_spec] * len(tables) + [pl.BlockSpec(p["gq"].shape, lambda i, l: (0, 0))],
        out_specs=[
            pl.BlockSpec((TM_LAT, Q_W), lambda i, l: (i, 0)),
            pl.BlockSpec((TM_LAT, K_W), lambda i, l: (i, 0)),
            pl.BlockSpec((TM_LAT, V_W), lambda i, l: (i, 0)),
            pl.BlockSpec((1, 1, LANES), lambda i, l: (i, 0, 0)),
        ],
    )
    return pl.pallas_call(
        _lat_proj_kernel,
        grid_spec=grid_spec,
        out_shape=[jax.ShapeDtypeStruct((N_LAT_TOK, Q_W), BF16),
                   jax.ShapeDtypeStruct((N_LAT_TOK, K_W), BF16),
                   jax.ShapeDtypeStruct((N_LAT_TOK, V_W), BF16),
                   jax.ShapeDtypeStruct((N_LAT_TOK // TM_LAT, 1, LANES), F32)],
        compiler_params=pltpu.CompilerParams(
            dimension_semantics=("arbitrary",), vmem_limit_bytes=VMEM_LIMIT),
        name="latent_project",
    )(l_idx, x, mod, p["norm1_g"], p["w_in"], p["mla_q_norm_g"], p["mla_w_qb"], p["mla_kv_norm_g"],
      p["mla_w_kvb"], *tables, p["gq"])


def _lat_attn_kernel(l_ref, q_ref, k_ref, v_ref, cdk_ref, cdv_ref, csk_ref, csv_ref, cckv_ref, ckr_ref,
                     wkvb_ref, pad_ref, lamv_ref, subg_ref, qmax_ref, gk_ref, scal_ref,
                     mix_ref,
                     kda, vda, ksw, vsw, kml, vml, kmax_sq):
    l = l_ref[0]
    qb = pl.program_id(1)

    @pl.when(qb == 0)
    def _stage_keys():
        kda[0:PAST_LEN, :] = cdk_ref[0, 0].astype(BF16)
        kda[PAST_LEN:N_KEYS, :] = k_ref[0, :, 0:256]
        vda[0:PAST_LEN, :] = cdv_ref[0, 0].astype(BF16)
        vda[PAST_LEN:N_KEYS, :] = v_ref[0, :, 0:256]
        sk_lat = k_ref[0, :, 256:384].astype(F32)
        sv_lat = v_ref[0, :, 256:384].astype(F32)
        for g in range(SW_KV_HEADS):
            ksw[0:PAST_LEN, g * LANES:(g + 1) * LANES] = _both_halves(csk_ref[0, 0], g).astype(BF16)
            ksw[PAST_LEN:N_KEYS, g * LANES:(g + 1) * LANES] = _both_halves(sk_lat, g).astype(BF16)
            vsw[0:PAST_LEN, 2 * g * LANES:(2 * g + 1) * LANES] = _both_halves(csv_ref[0, 0], g).astype(BF16)
            vsw[PAST_LEN:N_KEYS, 2 * g * LANES:(2 * g + 1) * LANES] = _both_halves(sv_lat, g).astype(BF16)
            vsw[:, (2 * g + 1) * LANES:(2 * g + 2) * LANES] = jnp.ones((N_KEYS, LANES), BF16)
        kvc = jnp.dot(cckv_ref[0, 0].astype(BF16), wkvb_ref[0], preferred_element_type=F32)
        krz = jnp.dot(ckr_ref[0, 0].astype(BF16), pad_ref[...], preferred_element_type=F32)
        mk_ctx = [_chunk(kvc, h) + krz for h in range(MLA_HEADS)]
        for h in range(MLA_HEADS):
            kml[0:PAST_LEN, h * LANES:(h + 1) * LANES] = mk_ctx[h].astype(BF16)
            v_ctx = _chunk(kvc, MLA_HEADS + h // 2)
            v_lat = v_ref[0, :, 384 + (h // 2) * LANES:384 + (h // 2 + 1) * LANES].astype(F32)
            vml[0:PAST_LEN, h * LANES:(h + 1) * LANES] = _ones_in_other_half(v_ctx, h % 2).astype(BF16)
            vml[PAST_LEN:N_KEYS, h * LANES:(h + 1) * LANES] = _ones_in_other_half(v_lat, h % 2).astype(BF16)
        kml[PAST_LEN:N_KEYS, :] = k_ref[0, :, O_MK:K_W]
        gk = gk_ref[...]
        k_ctx = jnp.concatenate([cdk_ref[0, 0], csk_ref[0, 0]] + mk_ctx, axis=-1)
        kmax_sq[...] = jnp.maximum(_max_group_sq_norm(k_ctx, gk), _max_group_sq_norm(k_ref[0], gk))

    bound_sq = qmax_ref[0] * kmax_sq[...] * BOUND_MARGIN
    small = (jnp.max(bound_sq) <= NOSHIFT_BOUND ** 2) & (scal_ref[l, S_SINK_ABSMAX] <= NOSHIFT_BOUND)

    lam = _diff_lambda(lamv_ref, scal_ref[l, S_LAM_INIT])
    subg = subg_ref[0]
    sinks = [scal_ref[l, h] for h in range(SW_HEADS)]
    q0 = qb * TQ
    start = pl.multiple_of(jnp.clip(q0 - WINDOW, 0, DEC_SEQ - SW_BAND), LANES)

    def attend(shift):
        da_o = _diff_attention(q_ref[0, :, 0:256].astype(F32), lambda: kda[...],
                               lambda c: vda[:, c * LANES:(c + 1) * LANES], lam, shift)
        for c, o in enumerate(da_o):
            mix_ref[0, :, c * LANES:(c + 1) * LANES] = _diff_subln(
                o, _chunk(subg, c), scal_ref[l, S_ONE_MINUS]).astype(BF16)

        band = pl.ds(PAST_LEN + start, SW_BAND)
        k_sel = jnp.concatenate([ksw[0:PAST_LEN, :], ksw[band, :]], axis=0)
        v_sel = jnp.concatenate([vsw[0:PAST_LEN, :], vsw[band, :]], axis=0)
        n_sel = PAST_LEN + SW_BAND
        col = lax.broadcasted_iota(jnp.int32, (TQ, n_sel), 1)
        row = lax.broadcasted_iota(jnp.int32, (TQ, n_sel), 0)
        dist = (q0 + row) - (start + col - PAST_LEN)
        valid = (col < PAST_LEN) | ((dist <= WINDOW) & (dist >= -WINDOW))
        sw_o = _swa_attention(q_ref[0, :, 256:768].astype(F32), lambda g: _chunk(k_sel, g),
                              lambda g: v_sel[:, 2 * g * LANES:(2 * g + 2) * LANES], sinks, shift, valid)
        for c, o in enumerate(sw_o):
            mix_ref[0, :, 256 + c * LANES:256 + (c + 1) * LANES] = o.astype(BF16)

        mla_o = _mla_attention(lambda h: q_ref[0, :, O_MQ + h * LANES:O_MQ + (h + 1) * LANES],
                               lambda h: kml[:, h * LANES:(h + 1) * LANES],
                               lambda h: vml[:, h * LANES:(h + 1) * LANES], shift)
        for c, o in enumerate(mla_o):
            mix_ref[0, :, 768 + c * LANES:768 + (c + 1) * LANES] = o.astype(BF16)

    pl.when(small)(functools.partial(attend, False))
    pl.when(jnp.logical_not(small))(functools.partial(attend, True))


def _lat_attend(l_idx, q, k, v, qmax, caches, p, pad_eye):
    cdk, cdv, csk, csv, cckv, ckr = caches
    proj_tiles_per_req = DEC_SEQ // TM_LAT

    def cache_spec(a):
        return pl.BlockSpec((1, 1) + a.shape[2:], lambda b, j, l: (b, l[0], 0, 0))

    grid_spec = pltpu.PrefetchScalarGridSpec(
        num_scalar_prefetch=1,
        grid=(DEC_BATCH, DEC_SEQ // TQ),
        in_specs=[
            pl.BlockSpec((1, TQ, Q_W), lambda b, j, l: (b, j, 0)),
            pl.BlockSpec((1, DEC_SEQ, K_W), lambda b, j, l: (b, 0, 0)),
            pl.BlockSpec((1, DEC_SEQ, V_W), lambda b, j, l: (b, 0, 0)),
            cache_spec(cdk), cache_spec(cdv), cache_spec(csk), cache_spec(csv), cache_spec(cckv), cache_spec(ckr),
            _layer_spec(p["mla_w_kvb"].shape),
            pl.BlockSpec(pad_eye.shape, lambda b, j, l: (0, 0)),
            _layer_spec(p["lamv"].shape),
            _layer_spec(p["subg"].shape),
            pl.BlockSpec((1, 1, LANES), lambda b, j, l: (b * proj_tiles_per_req + (j * TQ) // TM_LAT, 0, 0)),
            pl.BlockSpec(p["gk"].shape, lambda b, j, l: (0, 0)),
            pl.BlockSpec(memory_space=pltpu.SMEM),
        ],
        out_specs=pl.BlockSpec((1, TQ, D_MODEL), lambda b, j, l: (b, j, 0)),
        scratch_shapes=[
            pltpu.VMEM((N_KEYS, 256), BF16), pltpu.VMEM((N_KEYS, 256), BF16),
            pltpu.VMEM((N_KEYS, SW_KV_HEADS * LANES), BF16), pltpu.VMEM((N_KEYS, SW_KV_HEADS * 2 * LANES), BF16),
            pltpu.VMEM((N_KEYS, MLA_CHUNKS), BF16), pltpu.VMEM((N_KEYS, MLA_CHUNKS), BF16),
            pltpu.VMEM((1, LANES), F32),
        ],
    )
    return pl.pallas_call(
        _lat_attn_kernel,
        grid_spec=grid_spec,
        out_shape=jax.ShapeDtypeStruct((DEC_BATCH, DEC_SEQ, D_MODEL), BF16),
        compiler_params=pltpu.CompilerParams(
            dimension_semantics=("arbitrary", "arbitrary"), vmem_limit_bytes=VMEM_LIMIT),
        name="latent_attend",
    )(l_idx, q.reshape(DEC_BATCH, DEC_SEQ, Q_W), k.reshape(DEC_BATCH, DEC_SEQ, K_W),
      v.reshape(DEC_BATCH, DEC_SEQ, V_W), cdk, cdv, csk, csv, cckv, ckr,
      p["mla_w_kvb"], pad_eye, p["lamv"], p["subg"], qmax, p["gk"], p["scal"])


def _mlp_kernel(l_ref, x_ref, mix_ref, mod_ref, n2g_ref, w_out_ref, w_up_ref, w_down_ref, fg_ref, o_ref,
                *, final_norm):
    m = mod_ref[0, 0]
    x = x_ref[...] + m[2:3] * jnp.dot(mix_ref[...], w_out_ref[0], preferred_element_type=F32)
    h = (_rms(x, n2g_ref[0]) * (1.0 + m[4:5]) + m[3:4]).astype(BF16)
    acc = jnp.zeros_like(x)
    for c in range(D_FF // FF_CHUNK):
        u = jnp.dot(h, w_up_ref[0, :, c * FF_CHUNK:(c + 1) * FF_CHUNK], preferred_element_type=F32)
        a = jnp.square(jnp.maximum(u, 0.0)).astype(BF16)
        acc = acc + jnp.dot(a, w_down_ref[0, c * FF_CHUNK:(c + 1) * FF_CHUNK, :], preferred_element_type=F32)
    x = x + m[5:6] * acc
    if final_norm:
        x = _rms(x, fg_ref[...])
    o_ref[...] = x


def _mlp_layer(l_idx, x, mix, mod, p, final_g, mod_row, final_norm):
    n_tok = x.shape[0]
    grid_spec = pltpu.PrefetchScalarGridSpec(
        num_scalar_prefetch=1,
        grid=(n_tok // TM_MLP,),
        in_specs=[
            pl.BlockSpec((TM_MLP, D_MODEL), lambda i, l: (i, 0)),
            pl.BlockSpec((TM_MLP, D_MODEL), lambda i, l: (i, 0)),
            pl.BlockSpec((1, 1, 6, D_MODEL), lambda i, l: (l[0], mod_row(i), 0, 0)),
            _layer_spec(p["norm2_g"].shape),
            _layer_spec(p["w_out"].shape),
            _layer_spec(p["w_up"].shape),
            _layer_spec(p["w_down"].shape),
            pl.BlockSpec((1, D_MODEL), lambda i, l: (0, 0)),
        ],
        out_specs=pl.BlockSpec((TM_MLP, D_MODEL), lambda i, l: (i, 0)),
    )
    return pl.pallas_call(
        functools.partial(_mlp_kernel, final_norm=final_norm),
        grid_spec=grid_spec,
        out_shape=jax.ShapeDtypeStruct((n_tok, D_MODEL), F32),
        compiler_params=pltpu.CompilerParams(
            dimension_semantics=("arbitrary",), vmem_limit_bytes=VMEM_LIMIT),
        name="out_proj_mlp",
    )(l_idx, x, mix, mod, p["norm2_g"], p["w_out"], p["w_up"], p["w_down"], final_g)


def _rope_tables(period, active_lanes=LANES):
    m = period // 2
    half = m // 2
    t = np.arange(DEC_SEQ)
    pos = np.stack([t // GRID_W, t % GRID_W], axis=0).astype(np.float64)
    lane = np.arange(LANES) % period
    axis = lane // m
    k = lane % m
    freq = ROPE_BASE ** (-(k % half).astype(np.float64) / half)
    ang = pos[axis, :].T * freq[None, :]
    first = (k < half)[None, :]
    active = (np.arange(LANES) < active_lanes)[None, :]
    cos, sin = np.where(active, np.cos(ang), 1.0), np.where(active, np.sin(ang), 0.0)
    return (jnp.asarray(cos, F32), jnp.asarray(np.where(first, -sin, 0.0), F32),
            jnp.asarray(np.where(first, 0.0, sin), F32))


def _group_matrices():
    gq = np.zeros((Q_W, LANES), np.float32)
    gk = np.zeros((K_W, LANES), np.float32)
    for d in range(DA_HEADS * 2 * DA_QK):
        gq[d, G_DIFF + d // DA_QK] = 1.0
        gk[d, G_DIFF + d // DA_QK] = 1.0
    for h in range(SW_HEADS):
        g = h // (SW_HEADS // SW_KV_HEADS)
        gq[256 + h * SW_DIM:256 + (h + 1) * SW_DIM, G_SWA + h] = 1.0
        gk[256 + g * SW_DIM:256 + (g + 1) * SW_DIM, G_SWA + h] = 1.0
    for h in range(MLA_HEADS):
        gq[O_MQ + h * LANES:O_MQ + (h + 1) * LANES, G_MLA + h] = 1.0
        gk[O_MK + h * LANES:O_MK + (h + 1) * LANES, G_MLA + h] = 1.0
    return jnp.asarray(gq, BF16), jnp.asarray(gk, BF16)


def _prepare_params(norm1_g, norm2_g, w_in, w_out, lq1, lk1, lq2, lk2, diff_subln_g, swa_sink,
                    mla_q_norm_g, mla_w_qb, mla_kv_norm_g, mla_w_kvb, w_up, w_down):
    w_in_ext = jnp.pad(w_in.astype(BF16), ((0, 0), (0, 0), (0, P_EXT - w_in.shape[-1])))
    w_out_p = w_out.astype(BF16)
    dq = MLA_NOPE + MLA_ROPE
    gap = LANES - dq
    zq = jnp.zeros((DEPTH, MLA_Q_RANK, gap), mla_w_qb.dtype)
    wqb = jnp.concatenate(
        [blk for h in range(MLA_HEADS)
         for blk in (mla_w_qb[:, :, h * dq + MLA_NOPE:(h + 1) * dq], zq, mla_w_qb[:, :, h * dq:h * dq + MLA_NOPE])],
        axis=-1).astype(BF16)
    dkv = MLA_NOPE + MLA_V
    zk = jnp.zeros((DEPTH, MLA_KV_RANK, LANES - MLA_NOPE), mla_w_kvb.dtype)
    wkvb = jnp.concatenate(
        [blk for h in range(MLA_HEADS) for blk in (zk, mla_w_kvb[:, :, h * dkv:h * dkv + MLA_NOPE])]
        + [mla_w_kvb[:, :, h * dkv + MLA_NOPE:(h + 1) * dkv] for h in range(MLA_HEADS)],
        axis=-1).astype(BF16)
    lam_init = np.array([0.8 - 0.6 * math.exp(-0.3 * l) for l in range(DEPTH)], np.float32)
    sink_l2 = swa_sink.astype(F32) * LOG2E
    scal = jnp.concatenate([
        sink_l2,
        jnp.asarray(np.stack([lam_init, (1.0 - lam_init.astype(np.float64)).astype(np.float32)], axis=1)),
        jnp.max(jnp.abs(sink_l2), axis=1, keepdims=True),
        jnp.zeros((DEPTH, S_COLS - SW_HEADS - 3), F32)], axis=1)
    gq, gk = _group_matrices()
    return {
        "gq": gq,
        "gk": gk,
        "norm1_g": norm1_g.reshape(DEPTH, 1, D_MODEL),
        "norm2_g": norm2_g.reshape(DEPTH, 1, D_MODEL),
        "w_in": w_in_ext,
        "w_out": w_out_p,
        "mla_q_norm_g": mla_q_norm_g.reshape(DEPTH, 1, MLA_Q_RANK),
        "mla_w_qb": wqb,
        "mla_kv_norm_g": mla_kv_norm_g.reshape(DEPTH, 1, MLA_KV_RANK),
        "mla_w_kvb": wkvb,
        "lamv": jnp.stack([lq1, lk1, lq2, lk2], axis=1),
        "subg": jnp.tile(diff_subln_g, (1, DA_HEADS)).reshape(DEPTH, 1, DA_HEADS * DA_V),
        "scal": scal,
        "w_up": w_up.astype(BF16),
        "w_down": w_down.astype(BF16),
    }


def kernel(x_prompt, x_sample, c, cache_diff_k, cache_diff_v, cache_swa_k, cache_swa_v, cache_mla_ckv,
           cache_mla_krope, c_ctx, norm1_g, norm2_g, w_ada, b_ada, w_in, w_out, diff_lambda_q1,
           diff_lambda_k1, diff_lambda_q2, diff_lambda_k2, diff_subln_g, swa_sink, mla_q_norm_g,
           mla_w_qb, mla_kv_norm_g, mla_w_kvb, w_up, w_down, final_g):
    p = _prepare_params(norm1_g, norm2_g, w_in, w_out, diff_lambda_q1, diff_lambda_k1, diff_lambda_q2,
                        diff_lambda_k2, diff_subln_g, swa_sink, mla_q_norm_g, mla_w_qb, mla_kv_norm_g,
                        mla_w_kvb, w_up, w_down)
    cvecs = jnp.concatenate([c, c_ctx[None, :], jnp.zeros((8 - DEC_BATCH - 1, D_MODEL), F32)], axis=0)
    mod = _modulation(cvecs, w_ada, b_ada)
    tables = _rope_tables(DA_QK) + _rope_tables(SW_DIM) + _rope_tables(MLA_ROPE, MLA_ROPE)
    pad_eye = jnp.asarray(np.eye(MLA_ROPE, LANES, dtype=np.float32), BF16)
    final_g2 = final_g.reshape(1, D_MODEL)
    caches = (cache_diff_k.reshape(DEC_BATCH, DEPTH, PAST_LEN, DA_HEADS * 2 * DA_QK),
              cache_diff_v.reshape(DEC_BATCH, DEPTH, PAST_LEN, DA_HEADS * DA_V),
              cache_swa_k.reshape(DEC_BATCH, DEPTH, PAST_LEN, SW_KV_HEADS * SW_DIM),
              cache_swa_v.reshape(DEC_BATCH, DEPTH, PAST_LEN, SW_KV_HEADS * SW_DIM),
              cache_mla_ckv, cache_mla_krope)

    x_ctx = x_prompt.reshape(N_CTX_TOK, D_MODEL)
    x_lat = x_sample.reshape(N_LAT_TOK, D_MODEL)
    new = None
    lat_tiles_per_req = DEC_SEQ // TM_MLP
    for l in range(DEPTH):
        l_idx = jnp.full((1,), l, jnp.int32)
        last = l == DEPTH - 1
        *new, mix_ctx = _ctx_layer(l_idx, x_ctx, mod, p, new)
        x_ctx = _mlp_layer(l_idx, x_ctx, mix_ctx, mod, p, final_g2, lambda i: CTX_MOD_ROW, last)
        q, k, v, qmax = _lat_project(l_idx, x_lat, mod, p, tables)
        mix_lat = _lat_attend(l_idx, q, k, v, qmax, caches, p, pad_eye).reshape(N_LAT_TOK, D_MODEL)
        x_lat = _mlp_layer(l_idx, x_lat, mix_lat, mod, p, final_g2, lambda i: i // lat_tiles_per_req, last)

    y_prompt = x_ctx.reshape(BATCH, SEQ, D_MODEL)
    y_sample = x_lat.reshape(DEC_BATCH, DEC_SEQ, D_MODEL)
    new_dk, new_dv, new_sk, new_sv, new_ckv, new_kr = new
    return (y_prompt, y_sample,
            new_dk.reshape(BATCH, DEPTH, SEQ, DA_HEADS, 2 * DA_QK),
            new_dv.reshape(BATCH, DEPTH, SEQ, DA_HEADS, DA_V),
            new_sk.reshape(BATCH, DEPTH, SEQ, SW_KV_HEADS, SW_DIM),
            new_sv.reshape(BATCH, DEPTH, SEQ, SW_KV_HEADS, SW_DIM),
            new_ckv, new_kr)
```

```python
import functools
import math

import numpy as np
import jax
import jax.numpy as jnp
from jax import lax
from jax.experimental import pallas as pl
from jax.experimental.pallas import tpu as pltpu

F32 = jnp.float32
BF16 = jnp.bfloat16

D_MODEL = 1024
BATCH = 32
SEQ = 256
DEPTH = 4
DEC_BATCH = 4
DEC_SEQ = 2048
PAST_LEN = 256
GRID_W = 64
WINDOW = 128
ROPE_BASE = 10000.0
EPS = 1e-6
DA_HEADS = 4
DA_QK = 32
DA_V = 64
SW_HEADS = 8
SW_KV_HEADS = 2
SW_DIM = 64
MLA_HEADS = 4
MLA_Q_RANK = 256
MLA_KV_RANK = 128
MLA_NOPE = 64
MLA_ROPE = 32
MLA_V = 64
D_FF = 4 * D_MODEL

LANES = 128
N_CTX_TOK = BATCH * SEQ
N_LAT_TOK = DEC_BATCH * DEC_SEQ
N_KEYS = PAST_LEN + DEC_SEQ
CTX_MOD_ROW = DEC_BATCH

C_DAQ, C_DAK, C_DAV, C_SWQ, C_SWK, C_SWV, C_QA, C_CKV, C_KR, P_EXT = (
    0, 256, 512, 768, 1280, 1408, 1536, 1792, 1920, 2048)
MLA_CHUNKS = MLA_HEADS * 128
Q_W = 256 + 512 + MLA_CHUNKS
K_W = 256 + 128 + MLA_CHUNKS
V_W = 256 + 128 + 256
O_MQ = 768
O_MK = 384
LOG2E = 1.4426950408889634

DA_SCALE = DA_QK ** -0.5
SW_SCALE = SW_DIM ** -0.5
MLA_SCALE = (MLA_NOPE + MLA_ROPE) ** -0.5

CTX_REQ_PER_TILE = 2
TM_CTX = CTX_REQ_PER_TILE * SEQ
TM_LAT = 512
TM_MLP = 512
TQ = 256
SW_BAND = TQ + 2 * WINDOW
FF_CHUNK = 1024
MOD_TILE = 1536
VMEM_LIMIT = 56 * 1024 * 1024

_NT = (((1,), (1,)), ((), ()))


def _scores(q, k):
    return lax.dot_general(q, k, _NT, preferred_element_type=F32)

S_LAM_INIT = SW_HEADS
S_ONE_MINUS = SW_HEADS + 1
S_SINK_ABSMAX = SW_HEADS + 2
S_COLS = 16

NOSHIFT_BOUND = 50.0
G_DIFF, G_SWA, G_MLA = 0, 8, 16


def _rms(x, g):
    ms = jnp.mean(x * x, axis=-1, keepdims=True)
    return x * lax.rsqrt(ms + EPS) * g


def _lane_mask(n, lo, width):
    lane = lax.broadcasted_iota(jnp.int32, (1, n), 1)
    return (lane >= lo) & (lane < lo + width)


def _exp_scores(s, shift, sink=None):
    if not shift:
        return jnp.exp2(s), jnp.zeros((1, 1), F32)
    m = jnp.max(s, axis=-1, keepdims=True)
    if sink is not None:
        m = jnp.maximum(m, sink)
    return jnp.exp2(s - m), m


def _sq_bf16(x):
    xf = x.astype(F32)
    return (xf * xf).astype(BF16)


def _max_group_sq_norm(x, g):
    return jnp.max(jnp.dot(_sq_bf16(x), g, preferred_element_type=F32), axis=0, keepdims=True)


BOUND_MARGIN = 1.03


def _chunk(a, j):
    return a[:, j * LANES:(j + 1) * LANES]


def _project(x_ref, mod_ref, n1g_ref, w_in_ref):
    m = mod_ref[0, 0]
    h = _rms(x_ref[...], n1g_ref[0]) * (1.0 + m[1:2]) + m[0:1]
    return jnp.dot(h.astype(BF16), w_in_ref[0], preferred_element_type=F32)


def _diff_lambda(lamv_ref, lam_init):
    lv = lamv_ref[0]
    a = jnp.sum(lv[0:1] * lv[1:2], axis=-1, keepdims=True)
    b = jnp.sum(lv[2:3] * lv[3:4], axis=-1, keepdims=True)
    return jnp.exp(a) - jnp.exp(b) + lam_init


def _half_mask(par):
    lane = lax.broadcasted_iota(jnp.int32, (1, LANES), 1)
    return (lane >= par * 64) & (lane < (par + 1) * 64)


def _diff_attention(qf, load_k, load_v, lam, shift):
    width = DA_HEADS * DA_V
    chunks = []
    for c in range(DA_HEADS // 2):
        o_par = []
        for par in range(2):
            lo = (2 * c + par) * DA_V
            q1 = jnp.where(_lane_mask(width, lo, DA_QK), qf, 0.0).astype(BF16)
            q2 = jnp.where(_lane_mask(width, lo + DA_QK, DA_QK), qf, 0.0).astype(BF16)
            e1, _ = _exp_scores(_scores(q1, load_k()), shift)
            e2, _ = _exp_scores(_scores(q2, load_k()), shift)
            d1 = jnp.sum(e1, axis=-1, keepdims=True)
            d2 = jnp.sum(e2, axis=-1, keepdims=True)
            w = (e1 - e2 * (lam * d1 * (1.0 / d2))).astype(BF16)
            o_par.append(jnp.dot(w, load_v(c), preferred_element_type=F32) * (1.0 / d1))
        chunks.append(jnp.where(_half_mask(0), o_par[0], o_par[1]))
    return chunks


def _diff_subln(o, subg, one_minus_lam_init):
    sq = o * o
    r = None
    for par in range(2):
        ms = jnp.sum(jnp.where(_half_mask(par), sq, 0.0), axis=-1, keepdims=True) * (1.0 / DA_V)
        rs = lax.rsqrt(ms + EPS)
        r = rs if r is None else jnp.where(_half_mask(0), r, rs)
    return o * r * subg * one_minus_lam_init


def _both_halves(x, g):
    swapped = pltpu.roll(x, 64, 1)
    return jnp.where(_half_mask(g), x, swapped)


def _swa_attention(qf, load_k, load_v, sinks, shift, valid=None):
    heads_per_group = SW_HEADS // SW_KV_HEADS
    outs = []
    for c in range(SW_HEADS // 2):
        qc = _chunk(qf, c)
        g = (2 * c) // heads_per_group
        o_par = []
        for par in range(2):
            h = 2 * c + par
            qm = jnp.where(_half_mask(par), qc, 0.0).astype(BF16)
            s = _scores(qm, load_k(g))
            if valid is not None:
                s = jnp.where(valid, s, -1e30)
            e, m = _exp_scores(s, shift, sink=sinks[h])
            o = jnp.dot(e.astype(BF16), load_v(g), preferred_element_type=F32)
            d = _chunk(o, 1) + jnp.exp2(sinks[h] - m)
            o_par.append(_chunk(o, 0) * (1.0 / d))
        outs.append(jnp.where(_half_mask(0), o_par[0], o_par[1]))
    return outs


def _mla_attention(load_q, load_k, load_v, shift):
    chunks = []
    for c in range(MLA_HEADS // 2):
        o_par = []
        for par in range(2):
            h = 2 * c + par
            e, _ = _exp_scores(_scores(load_q(h), load_k(h)), shift)
            o = jnp.dot(e.astype(BF16), load_v(h), preferred_element_type=F32)
            o_par.append(o * (1.0 / pltpu.roll(o, 64, 1)))
        chunks.append(jnp.where(_half_mask(0), o_par[0], o_par[1]))
    return chunks


def _ones_in_other_half(v, par):
    return jnp.where(_half_mask(par), v, 1.0)


def _mod_kernel(c_ref, w_ref, b_ref, o_ref):
    c = c_ref[...]
    s = c * (1.0 / (1.0 + jnp.exp(-c)))
    o_ref[0] = jnp.dot(s.astype(BF16), w_ref[0].astype(BF16), preferred_element_type=F32) + b_ref[0]


def _modulation(cvecs, w_ada, b_ada):
    n_out = 6 * D_MODEL
    out = pl.pallas_call(
        _mod_kernel,
        grid=(DEPTH, n_out // MOD_TILE),
        in_specs=[
            pl.BlockSpec((8, D_MODEL), lambda l, j: (0, 0)),
            pl.BlockSpec((1, D_MODEL, MOD_TILE), lambda l, j: (l, 0, j)),
            pl.BlockSpec((1, 1, MOD_TILE), lambda l, j: (l, 0, j)),
        ],
        out_specs=pl.BlockSpec((1, 8, MOD_TILE), lambda l, j: (l, 0, j)),
        out_shape=jax.ShapeDtypeStruct((DEPTH, 8, n_out), F32),
        compiler_params=pltpu.CompilerParams(
            dimension_semantics=("arbitrary", "arbitrary"), vmem_limit_bytes=VMEM_LIMIT),
        name="adaln_modulation",
    )(cvecs, w_ada, b_ada.reshape(DEPTH, 1, n_out))
    return out.reshape(DEPTH, 8, 6, D_MODEL)


def _ctx_proj_kernel(l_ref, x_ref, mod_ref, n1g_ref, w_in_ref, qg_ref, wqb_ref, kvg_ref, wkvb_ref, gq_ref, gk_ref,
                     *rest, first_layer):
    dk_ref, dv_ref, sk_ref, sv_ref, ckv_ref, kr_ref, q_ref, k_ref, v_ref, qmax_ref, kmax_ref = rest[-11:]
    if first_layer:
        for ref in (dk_ref, dv_ref, sk_ref, sv_ref, ckv_ref, kr_ref):
            ref[:, 1:] = jnp.zeros((CTX_REQ_PER_TILE, DEPTH - 1) + ref.shape[2:], F32)
    proj = _project(x_ref, mod_ref, n1g_ref, w_in_ref)
    ckv = _rms(proj[:, C_CKV:C_KR], kvg_ref[0])
    mq = jnp.dot(_rms(proj[:, C_QA:C_CKV], qg_ref[0]).astype(BF16), wqb_ref[0], preferred_element_type=F32)
    kv = jnp.dot(ckv.astype(BF16), wkvb_ref[0], preferred_element_type=F32)
    krz = proj[:, C_KR:P_EXT]
    for r in range(CTX_REQ_PER_TILE):
        r0, r1 = r * SEQ, (r + 1) * SEQ
        dk_ref[r, 0] = proj[r0:r1, C_DAK:C_DAV]
        dv_ref[r, 0] = proj[r0:r1, C_DAV:C_SWQ]
        sk_ref[r, 0] = proj[r0:r1, C_SWK:C_SWV]
        sv_ref[r, 0] = proj[r0:r1, C_SWV:C_QA]
        ckv_ref[r, 0] = ckv[r0:r1]
        kr_ref[r, 0] = krz[r0:r1, 0:MLA_ROPE]

    q_ref[:, 0:256] = (proj[:, C_DAQ:C_DAK] * (DA_SCALE * LOG2E)).astype(BF16)
    q_ref[:, 256:O_MQ] = (proj[:, C_SWQ:C_SWK] * (SW_SCALE * LOG2E)).astype(BF16)
    q_ref[:, O_MQ:Q_W] = (mq * (MLA_SCALE * LOG2E)).astype(BF16)
    k_ref[:, 0:256] = proj[:, C_DAK:C_DAV].astype(BF16)
    k_ref[:, 256:O_MK] = proj[:, C_SWK:C_SWV].astype(BF16)
    for h in range(MLA_HEADS):
        k_ref[:, O_MK + h * LANES:O_MK + (h + 1) * LANES] = (_chunk(kv, h) + krz).astype(BF16)
    v_ref[:, 0:256] = proj[:, C_DAV:C_SWQ].astype(BF16)
    v_ref[:, 256:384] = proj[:, C_SWV:C_QA].astype(BF16)
    v_ref[:, 384:V_W] = kv[:, MLA_CHUNKS:].astype(BF16)
    for r in range(CTX_REQ_PER_TILE):
        r0, r1 = r * SEQ, (r + 1) * SEQ
        qmax_ref[r] = _max_group_sq_norm(q_ref[r0:r1, :], gq_ref[...])
        kmax_ref[r] = _max_group_sq_norm(k_ref[r0:r1, :], gk_ref[...])


def _ctx_attn_kernel(l_ref, q_ref, k_ref, v_ref, qmax_ref, kmax_ref, lamv_ref, subg_ref, scal_ref, mix_ref):
    l = l_ref[0]
    bound_sq = qmax_ref[0] * kmax_ref[0] * BOUND_MARGIN
    small = (jnp.max(bound_sq) <= NOSHIFT_BOUND ** 2) & (scal_ref[l, S_SINK_ABSMAX] <= NOSHIFT_BOUND)
    lam = _diff_lambda(lamv_ref, scal_ref[l, S_LAM_INIT])
    subg = subg_ref[0]
    sinks = [scal_ref[l, h] for h in range(SW_HEADS)]

    def attend(shift):
        da_o = _diff_attention(q_ref[:, 0:256].astype(F32), lambda: k_ref[:, 0:256],
                               lambda c: v_ref[:, c * LANES:(c + 1) * LANES], lam, shift)
        for c, o in enumerate(da_o):
            mix_ref[:, c * LANES:(c + 1) * LANES] = _diff_subln(
                o, _chunk(subg, c), scal_ref[l, S_ONE_MINUS]).astype(BF16)
        sw_k = k_ref[:, 256:O_MK].astype(F32)
        sw_v = v_ref[:, 256:384].astype(F32)
        sw_kg = [_both_halves(sw_k, g).astype(BF16) for g in range(SW_KV_HEADS)]
        sw_vg = [jnp.concatenate([_both_halves(sw_v, g), jnp.ones_like(sw_v)], axis=-1).astype(BF16)
                 for g in range(SW_KV_HEADS)]
        sw_o = _swa_attention(q_ref[:, 256:O_MQ].astype(F32), lambda g: sw_kg[g], lambda g: sw_vg[g], sinks, shift)
        for c, o in enumerate(sw_o):
            mix_ref[:, 256 + c * LANES:256 + (c + 1) * LANES] = o.astype(BF16)
        mla_o = _mla_attention(
            lambda h: q_ref[:, O_MQ + h * LANES:O_MQ + (h + 1) * LANES],
            lambda h: k_ref[:, O_MK + h * LANES:O_MK + (h + 1) * LANES],
            lambda h: _ones_in_other_half(
                v_ref[:, 384 + (h // 2) * LANES:384 + (h // 2 + 1) * LANES].astype(F32), h % 2).astype(BF16),
            shift)
        for c, o in enumerate(mla_o):
            mix_ref[:, 768 + c * LANES:768 + (c + 1) * LANES] = o.astype(BF16)

    pl.when(small)(functools.partial(attend, False))
    pl.when(jnp.logical_not(small))(functools.partial(attend, True))


def _layer_spec(shape):
    nd = len(shape)
    return pl.BlockSpec((1,) + tuple(shape[1:]), lambda *a: (a[-1][0],) + (0,) * (nd - 1))


CACHE_WIDTHS = (DA_HEADS * 2 * DA_QK, DA_HEADS * DA_V, SW_KV_HEADS * SW_DIM, SW_KV_HEADS * SW_DIM,
                MLA_KV_RANK, MLA_ROPE)


def _ctx_layer(l_idx, x, mod, p, new_caches):
    first_layer = new_caches is None
    if first_layer:
        new_caches = []
        out_shape = [jax.ShapeDtypeStruct((BATCH, DEPTH, SEQ, w), F32) for w in CACHE_WIDTHS]
        out_specs = [pl.BlockSpec((CTX_REQ_PER_TILE, DEPTH, SEQ, w), lambda i, l: (i, 0, 0, 0)) for w in CACHE_WIDTHS]
    else:
        out_shape = [jax.ShapeDtypeStruct(c.shape, c.dtype) for c in new_caches]
        out_specs = [pl.BlockSpec((CTX_REQ_PER_TILE, 1, SEQ, w), lambda i, l: (i, l[0], 0, 0)) for w in CACHE_WIDTHS]
    n_state = len(CACHE_WIDTHS)
    for w in (Q_W, K_W, V_W):
        out_shape.append(jax.ShapeDtypeStruct((N_CTX_TOK, w), BF16))
        out_specs.append(pl.BlockSpec((TM_CTX, w), lambda i, l: (i, 0)))
    for _ in range(2):
        out_shape.append(jax.ShapeDtypeStruct((BATCH, 1, LANES), F32))
        out_specs.append(pl.BlockSpec((CTX_REQ_PER_TILE, 1, LANES), lambda i, l: (i, 0, 0)))
    n_fixed_inputs = 11
    aliases = {n_fixed_inputs + j: j for j in range(len(new_caches))}
    proj_spec = pltpu.PrefetchScalarGridSpec(
        num_scalar_prefetch=1,
        grid=(N_CTX_TOK // TM_CTX,),
        in_specs=[
            pl.BlockSpec((TM_CTX, D_MODEL), lambda i, l: (i, 0)),
            pl.BlockSpec((1, 1, 6, D_MODEL), lambda i, l: (l[0], CTX_MOD_ROW, 0, 0)),
            _layer_spec(p["norm1_g"].shape),
            _layer_spec(p["w_in"].shape),
            _layer_spec(p["mla_q_norm_g"].shape),
            _layer_spec(p["mla_w_qb"].shape),
            _layer_spec(p["mla_kv_norm_g"].shape),
            _layer_spec(p["mla_w_kvb"].shape),
            pl.BlockSpec(p["gq"].shape, lambda i, l: (0, 0)),
            pl.BlockSpec(p["gk"].shape, lambda i, l: (0, 0)),
        ] + [pl.BlockSpec(memory_space=pl.ANY)] * len(new_caches),
        out_specs=out_specs,
    )
    *outs, q, k, v, qmax, kmax = pl.pallas_call(
        functools.partial(_ctx_proj_kernel, first_layer=first_layer),
        grid_spec=proj_spec,
        out_shape=out_shape,
        input_output_aliases=aliases,
        compiler_params=pltpu.CompilerParams(
            dimension_semantics=("arbitrary",), vmem_limit_bytes=VMEM_LIMIT),
        name="ctx_project",
    )(l_idx, x, mod, p["norm1_g"], p["w_in"], p["mla_q_norm_g"], p["mla_w_qb"], p["mla_kv_norm_g"],
      p["mla_w_kvb"], p["gq"], p["gk"], *new_caches)
    assert len(outs) == n_state

    attn_spec = pltpu.PrefetchScalarGridSpec(
        num_scalar_prefetch=1,
        grid=(BATCH,),
        in_specs=[
            pl.BlockSpec((SEQ, Q_W), lambda i, l: (i, 0)),
            pl.BlockSpec((SEQ, K_W), lambda i, l: (i, 0)),
            pl.BlockSpec((SEQ, V_W), lambda i, l: (i, 0)),
            pl.BlockSpec((1, 1, LANES), lambda i, l: (i, 0, 0)),
            pl.BlockSpec((1, 1, LANES), lambda i, l: (i, 0, 0)),
            _layer_spec(p["lamv"].shape),
            _layer_spec(p["subg"].shape),
            pl.BlockSpec(memory_space=pltpu.SMEM),
        ],
        out_specs=pl.BlockSpec((SEQ, D_MODEL), lambda i, l: (i, 0)),
    )
    mix = pl.pallas_call(
        _ctx_attn_kernel,
        grid_spec=attn_spec,
        out_shape=jax.ShapeDtypeStruct((N_CTX_TOK, D_MODEL), BF16),
        compiler_params=pltpu.CompilerParams(
            dimension_semantics=("arbitrary",), vmem_limit_bytes=VMEM_LIMIT),
        name="ctx_attend",
    )(l_idx, q, k, v, qmax, kmax, p["lamv"], p["subg"], p["scal"])
    return (*outs, mix)


def _rope(x, cos, sa, sb, half):
    return x * cos + pltpu.roll(x, LANES - half, 1) * sa + pltpu.roll(x, half, 1) * sb


def _lat_proj_kernel(l_ref, x_ref, mod_ref, n1g_ref, w_in_ref, qg_ref, wqb_ref, kvg_ref, wkvb_ref,
                     c32_ref, a32_ref, b32_ref, c64_ref, a64_ref, b64_ref, cm_ref, am_ref, bm_ref, gq_ref,
                     q_ref, k_ref, v_ref, qmax_ref):
    proj = _project(x_ref, mod_ref, n1g_ref, w_in_ref)
    t32 = (c32_ref[...], a32_ref[...], b32_ref[...], DA_QK // 4)
    t64 = (c64_ref[...], a64_ref[...], b64_ref[...], SW_DIM // 4)
    tml = (cm_ref[...], am_ref[...], bm_ref[...], MLA_ROPE // 4)

    for j in range(2):
        q = _rope(_chunk(proj, j), *t32) * (DA_SCALE * LOG2E)
        q_ref[:, j * LANES:(j + 1) * LANES] = q.astype(BF16)
    for j in range(4):
        q = _rope(_chunk(proj, C_SWQ // LANES + j), *t64) * (SW_SCALE * LOG2E)
        q_ref[:, 256 + j * LANES:256 + (j + 1) * LANES] = q.astype(BF16)
    mq = jnp.dot(_rms(proj[:, C_QA:C_CKV], qg_ref[0]).astype(BF16), wqb_ref[0], preferred_element_type=F32)
    for h in range(MLA_HEADS):
        q = _rope(_chunk(mq, h), *tml) * (MLA_SCALE * LOG2E)
        q_ref[:, O_MQ + h * LANES:O_MQ + (h + 1) * LANES] = q.astype(BF16)
    qmax_ref[0] = _max_group_sq_norm(q_ref[...], gq_ref[...])

    for j in range(2):
        k_ref[:, j * LANES:(j + 1) * LANES] = _rope(_chunk(proj, C_DAK // LANES + j), *t32).astype(BF16)
    k_ref[:, 256:384] = _rope(proj[:, C_SWK:C_SWV], *t64).astype(BF16)
    ckv = _rms(proj[:, C_CKV:C_KR], kvg_ref[0])
    kv = jnp.dot(ckv.astype(BF16), wkvb_ref[0], preferred_element_type=F32)
    krz = _rope(proj[:, C_KR:P_EXT], *tml)
    for h in range(MLA_HEADS):
        k_ref[:, O_MK + h * LANES:O_MK + (h + 1) * LANES] = (_chunk(kv, h) + krz).astype(BF16)

    v_ref[:, 0:256] = proj[:, C_DAV:C_SWQ].astype(BF16)
    v_ref[:, 256:384] = proj[:, C_SWV:C_QA].astype(BF16)
    v_ref[:, 384:640] = kv[:, MLA_CHUNKS:].astype(BF16)


def _lat_project(l_idx, x, mod, p, tables):
    tiles_per_req = DEC_SEQ // TM_LAT
    tab_spec = pl.BlockSpec((TM_LAT, LANES), lambda i, l: (i % tiles_per_req, 0))
    grid_spec = pltpu.PrefetchScalarGridSpec(
        num_scalar_prefetch=1,
        grid=(N_LAT_TOK // TM_LAT,),
        in_specs=[
            pl.BlockSpec((TM_LAT, D_MODEL), lambda i, l: (i, 0)),
            pl.BlockSpec((1, 1, 6, D_MODEL), lambda i, l: (l[0], i // tiles_per_req, 0, 0)),
            _layer_spec(p["norm1_g"].shape),
            _layer_spec(p["w_in"].shape),
            _layer_spec(p["mla_q_norm_g"].shape),
            _layer_spec(p["mla_w_qb"].shape),
            _layer_spec(p["mla_kv_norm_g"].shape),
            _layer_spec(p["mla_w_kvb"].shape),
        ] + [tab_spec] * len(tables) + [pl.BlockSpec(p["gq"].shape, lambda i, l: (0, 0))],
        out_specs=[
            pl.BlockSpec((TM_LAT, Q_W), lambda i, l: (i, 0)),
            pl.BlockSpec((TM_LAT, K_W), lambda i, l: (i, 0)),
            pl.BlockSpec((TM_LAT, V_W), lambda i, l: (i, 0)),
            pl.BlockSpec((1, 1, LANES), lambda i, l: (i, 0, 0)),
        ],
    )
    return pl.pallas_call(
        _lat_proj_kernel,
        grid_spec=grid_spec,
        out_shape=[jax.ShapeDtypeStruct((N_LAT_TOK, Q_W), BF16),
                   jax.ShapeDtypeStruct((N_LAT_TOK, K_W), BF16),
                   jax.ShapeDtypeStruct((N_LAT_TOK, V_W), BF16),
                   jax.ShapeDtypeStruct((N_LAT_TOK // TM_LAT, 1, LANES), F32)],
        compiler_params=pltpu.CompilerParams(
            dimension_semantics=("arbitrary",), vmem_limit_bytes=VMEM_LIMIT),
        name="latent_project",
    )(l_idx, x, mod, p["norm1_g"], p["w_in"], p["mla_q_norm_g"], p["mla_w_qb"], p["mla_kv_norm_g"],
      p["mla_w_kvb"], *tables, p["gq"])


def _lat_attn_kernel(l_ref, q_ref, k_ref, v_ref, cdk_ref, cdv_ref, csk_ref, csv_ref, cckv_ref, ckr_ref,
                     wkvb_ref, pad_ref, lamv_ref, subg_ref, qmax_ref, gk_ref, scal_ref,
                     mix_ref,
                     kda, vda, ksw, vsw, kml, vml, kmax_sq):
    l = l_ref[0]
    qb = pl.program_id(1)

    @pl.when(qb == 0)
    def _stage_keys():
        kda[0:PAST_LEN, :] = cdk_ref[0, 0].astype(BF16)
        kda[PAST_LEN:N_KEYS, :] = k_ref[0, :, 0:256]
        vda[0:PAST_LEN, :] = cdv_ref[0, 0].astype(BF16)
        vda[PAST_LEN:N_KEYS, :] = v_ref[0, :, 0:256]
        sk_lat = k_ref[0, :, 256:384].astype(F32)
        sv_lat = v_ref[0, :, 256:384].astype(F32)
        for g in range(SW_KV_HEADS):
            ksw[0:PAST_LEN, g * LANES:(g + 1) * LANES] = _both_halves(csk_ref[0, 0], g).astype(BF16)
            ksw[PAST_LEN:N_KEYS, g * LANES:(g + 1) * LANES] = _both_halves(sk_lat, g).astype(BF16)
            vsw[0:PAST_LEN, 2 * g * LANES:(2 * g + 1) * LANES] = _both_halves(csv_ref[0, 0], g).astype(BF16)
            vsw[PAST_LEN:N_KEYS, 2 * g * LANES:(2 * g + 1) * LANES] = _both_halves(sv_lat, g).astype(BF16)
            vsw[:, (2 * g + 1) * LANES:(2 * g + 2) * LANES] = jnp.ones((N_KEYS, LANES), BF16)
        kvc = jnp.dot(cckv_ref[0, 0].astype(BF16), wkvb_ref[0], preferred_element_type=F32)
        krz = jnp.dot(ckr_ref[0, 0].astype(BF16), pad_ref[...], preferred_element_type=F32)
        mk_ctx = [_chunk(kvc, h) + krz for h in range(MLA_HEADS)]
        for h in range(MLA_HEADS):
            kml[0:PAST_LEN, h * LANES:(h + 1) * LANES] = mk_ctx[h].astype(BF16)
            v_ctx = _chunk(kvc, MLA_HEADS + h // 2)
            v_lat = v_ref[0, :, 384 + (h // 2) * LANES:384 + (h // 2 + 1) * LANES].astype(F32)
            vml[0:PAST_LEN, h * LANES:(h + 1) * LANES] = _ones_in_other_half(v_ctx, h % 2).astype(BF16)
            vml[PAST_LEN:N_KEYS, h * LANES:(h + 1) * LANES] = _ones_in_other_half(v_lat, h % 2).astype(BF16)
        kml[PAST_LEN:N_KEYS, :] = k_ref[0, :, O_MK:K_W]
        gk = gk_ref[...]
        k_ctx = jnp.concatenate([cdk_ref[0, 0], csk_ref[0, 0]] + mk_ctx, axis=-1)
        kmax_sq[...] = jnp.maximum(_max_group_sq_norm(k_ctx, gk), _max_group_sq_norm(k_ref[0], gk))

    bound_sq = qmax_ref[0] * kmax_sq[...] * BOUND_MARGIN
    small = (jnp.max(bound_sq) <= NOSHIFT_BOUND ** 2) & (scal_ref[l, S_SINK_ABSMAX] <= NOSHIFT_BOUND)

    lam = _diff_lambda(lamv_ref, scal_ref[l, S_LAM_INIT])
    subg = subg_ref[0]
    sinks = [scal_ref[l, h] for h in range(SW_HEADS)]
    q0 = qb * TQ
    start = pl.multiple_of(jnp.clip(q0 - WINDOW, 0, DEC_SEQ - SW_BAND), LANES)

    def attend(shift):
        da_o = _diff_attention(q_ref[0, :, 0:256].astype(F32), lambda: kda[...],
                               lambda c: vda[:, c * LANES:(c + 1) * LANES], lam, shift)
        for c, o in enumerate(da_o):
            mix_ref[0, :, c * LANES:(c + 1) * LANES] = _diff_subln(
                o, _chunk(subg, c), scal_ref[l, S_ONE_MINUS]).astype(BF16)

        band = pl.ds(PAST_LEN + start, SW_BAND)
        k_sel = jnp.concatenate([ksw[0:PAST_LEN, :], ksw[band, :]], axis=0)
        v_sel = jnp.concatenate([vsw[0:PAST_LEN, :], vsw[band, :]], axis=0)
        n_sel = PAST_LEN + SW_BAND
        col = lax.broadcasted_iota(jnp.int32, (TQ, n_sel), 1)
        row = lax.broadcasted_iota(jnp.int32, (TQ, n_sel), 0)
        dist = (q0 + row) - (start + col - PAST_LEN)
        valid = (col < PAST_LEN) | ((dist <= WINDOW) & (dist >= -WINDOW))
        sw_o = _swa_attention(q_ref[0, :, 256:768].astype(F32), lambda g: _chunk(k_sel, g),
                              lambda g: v_sel[:, 2 * g * LANES:(2 * g + 2) * LANES], sinks, shift, valid)
        for c, o in enumerate(sw_o):
            mix_ref[0, :, 256 + c * LANES:256 + (c + 1) * LANES] = o.astype(BF16)

        mla_o = _mla_attention(lambda h: q_ref[0, :, O_MQ + h * LANES:O_MQ + (h + 1) * LANES],
                               lambda h: kml[:, h * LANES:(h + 1) * LANES],
                               lambda h: vml[:, h * LANES:(h + 1) * LANES], shift)
        for c, o in enumerate(mla_o):
            mix_ref[0, :, 768 + c * LANES:768 + (c + 1) * LANES] = o.astype(BF16)

    pl.when(small)(functools.partial(attend, False))
    pl.when(jnp.logical_not(small))(functools.partial(attend, True))


def _lat_attend(l_idx, q, k, v, qmax, caches, p, pad_eye):
    cdk, cdv, csk, csv, cckv, ckr = caches
    proj_tiles_per_req = DEC_SEQ // TM_LAT

    def cache_spec(a):
        return pl.BlockSpec((1, 1) + a.shape[2:], lambda b, j, l: (b, l[0], 0, 0))

    grid_spec = pltpu.PrefetchScalarGridSpec(
        num_scalar_prefetch=1,
        grid=(DEC_BATCH, DEC_SEQ // TQ),
        in_specs=[
            pl.BlockSpec((1, TQ, Q_W), lambda b, j, l: (b, j, 0)),
            pl.BlockSpec((1, DEC_SEQ, K_W), lambda b, j, l: (b, 0, 0)),
            pl.BlockSpec((1, DEC_SEQ, V_W), lambda b, j, l: (b, 0, 0)),
            cache_spec(cdk), cache_spec(cdv), cache_spec(csk), cache_spec(csv), cache_spec(cckv), cache_spec(ckr),
            _layer_spec(p["mla_w_kvb"].shape),
            pl.BlockSpec(pad_eye.shape, lambda b, j, l: (0, 0)),
            _layer_spec(p["lamv"].shape),
            _layer_spec(p["subg"].shape),
            pl.BlockSpec((1, 1, LANES), lambda b, j, l: (b * proj_tiles_per_req + (j * TQ) // TM_LAT, 0, 0)),
            pl.BlockSpec(p["gk"].shape, lambda b, j, l: (0, 0)),
            pl.BlockSpec(memory_space=pltpu.SMEM),
        ],
        out_specs=pl.BlockSpec((1, TQ, D_MODEL), lambda b, j, l: (b, j, 0)),
        scratch_shapes=[
            pltpu.VMEM((N_KEYS, 256), BF16), pltpu.VMEM((N_KEYS, 256), BF16),
            pltpu.VMEM((N_KEYS, SW_KV_HEADS * LANES), BF16), pltpu.VMEM((N_KEYS, SW_KV_HEADS * 2 * LANES), BF16),
            pltpu.VMEM((N_KEYS, MLA_CHUNKS), BF16), pltpu.VMEM((N_KEYS, MLA_CHUNKS), BF16),
            pltpu.VMEM((1, LANES), F32),
        ],
    )
    return pl.pallas_call(
        _lat_attn_kernel,
        grid_spec=grid_spec,
        out_shape=jax.ShapeDtypeStruct((DEC_BATCH, DEC_SEQ, D_MODEL), BF16),
        compiler_params=pltpu.CompilerParams(
            dimension_semantics=("arbitrary", "arbitrary"), vmem_limit_bytes=VMEM_LIMIT),
        name="latent_attend",
    )(l_idx, q.reshape(DEC_BATCH, DEC_SEQ, Q_W), k.reshape(DEC_BATCH, DEC_SEQ, K_W),
      v.reshape(DEC_BATCH, DEC_SEQ, V_W), cdk, cdv, csk, csv, cckv, ckr,
      p["mla_w_kvb"], pad_eye, p["lamv"], p["subg"], qmax, p["gk"], p["scal"])


def _mlp_kernel(l_ref, x_ref, mix_ref, mod_ref, n2g_ref, w_out_ref, w_up_ref, w_down_ref, fg_ref, o_ref,
                *, final_norm):
    m = mod_ref[0, 0]
    x = x_ref[...] + m[2:3] * jnp.dot(mix_ref[...], w_out_ref[0], preferred_element_type=F32)
    h = (_rms(x, n2g_ref[0]) * (1.0 + m[4:5]) + m[3:4]).astype(BF16)
    acc = jnp.zeros_like(x)
    for c in range(D_FF // FF_CHUNK):
        u = jnp.dot(h, w_up_ref[0, :, c * FF_CHUNK:(c + 1) * FF_CHUNK], preferred_element_type=F32)
        a = jnp.square(jnp.maximum(u, 0.0)).astype(BF16)
        acc = acc + jnp.dot(a, w_down_ref[0, c * FF_CHUNK:(c + 1) * FF_CHUNK, :], preferred_element_type=F32)
    x = x + m[5:6] * acc
    if final_norm:
        x = _rms(x, fg_ref[...])
    o_ref[...] = x


def _mlp_layer(l_idx, x, mix, mod, p, final_g, mod_row, final_norm):
    n_tok = x.shape[0]
    grid_spec = pltpu.PrefetchScalarGridSpec(
        num_scalar_prefetch=1,
        grid=(n_tok // TM_MLP,),
        in_specs=[
            pl.BlockSpec((TM_MLP, D_MODEL), lambda i, l: (i, 0)),
            pl.BlockSpec((TM_MLP, D_MODEL), lambda i, l: (i, 0)),
            pl.BlockSpec((1, 1, 6, D_MODEL), lambda i, l: (l[0], mod_row(i), 0, 0)),
            _layer_spec(p["norm2_g"].shape),
            _layer_spec(p["w_out"].shape),
            _layer_spec(p["w_up"].shape),
            _layer_spec(p["w_down"].shape),
            pl.BlockSpec((1, D_MODEL), lambda i, l: (0, 0)),
        ],
        out_specs=pl.BlockSpec((TM_MLP, D_MODEL), lambda i, l: (i, 0)),
    )
    return pl.pallas_call(
        functools.partial(_mlp_kernel, final_norm=final_norm),
        grid_spec=grid_spec,
        out_shape=jax.ShapeDtypeStruct((n_tok, D_MODEL), F32),
        compiler_params=pltpu.CompilerParams(
            dimension_semantics=("arbitrary",), vmem_limit_bytes=VMEM_LIMIT),
        name="out_proj_mlp",
    )(l_idx, x, mix, mod, p["norm2_g"], p["w_out"], p["w_up"], p["w_down"], final_g)


def _rope_tables(period, active_lanes=LANES):
    m = period // 2
    half = m // 2
    t = np.arange(DEC_SEQ)
    pos = np.stack([t // GRID_W, t % GRID_W], axis=0).astype(np.float64)
    lane = np.arange(LANES) % period
    axis = lane // m
    k = lane % m
    freq = ROPE_BASE ** (-(k % half).astype(np.float64) / half)
    ang = pos[axis, :].T * freq[None, :]
    first = (k < half)[None, :]
    active = (np.arange(LANES) < active_lanes)[None, :]
    cos, sin = np.where(active, np.cos(ang), 1.0), np.where(active, np.sin(ang), 0.0)
    return (jnp.asarray(cos, F32), jnp.asarray(np.where(first, -sin, 0.0), F32),
            jnp.asarray(np.where(first, 0.0, sin), F32))


def _group_matrices():
    gq = np.zeros((Q_W, LANES), np.float32)
    gk = np.zeros((K_W, LANES), np.float32)
    for d in range(DA_HEADS * 2 * DA_QK):
        gq[d, G_DIFF + d // DA_QK] = 1.0
        gk[d, G_DIFF + d // DA_QK] = 1.0
    for h in range(SW_HEADS):
        g = h // (SW_HEADS // SW_KV_HEADS)
        gq[256 + h * SW_DIM:256 + (h + 1) * SW_DIM, G_SWA + h] = 1.0
        gk[256 + g * SW_DIM:256 + (g + 1) * SW_DIM, G_SWA + h] = 1.0
    for h in range(MLA_HEADS):
        gq[O_MQ + h * LANES:O_MQ + (h + 1) * LANES, G_MLA + h] = 1.0
        gk[O_MK + h * LANES:O_MK + (h + 1) * LANES, G_MLA + h] = 1.0
    return jnp.asarray(gq, BF16), jnp.asarray(gk, BF16)


def _prepare_params(norm1_g, norm2_g, w_in, w_out, lq1, lk1, lq2, lk2, diff_subln_g, swa_sink,
                    mla_q_norm_g, mla_w_qb, mla_kv_norm_g, mla_w_kvb, w_up, w_down):
    w_in_ext = jnp.pad(w_in.astype(BF16), ((0, 0), (0, 0), (0, P_EXT - w_in.shape[-1])))
    w_out_p = w_out.astype(BF16)
    dq = MLA_NOPE + MLA_ROPE
    gap = LANES - dq
    zq = jnp.zeros((DEPTH, MLA_Q_RANK, gap), mla_w_qb.dtype)
    wqb = jnp.concatenate(
        [blk for h in range(MLA_HEADS)
         for blk in (mla_w_qb[:, :, h * dq + MLA_NOPE:(h + 1) * dq], zq, mla_w_qb[:, :, h * dq:h * dq + MLA_NOPE])],
        axis=-1).astype(BF16)
    dkv = MLA_NOPE + MLA_V
    zk = jnp.zeros((DEPTH, MLA_KV_RANK, LANES - MLA_NOPE), mla_w_kvb.dtype)
    wkvb = jnp.concatenate(
        [blk for h in range(MLA_HEADS) for blk in (zk, mla_w_kvb[:, :, h * dkv:h * dkv + MLA_NOPE])]
        + [mla_w_kvb[:, :, h * dkv + MLA_NOPE:(h + 1) * dkv] for h in range(MLA_HEADS)],
        axis=-1).astype(BF16)
    lam_init = np.array([0.8 - 0.6 * math.exp(-0.3 * l) for l in range(DEPTH)], np.float32)
    sink_l2 = swa_sink.astype(F32) * LOG2E
    scal = jnp.concatenate([
        sink_l2,
        jnp.asarray(np.stack([lam_init, (1.0 - lam_init.astype(np.float64)).astype(np.float32)], axis=1)),
        jnp.max(jnp.abs(sink_l2), axis=1, keepdims=True),
        jnp.zeros((DEPTH, S_COLS - SW_HEADS - 3), F32)], axis=1)
    gq, gk = _group_matrices()
    return {
        "gq": gq,
        "gk": gk,
        "norm1_g": norm1_g.reshape(DEPTH, 1, D_MODEL),
        "norm2_g": norm2_g.reshape(DEPTH, 1, D_MODEL),
        "w_in": w_in_ext,
        "w_out": w_out_p,
        "mla_q_norm_g": mla_q_norm_g.reshape(DEPTH, 1, MLA_Q_RANK),
        "mla_w_qb": wqb,
        "mla_kv_norm_g": mla_kv_norm_g.reshape(DEPTH, 1, MLA_KV_RANK),
        "mla_w_kvb": wkvb,
        "lamv": jnp.stack([lq1, lk1, lq2, lk2], axis=1),
        "subg": jnp.tile(diff_subln_g, (1, DA_HEADS)).reshape(DEPTH, 1, DA_HEADS * DA_V),
        "scal": scal,
        "w_up": w_up.astype(BF16),
        "w_down": w_down.astype(BF16),
    }


def kernel(x_prompt, x_sample, c, cache_diff_k, cache_diff_v, cache_swa_k, cache_swa_v, cache_mla_ckv,
           cache_mla_krope, c_ctx, norm1_g, norm2_g, w_ada, b_ada, w_in, w_out, diff_lambda_q1,
           diff_lambda_k1, diff_lambda_q2, diff_lambda_k2, diff_subln_g, swa_sink, mla_q_norm_g,
           mla_w_qb, mla_kv_norm_g, mla_w_kvb, w_up, w_down, final_g):
    p = _prepare_params(norm1_g, norm2_g, w_in, w_out, diff_lambda_q1, diff_lambda_k1, diff_lambda_q2,
                        diff_lambda_k2, diff_subln_g, swa_sink, mla_q_norm_g, mla_w_qb, mla_kv_norm_g,
                        mla_w_kvb, w_up, w_down)
    cvecs = jnp.concatenate([c, c_ctx[None, :], jnp.zeros((8 - DEC_BATCH - 1, D_MODEL), F32)], axis=0)
    mod = _modulation(cvecs, w_ada, b_ada)
    tables = _rope_tables(DA_QK) + _rope_tables(SW_DIM) + _rope_tables(MLA_ROPE, MLA_ROPE)
    pad_eye = jnp.asarray(np.eye(MLA_ROPE, LANES, dtype=np.float32), BF16)
    final_g2 = final_g.reshape(1, D_MODEL)
    caches = (cache_diff_k.reshape(DEC_BATCH, DEPTH, PAST_LEN, DA_HEADS * 2 * DA_QK),
              cache_diff_v.reshape(DEC_BATCH, DEPTH, PAST_LEN, DA_HEADS * DA_V),
              cache_swa_k.reshape(DEC_BATCH, DEPTH, PAST_LEN, SW_KV_HEADS * SW_DIM),
              cache_swa_v.reshape(DEC_BATCH, DEPTH, PAST_LEN, SW_KV_HEADS * SW_DIM),
              cache_mla_ckv, cache_mla_krope)

    x_ctx = x_prompt.reshape(N_CTX_TOK, D_MODEL)
    x_lat = x_sample.reshape(N_LAT_TOK, D_MODEL)
    new = None
    lat_tiles_per_req = DEC_SEQ // TM_MLP
    for l in range(DEPTH):
        l_idx = jnp.full((1,), l, jnp.int32)
        last = l == DEPTH - 1
        *new, mix_ctx = _ctx_layer(l_idx, x_ctx, mod, p, new)
        x_ctx = _mlp_layer(l_idx, x_ctx, mix_ctx, mod, p, final_g2, lambda i: CTX_MOD_ROW, last)
        q, k, v, qmax = _lat_project(l_idx, x_lat, mod, p, tables)
        mix_lat = _lat_attend(l_idx, q, k, v, qmax, caches, p, pad_eye).reshape(N_LAT_TOK, D_MODEL)
        x_lat = _mlp_layer(l_idx, x_lat, mix_lat, mod, p, final_g2, lambda i: i // lat_tiles_per_req, last)

    y_prompt = x_ctx.reshape(BATCH, SEQ, D_MODEL)
    y_sample = x_lat.reshape(DEC_BATCH, DEC_SEQ, D_MODEL)
    new_dk, new_dv, new_sk, new_sv, new_ckv, new_kr = new
    return (y_prompt, y_sample,
            new_dk.reshape(BATCH, DEPTH, SEQ, DA_HEADS, 2 * DA_QK),
            new_dv.reshape(BATCH, DEPTH, SEQ, DA_HEADS, DA_V),
            new_sk.reshape(BATCH, DEPTH, SEQ, SW_KV_HEADS, SW_DIM),
            new_sv.reshape(BATCH, DEPTH, SEQ, SW_KV_HEADS, SW_DIM),
            new_ckv, new_kr)
```

```python
import functools
import math

import numpy as np
import jax
import jax.numpy as jnp
from jax import lax
from jax.experimental import pallas as pl
from jax.experimental.pallas import tpu as pltpu

F32 = jnp.float32
BF16 = jnp.bfloat16

D_MODEL = 1024
BATCH = 32
SEQ = 256
DEPTH = 4
DEC_BATCH = 4
DEC_SEQ = 2048
PAST_LEN = 256
GRID_W = 64
WINDOW = 128
ROPE_BASE = 10000.0
EPS = 1e-6
DA_HEADS = 4
DA_QK = 32
DA_V = 64
SW_HEADS = 8
SW_KV_HEADS = 2
SW_DIM = 64
MLA_HEADS = 4
MLA_Q_RANK = 256
MLA_KV_RANK = 128
MLA_NOPE = 64
MLA_ROPE = 32
MLA_V = 64
D_FF = 4 * D_MODEL

LANES = 128
HALF = LANES // 2
N_CTX_TOK = BATCH * SEQ
N_LAT_TOK = DEC_BATCH * DEC_SEQ
N_KEYS = PAST_LEN + DEC_SEQ
CTX_MOD_ROW = DEC_BATCH

C_DAQ, C_DAK, C_DAV, C_SWQ, C_SWK, C_SWV, C_QA, C_CKV, C_KR, P_EXT = (
    0, 256, 512, 768, 1280, 1408, 1536, 1792, 1920, 2048)
MLA_CHUNKS = MLA_HEADS * 128
Q_W = 256 + 512 + MLA_CHUNKS
K_W = 256 + 128 + MLA_CHUNKS
V_W = 256 + 128 + 256
O_MQ = 768
O_MK = 384
LOG2E = 1.4426950408889634

DA_SCALE = DA_QK ** -0.5
SW_SCALE = SW_DIM ** -0.5
MLA_SCALE = (MLA_NOPE + MLA_ROPE) ** -0.5

CTX_REQ_PER_TILE = 2
TM_CTX = CTX_REQ_PER_TILE * SEQ
TM_LAT = 512
TM_MLP = 512
TQ = 256
SW_BAND = TQ + 2 * WINDOW
FF_CHUNK = 1024
MOD_TILE = 1536
VMEM_LIMIT = 56 * 1024 * 1024

_NT = (((1,), (1,)), ((), ()))

S_LAM_INIT = SW_HEADS
S_ONE_MINUS = SW_HEADS + 1
S_SINK_ABSMAX = SW_HEADS + 2
S_COLS = 16

NOSHIFT_BOUND = 50.0
G_DIFF, G_SWA, G_MLA = 0, 8, 16


def _scores(q, k):
    return lax.dot_general(q, k, _NT, preferred_element_type=F32)


def _rms(x, g):
    ms = jnp.mean(x * x, axis=-1, keepdims=True)
    return x * lax.rsqrt(ms + EPS) * g


def _lane_mask(n, lo, width):
    lane = lax.broadcasted_iota(jnp.int32, (1, n), 1)
    return (lane >= lo) & (lane < lo + width)


def _exp_scores(s, shift, sink=None):
    if not shift:
        return jnp.exp2(s), jnp.zeros((1, 1), F32)
    m = jnp.max(s, axis=-1, keepdims=True)
    if sink is not None:
        m = jnp.maximum(m, sink)
    return jnp.exp2(s - m), m


def _sq_bf16(x):
    xf = x.astype(F32)
    return (xf * xf).astype(BF16)


def _max_group_sq_norm(x, g):
    return jnp.max(jnp.dot(_sq_bf16(x), g, preferred_element_type=F32), axis=0, keepdims=True)


BOUND_MARGIN = 1.03


def _chunk(a, j):
    return a[:, j * LANES:(j + 1) * LANES]


def _project(x_ref, mod_ref, n1g_ref, w_in_ref):
    m = mod_ref[0, 0]
    h = _rms(x_ref[...], n1g_ref[0]) * (1.0 + m[1:2]) + m[0:1]
    return jnp.dot(h.astype(BF16), w_in_ref[0], preferred_element_type=F32)


def _diff_lambda(lamv_ref, lam_init):
    lv = lamv_ref[0]
    a = jnp.sum(lv[0:1] * lv[1:2], axis=-1, keepdims=True)
    b = jnp.sum(lv[2:3] * lv[3:4], axis=-1, keepdims=True)
    return jnp.exp(a) - jnp.exp(b) + lam_init


def _half_mask(par):
    lane = lax.broadcasted_iota(jnp.int32, (1, LANES), 1)
    return (lane >= par * HALF) & (lane < (par + 1) * HALF)


def _diff_attention(qf, load_k, load_v, lam, shift):
    width = DA_HEADS * DA_V
    chunks = []
    for c in range(DA_HEADS // 2):
        o_par = []
        for par in range(2):
            lo = (2 * c + par) * DA_V
            q1 = jnp.where(_lane_mask(width, lo, DA_QK), qf, 0.0).astype(BF16)
            q2 = jnp.where(_lane_mask(width, lo + DA_QK, DA_QK), qf, 0.0).astype(BF16)
            e1, _ = _exp_scores(_scores(q1, load_k()), shift)
            e2, _ = _exp_scores(_scores(q2, load_k()), shift)
            d1 = jnp.sum(e1, axis=-1, keepdims=True)
            d2 = jnp.sum(e2, axis=-1, keepdims=True)
            w = (e1 - e2 * (lam * d1 * (1.0 / d2))).astype(BF16)
            o_par.append(jnp.dot(w, load_v(c), preferred_element_type=F32) * (1.0 / d1))
        chunks.append(jnp.where(_half_mask(0), o_par[0], o_par[1]))
    return chunks


def _diff_subln(o, subg, one_minus_lam_init):
    sq = o * o
    r = None
    for par in range(2):
        ms = jnp.sum(jnp.where(_half_mask(par), sq, 0.0), axis=-1, keepdims=True) * (1.0 / DA_V)
        rs = lax.rsqrt(ms + EPS)
        r = rs if r is None else jnp.where(_half_mask(0), r, rs)
    return o * r * subg * one_minus_lam_init


def _both_halves(x, g):
    swapped = pltpu.roll(x, HALF, 1)
    return jnp.where(_half_mask(g), x, swapped)


def _swa_attention(qf, load_k, load_v, sinks, shift, valid=None):
    heads_per_group = SW_HEADS // SW_KV_HEADS
    outs = []
    for c in range(SW_HEADS // 2):
        qc = _chunk(qf, c)
        g = (2 * c) // heads_per_group
        o_par = []
        for par in range(2):
            h = 2 * c + par
            qm = jnp.where(_half_mask(par), qc, 0.0).astype(BF16)
            s = _scores(qm, load_k(g))
            if valid is not None:
                s = jnp.where(valid, s, -1e30)
            e, m = _exp_scores(s, shift, sink=sinks[h])
            o = jnp.dot(e.astype(BF16), load_v(g), preferred_element_type=F32)
            d = _chunk(o, 1) + jnp.exp2(sinks[h] - m)
            o_par.append(_chunk(o, 0) * (1.0 / d))
        outs.append(jnp.where(_half_mask(0), o_par[0], o_par[1]))
    return outs


def _mla_attention(load_q, load_k, load_v, shift):
    chunks = []
    for c in range(MLA_HEADS // 2):
        o_par = []
        for par in range(2):
            h = 2 * c + par
            e, _ = _exp_scores(_scores(load_q(h), load_k(h)), shift)
            o = jnp.dot(e.astype(BF16), load_v(h), preferred_element_type=F32)
            o_par.append(o * (1.0 / pltpu.roll(o, HALF, 1)))
        chunks.append(jnp.where(_half_mask(0), o_par[0], o_par[1]))
    return chunks


def _ones_in_other_half(v, par):
    return jnp.where(_half_mask(par), v, 1.0)


def _mod_kernel(c_ref, w_ref, b_ref, o_ref):
    c = c_ref[...]
    s = c * (1.0 / (1.0 + jnp.exp(-c)))
    o_ref[0] = jnp.dot(s.astype(BF16), w_ref[0].astype(BF16), preferred_element_type=F32) + b_ref[0]


def _modulation(cvecs, w_ada, b_ada):
    n_out = 6 * D_MODEL
    out = pl.pallas_call(
        _mod_kernel,
        grid=(DEPTH, n_out // MOD_TILE),
        in_specs=[
            pl.BlockSpec((8, D_MODEL), lambda l, j: (0, 0)),
            pl.BlockSpec((1, D_MODEL, MOD_TILE), lambda l, j: (l, 0, j)),
            pl.BlockSpec((1, 1, MOD_TILE), lambda l, j: (l, 0, j)),
        ],
        out_specs=pl.BlockSpec((1, 8, MOD_TILE), lambda l, j: (l, 0, j)),
        out_shape=jax.ShapeDtypeStruct((DEPTH, 8, n_out), F32),
        compiler_params=pltpu.CompilerParams(
            dimension_semantics=("arbitrary", "arbitrary"), vmem_limit_bytes=VMEM_LIMIT),
        name="adaln_modulation",
    )(cvecs, w_ada, b_ada.reshape(DEPTH, 1, n_out))
    return out.reshape(DEPTH, 8, 6, D_MODEL)


def _ctx_proj_kernel(l_ref, x_ref, mod_ref, n1g_ref, w_in_ref, qg_ref, wqb_ref, kvg_ref, wkvb_ref, gq_ref, gk_ref,
                     *rest, first_layer):
    dk_ref, dv_ref, sk_ref, sv_ref, ckv_ref, kr_ref, q_ref, k_ref, v_ref, qmax_ref, kmax_ref = rest[-11:]
    if first_layer:
        for ref in (dk_ref, dv_ref, sk_ref, sv_ref, ckv_ref, kr_ref):
            ref[:, 1:] = jnp.zeros((CTX_REQ_PER_TILE, DEPTH - 1) + ref.shape[2:], F32)
    proj = _project(x_ref, mod_ref, n1g_ref, w_in_ref)
    ckv = _rms(proj[:, C_CKV:C_KR], kvg_ref[0])
    mq = jnp.dot(_rms(proj[:, C_QA:C_CKV], qg_ref[0]).astype(BF16), wqb_ref[0], preferred_element_type=F32)
    kv = jnp.dot(ckv.astype(BF16), wkvb_ref[0], preferred_element_type=F32)
    krz = proj[:, C_KR:P_EXT]
    for r in range(CTX_REQ_PER_TILE):
        r0, r1 = r * SEQ, (r + 1) * SEQ
        dk_ref[r, 0] = proj[r0:r1, C_DAK:C_DAV]
        dv_ref[r, 0] = proj[r0:r1, C_DAV:C_SWQ]
        sk_ref[r, 0] = proj[r0:r1, C_SWK:C_SWV]
        sv_ref[r, 0] = proj[r0:r1, C_SWV:C_QA]
        ckv_ref[r, 0] = ckv[r0:r1]
        kr_ref[r, 0] = krz[r0:r1, 0:MLA_ROPE]

    q_ref[:, 0:256] = (proj[:, C_DAQ:C_DAK] * (DA_SCALE * LOG2E)).astype(BF16)
    q_ref[:, 256:O_MQ] = (proj[:, C_SWQ:C_SWK] * (SW_SCALE * LOG2E)).astype(BF16)
    q_ref[:, O_MQ:Q_W] = (mq * (MLA_SCALE * LOG2E)).astype(BF16)
    k_ref[:, 0:256] = proj[:, C_DAK:C_DAV].astype(BF16)
    k_ref[:, 256:O_MK] = proj[:, C_SWK:C_SWV].astype(BF16)
    for h in range(MLA_HEADS):
        k_ref[:, O_MK + h * LANES:O_MK + (h + 1) * LANES] = (_chunk(kv, h) + krz).astype(BF16)
    v_ref[:, 0:256] = proj[:, C_DAV:C_SWQ].astype(BF16)
    v_ref[:, 256:384] = proj[:, C_SWV:C_QA].astype(BF16)
    v_ref[:, 384:V_W] = kv[:, MLA_CHUNKS:].astype(BF16)
    for r in range(CTX_REQ_PER_TILE):
        r0, r1 = r * SEQ, (r + 1) * SEQ
        qmax_ref[r] = _max_group_sq_norm(q_ref[r0:r1, :], gq_ref[...])
        kmax_ref[r] = _max_group_sq_norm(k_ref[r0:r1, :], gk_ref[...])


def _ctx_attn_kernel(l_ref, q_ref, k_ref, v_ref, qmax_ref, kmax_ref, lamv_ref, subg_ref, scal_ref, mix_ref):
    l = l_ref[0]
    bound_sq = qmax_ref[0] * kmax_ref[0] * BOUND_MARGIN
    small = (jnp.max(bound_sq) <= NOSHIFT_BOUND ** 2) & (scal_ref[l, S_SINK_ABSMAX] <= NOSHIFT_BOUND)
    lam = _diff_lambda(lamv_ref, scal_ref[l, S_LAM_INIT])
    subg = subg_ref[0]
    sinks = [scal_ref[l, h] for h in range(SW_HEADS)]

    def attend(shift):
        da_o = _diff_attention(q_ref[:, 0:256].astype(F32), lambda: k_ref[:, 0:256],
                               lambda c: v_ref[:, c * LANES:(c + 1) * LANES], lam, shift)
        for c, o in enumerate(da_o):
            mix_ref[:, c * LANES:(c + 1) * LANES] = _diff_subln(
                o, _chunk(subg, c), scal_ref[l, S_ONE_MINUS]).astype(BF16)
        sw_k = k_ref[:, 256:O_MK].astype(F32)
        sw_v = v_ref[:, 256:384].astype(F32)
        sw_kg = [_both_halves(sw_k, g).astype(BF16) for g in range(SW_KV_HEADS)]
        sw_vg = [jnp.concatenate([_both_halves(sw_v, g), jnp.ones_like(sw_v)], axis=-1).astype(BF16)
                 for g in range(SW_KV_HEADS)]
        sw_o = _swa_attention(q_ref[:, 256:O_MQ].astype(F32), lambda g: sw_kg[g], lambda g: sw_vg[g], sinks, shift)
        for c, o in enumerate(sw_o):
            mix_ref[:, 256 + c * LANES:256 + (c + 1) * LANES] = o.astype(BF16)
        mla_o = _mla_attention(
            lambda h: q_ref[:, O_MQ + h * LANES:O_MQ + (h + 1) * LANES],
            lambda h: k_ref[:, O_MK + h * LANES:O_MK + (h + 1) * LANES],
            lambda h: _ones_in_other_half(
                v_ref[:, 384 + (h // 2) * LANES:384 + (h // 2 + 1) * LANES].astype(F32), h % 2).astype(BF16),
            shift)
        for c, o in enumerate(mla_o):
            mix_ref[:, 768 + c * LANES:768 + (c + 1) * LANES] = o.astype(BF16)

    pl.when(small)(functools.partial(attend, False))
    pl.when(jnp.logical_not(small))(functools.partial(attend, True))


def _layer_spec(shape):
    nd = len(shape)
    return pl.BlockSpec((1,) + tuple(shape[1:]), lambda *a: (a[-1][0],) + (0,) * (nd - 1))


CACHE_WIDTHS = (DA_HEADS * 2 * DA_QK, DA_HEADS * DA_V, SW_KV_HEADS * SW_DIM, SW_KV_HEADS * SW_DIM,
                MLA_KV_RANK, MLA_ROPE)


def _ctx_layer(l_idx, x, mod, p, new_caches):
    first_layer = new_caches is None
    if first_layer:
        new_caches = []
        out_shape = [jax.ShapeDtypeStruct((BATCH, DEPTH, SEQ, w), F32) for w in CACHE_WIDTHS]
        out_specs = [pl.BlockSpec((CTX_REQ_PER_TILE, DEPTH, SEQ, w), lambda i, l: (i, 0, 0, 0)) for w in CACHE_WIDTHS]
    else:
        out_shape = [jax.ShapeDtypeStruct(c.shape, c.dtype) for c in new_caches]
        out_specs = [pl.BlockSpec((CTX_REQ_PER_TILE, 1, SEQ, w), lambda i, l: (i, l[0], 0, 0)) for w in CACHE_WIDTHS]
    n_state = len(CACHE_WIDTHS)
    for w in (Q_W, K_W, V_W):
        out_shape.append(jax.ShapeDtypeStruct((N_CTX_TOK, w), BF16))
        out_specs.append(pl.BlockSpec((TM_CTX, w), lambda i, l: (i, 0)))
    for _ in range(2):
        out_shape.append(jax.ShapeDtypeStruct((BATCH, 1, LANES), F32))
        out_specs.append(pl.BlockSpec((CTX_REQ_PER_TILE, 1, LANES), lambda i, l: (i, 0, 0)))
    n_fixed_inputs = 11
    aliases = {n_fixed_inputs + j: j for j in range(len(new_caches))}
    proj_spec = pltpu.PrefetchScalarGridSpec(
        num_scalar_prefetch=1,
        grid=(N_CTX_TOK // TM_CTX,),
        in_specs=[
            pl.BlockSpec((TM_CTX, D_MODEL), lambda i, l: (i, 0)),
            pl.BlockSpec((1, 1, 6, D_MODEL), lambda i, l: (l[0], CTX_MOD_ROW, 0, 0)),
            _layer_spec(p["norm1_g"].shape),
            _layer_spec(p["w_in"].shape),
            _layer_spec(p["mla_q_norm_g"].shape),
            _layer_spec(p["mla_w_qb"].shape),
            _layer_spec(p["mla_kv_norm_g"].shape),
            _layer_spec(p["mla_w_kvb"].shape),
            pl.BlockSpec(p["gq"].shape, lambda i, l: (0, 0)),
            pl.BlockSpec(p["gk"].shape, lambda i, l: (0, 0)),
        ] + [pl.BlockSpec(memory_space=pl.ANY)] * len(new_caches),
        out_specs=out_specs,
    )
    *outs, q, k, v, qmax, kmax = pl.pallas_call(
        functools.partial(_ctx_proj_kernel, first_layer=first_layer),
        grid_spec=proj_spec,
        out_shape=out_shape,
        input_output_aliases=aliases,
        compiler_params=pltpu.CompilerParams(
            dimension_semantics=("arbitrary",), vmem_limit_bytes=VMEM_LIMIT),
        name="ctx_project",
    )(l_idx, x, mod, p["norm1_g"], p["w_in"], p["mla_q_norm_g"], p["mla_w_qb"], p["mla_kv_norm_g"],
      p["mla_w_kvb"], p["gq"], p["gk"], *new_caches)
    assert len(outs) == n_state

    attn_spec = pltpu.PrefetchScalarGridSpec(
        num_scalar_prefetch=1,
        grid=(BATCH,),
        in_specs=[
            pl.BlockSpec((SEQ, Q_W), lambda i, l: (i, 0)),
            pl.BlockSpec((SEQ, K_W), lambda i, l: (i, 0)),
            pl.BlockSpec((SEQ, V_W), lambda i, l: (i, 0)),
            pl.BlockSpec((1, 1, LANES), lambda i, l: (i, 0, 0)),
            pl.BlockSpec((1, 1, LANES), lambda i, l: (i, 0, 0)),
            _layer_spec(p["lamv"].shape),
            _layer_spec(p["subg"].shape),
            pl.BlockSpec(memory_space=pltpu.SMEM),
        ],
        out_specs=pl.BlockSpec((SEQ, D_MODEL), lambda i, l: (i, 0)),
    )
    mix = pl.pallas_call(
        _ctx_attn_kernel,
        grid_spec=attn_spec,
        out_shape=jax.ShapeDtypeStruct((N_CTX_TOK, D_MODEL), BF16),
        compiler_params=pltpu.CompilerParams(
            dimension_semantics=("arbitrary",), vmem_limit_bytes=VMEM_LIMIT),
        name="ctx_attend",
    )(l_idx, q, k, v, qmax, kmax, p["lamv"], p["subg"], p["scal"])
    return (*outs, mix)


def _rope(x, cos, sa, sb, half):
    return x * cos + pltpu.roll(x, LANES - half, 1) * sa + pltpu.roll(x, half, 1) * sb


def _lat_proj_kernel(l_ref, x_ref, mod_ref, n1g_ref, w_in_ref, qg_ref, wqb_ref, kvg_ref, wkvb_ref,
                     c32_ref, a32_ref, b32_ref, c64_ref, a64_ref, b64_ref, cm_ref, am_ref, bm_ref, gq_ref,
                     q_ref, k_ref, v_ref, qmax_ref):
    proj = _project(x_ref, mod_ref, n1g_ref, w_in_ref)
    t32 = (c32_ref[...], a32_ref[...], b32_ref[...], DA_QK // 4)
    t64 = (c64_ref[...], a64_ref[...], b64_ref[...], SW_DIM // 4)
    tml = (cm_ref[...], am_ref[...], bm_ref[...], MLA_ROPE // 4)

    for j in range(2):
        q = _rope(_chunk(proj, j), *t32) * (DA_SCALE * LOG2E)
        q_ref[:, j * LANES:(j + 1) * LANES] = q.astype(BF16)
    for j in range(4):
        q = _rope(_chunk(proj, C_SWQ // LANES + j), *t64) * (SW_SCALE * LOG2E)
        q_ref[:, 256 + j * LANES:256 + (j + 1) * LANES] = q.astype(BF16)
    mq = jnp.dot(_rms(proj[:, C_QA:C_CKV], qg_ref[0]).astype(BF16), wqb_ref[0], preferred_element_type=F32)
    for h in range(MLA_HEADS):
        q = _rope(_chunk(mq, h), *tml) * (MLA_SCALE * LOG2E)
        q_ref[:, O_MQ + h * LANES:O_MQ + (h + 1) * LANES] = q.astype(BF16)
    qmax_ref[0] = _max_group_sq_norm(q_ref[...], gq_ref[...])

    for j in range(2):
        k_ref[:, j * LANES:(j + 1) * LANES] = _rope(_chunk(proj, C_DAK // LANES + j), *t32).astype(BF16)
    k_ref[:, 256:384] = _rope(proj[:, C_SWK:C_SWV], *t64).astype(BF16)
    ckv = _rms(proj[:, C_CKV:C_KR], kvg_ref[0])
    kv = jnp.dot(ckv.astype(BF16), wkvb_ref[0], preferred_element_type=F32)
    krz = _rope(proj[:, C_KR:P_EXT], *tml)
    for h in range(MLA_HEADS):
        k_ref[:, O_MK + h * LANES:O_MK + (h + 1) * LANES] = (_chunk(kv, h) + krz).astype(BF16)

    v_ref[:, 0:256] = proj[:, C_DAV:C_SWQ].astype(BF16)
    v_ref[:, 256:384] = proj[:, C_SWV:C_QA].astype(BF16)
    v_ref[:, 384:640] = kv[:, MLA_CHUNKS:].astype(BF16)


def _lat_project(l_idx, x, mod, p, tables):
    tiles_per_req = DEC_SEQ // TM_LAT
    tab_spec = pl.BlockSpec((TM_LAT, LANES), lambda i, l: (i % tiles_per_req, 0))
    grid_spec = pltpu.PrefetchScalarGridSpec(
        num_scalar_prefetch=1,
        grid=(N_LAT_TOK // TM_LAT,),
        in_specs=[
            pl.BlockSpec((TM_LAT, D_MODEL), lambda i, l: (i, 0)),
            pl.BlockSpec((1, 1, 6, D_MODEL), lambda i, l: (l[0], i // tiles_per_req, 0, 0)),
            _layer_spec(p["norm1_g"].shape),
            _layer_spec(p["w_in"].shape),
            _layer_spec(p["mla_q_norm_g"].shape),
            _layer_spec(p["mla_w_qb"].shape),
            _layer_spec(p["mla_kv_norm_g"].shape),
            _layer_spec(p["mla_w_kvb"].shape),
        ] + [tab_spec] * len(tables) + [pl.BlockSpec(p["gq"].shape, lambda i, l: (0, 0))],
        out_specs=[
            pl.BlockSpec((TM_LAT, Q_W), lambda i, l: (i, 0)),
            pl.BlockSpec((TM_LAT, K_W), lambda i, l: (i, 0)),
            pl.BlockSpec((TM_LAT, V_W), lambda i, l: (i, 0)),
            pl.BlockSpec((1, 1, LANES), lambda i, l: (i, 0, 0)),
        ],
    )
    return pl.pallas_call(
        _lat_proj_kernel,
        grid_spec=grid_spec,
        out_shape=[jax.ShapeDtypeStruct((N_LAT_TOK, Q_W), BF16),
                   jax.ShapeDtypeStruct((N_LAT_TOK, K_W), BF16),
                   jax.ShapeDtypeStruct((N_LAT_TOK, V_W), BF16),
                   jax.ShapeDtypeStruct((N_LAT_TOK // TM_LAT, 1, LANES), F32)],
        compiler_params=pltpu.CompilerParams(
            dimension_semantics=("arbitrary",), vmem_limit_bytes=VMEM_LIMIT),
        name="latent_project",
    )(l_idx, x, mod, p["norm1_g"], p["w_in"], p["mla_q_norm_g"], p["mla_w_qb"], p["mla_kv_norm_g"],
      p["mla_w_kvb"], *tables, p["gq"])


def _lat_attn_kernel(l_ref, q_ref, k_ref, v_ref, cdk_ref, cdv_ref, csk_ref, csv_ref, cckv_ref, ckr_ref,
                     wkvb_ref, pad_ref, lamv_ref, subg_ref, qmax_ref, gk_ref, scal_ref,
                     mix_ref,
                     kda, vda, ksw, vsw, kml, vml, kmax_sq):
    l = l_ref[0]
    qb = pl.program_id(1)

    @pl.when(qb == 0)
    def _stage_keys():
        kda[0:PAST_LEN, :] = cdk_ref[0, 0].astype(BF16)
        kda[PAST_LEN:N_KEYS, :] = k_ref[0, :, 0:256]
        vda[0:PAST_LEN, :] = cdv_ref[0, 0].astype(BF16)
        vda[PAST_LEN:N_KEYS, :] = v_ref[0, :, 0:256]
        sk_lat = k_ref[0, :, 256:384].astype(F32)
        sv_lat = v_ref[0, :, 256:384].astype(F32)
        for g in range(SW_KV_HEADS):
            ksw[0:PAST_LEN, g * LANES:(g + 1) * LANES] = _both_halves(csk_ref[0, 0], g).astype(BF16)
            ksw[PAST_LEN:N_KEYS, g * LANES:(g + 1) * LANES] = _both_halves(sk_lat, g).astype(BF16)
            vsw[0:PAST_LEN, 2 * g * LANES:(2 * g + 1) * LANES] = _both_halves(csv_ref[0, 0], g).astype(BF16)
            vsw[PAST_LEN:N_KEYS, 2 * g * LANES:(2 * g + 1) * LANES] = _both_halves(sv_lat, g).astype(BF16)
            vsw[:, (2 * g + 1) * LANES:(2 * g + 2) * LANES] = jnp.ones((N_KEYS, LANES), BF16)
        kvc = jnp.dot(cckv_ref[0, 0].astype(BF16), wkvb_ref[0], preferred_element_type=F32)
        krz = jnp.dot(ckr_ref[0, 0].astype(BF16), pad_ref[...], preferred_element_type=F32)
        mk_ctx = [_chunk(kvc, h) + krz for h in range(MLA_HEADS)]
        for h in range(MLA_HEADS):
            kml[0:PAST_LEN, h * LANES:(h + 1) * LANES] = mk_ctx[h].astype(BF16)
            v_ctx = _chunk(kvc, MLA_HEADS + h // 2)
            v_lat = v_ref[0, :, 384 + (h // 2) * LANES:384 + (h // 2 + 1) * LANES].astype(F32)
            vml[0:PAST_LEN, h * LANES:(h + 1) * LANES] = _ones_in_other_half(v_ctx, h % 2).astype(BF16)
            vml[PAST_LEN:N_KEYS, h * LANES:(h + 1) * LANES] = _ones_in_other_half(v_lat, h % 2).astype(BF16)
        kml[PAST_LEN:N_KEYS, :] = k_ref[0, :, O_MK:K_W]
        gk = gk_ref[...]
        k_ctx = jnp.concatenate([cdk_ref[0, 0], csk_ref[0, 0]] + mk_ctx, axis=-1)
        kmax_sq[...] = jnp.maximum(_max_group_sq_norm(k_ctx, gk), _max_group_sq_norm(k_ref[0], gk))

    bound_sq = qmax_ref[0] * kmax_sq[...] * BOUND_MARGIN
    small = (jnp.max(bound_sq) <= NOSHIFT_BOUND ** 2) & (scal_ref[l, S_SINK_ABSMAX] <= NOSHIFT_BOUND)

    lam = _diff_lambda(lamv_ref, scal_ref[l, S_LAM_INIT])
    subg = subg_ref[0]
    sinks = [scal_ref[l, h] for h in range(SW_HEADS)]
    q0 = qb * TQ
    start = pl.multiple_of(jnp.clip(q0 - WINDOW, 0, DEC_SEQ - SW_BAND), LANES)

    def attend(shift):
        da_o = _diff_attention(q_ref[0, :, 0:256].astype(F32), lambda: kda[...],
                               lambda c: vda[:, c * LANES:(c + 1) * LANES], lam, shift)
        for c, o in enumerate(da_o):
            mix_ref[0, :, c * LANES:(c + 1) * LANES] = _diff_subln(
                o, _chunk(subg, c), scal_ref[l, S_ONE_MINUS]).astype(BF16)

        band = pl.ds(PAST_LEN + start, SW_BAND)
        k_sel = jnp.concatenate([ksw[0:PAST_LEN, :], ksw[band, :]], axis=0)
        v_sel = jnp.concatenate([vsw[0:PAST_LEN, :], vsw[band, :]], axis=0)
        n_sel = PAST_LEN + SW_BAND
        col = lax.broadcasted_iota(jnp.int32, (TQ, n_sel), 1)
        row = lax.broadcasted_iota(jnp.int32, (TQ, n_sel), 0)
        dist = (q0 + row) - (start + col - PAST_LEN)
        valid = (col < PAST_LEN) | ((dist <= WINDOW) & (dist >= -WINDOW))
        sw_o = _swa_attention(q_ref[0, :, 256:768].astype(F32), lambda g: _chunk(k_sel, g),
                              lambda g: v_sel[:, 2 * g * LANES:(2 * g + 2) * LANES], sinks, shift, valid)
        for c, o in enumerate(sw_o):
            mix_ref[0, :, 256 + c * LANES:256 + (c + 1) * LANES] = o.astype(BF16)

        mla_o = _mla_attention(lambda h: q_ref[0, :, O_MQ + h * LANES:O_MQ + (h + 1) * LANES],
                               lambda h: kml[:, h * LANES:(h + 1) * LANES],
                               lambda h: vml[:, h * LANES:(h + 1) * LANES], shift)
        for c, o in enumerate(mla_o):
            mix_ref[0, :, 768 + c * LANES:768 + (c + 1) * LANES] = o.astype(BF16)

    pl.when(small)(functools.partial(attend, False))
    pl.when(jnp.logical_not(small))(functools.partial(attend, True))


def _lat_attend(l_idx, q, k, v, qmax, caches, p, pad_eye):
    cdk, cdv, csk, csv, cckv, ckr = caches
    proj_tiles_per_req = DEC_SEQ // TM_LAT

    def cache_spec(a):
        return pl.BlockSpec((1, 1) + a.shape[2:], lambda b, j, l: (b, l[0], 0, 0))

    grid_spec = pltpu.PrefetchScalarGridSpec(
        num_scalar_prefetch=1,
        grid=(DEC_BATCH, DEC_SEQ // TQ),
        in_specs=[
            pl.BlockSpec((1, TQ, Q_W), lambda b, j, l: (b, j, 0)),
            pl.BlockSpec((1, DEC_SEQ, K_W), lambda b, j, l: (b, 0, 0)),
            pl.BlockSpec((1, DEC_SEQ, V_W), lambda b, j, l: (b, 0, 0)),
            cache_spec(cdk), cache_spec(cdv), cache_spec(csk), cache_spec(csv), cache_spec(cckv), cache_spec(ckr),
            _layer_spec(p["mla_w_kvb"].shape),
            pl.BlockSpec(pad_eye.shape, lambda b, j, l: (0, 0)),
            _layer_spec(p["lamv"].shape),
            _layer_spec(p["subg"].shape),
            pl.BlockSpec((1, 1, LANES), lambda b, j, l: (b * proj_tiles_per_req + (j * TQ) // TM_LAT, 0, 0)),
            pl.BlockSpec(p["gk"].shape, lambda b, j, l: (0, 0)),
            pl.BlockSpec(memory_space=pltpu.SMEM),
        ],
        out_specs=pl.BlockSpec((1, TQ, D_MODEL), lambda b, j, l: (b, j, 0)),
        scratch_shapes=[
            pltpu.VMEM((N_KEYS, 256), BF16), pltpu.VMEM((N_KEYS, 256), BF16),
            pltpu.VMEM((N_KEYS, SW_KV_HEADS * LANES), BF16), pltpu.VMEM((N_KEYS, SW_KV_HEADS * 2 * LANES), BF16),
            pltpu.VMEM((N_KEYS, MLA_CHUNKS), BF16), pltpu.VMEM((N_KEYS, MLA_CHUNKS), BF16),
            pltpu.VMEM((1, LANES), F32),
        ],
    )
    return pl.pallas_call(
        _lat_attn_kernel,
        grid_spec=grid_spec,
        out_shape=jax.ShapeDtypeStruct((DEC_BATCH, DEC_SEQ, D_MODEL), BF16),
        compiler_params=pltpu.CompilerParams(
            dimension_semantics=("arbitrary", "arbitrary"), vmem_limit_bytes=VMEM_LIMIT),
        name="latent_attend",
    )(l_idx, q.reshape(DEC_BATCH, DEC_SEQ, Q_W), k.reshape(DEC_BATCH, DEC_SEQ, K_W),
      v.reshape(DEC_BATCH, DEC_SEQ, V_W), cdk, cdv, csk, csv, cckv, ckr,
      p["mla_w_kvb"], pad_eye, p["lamv"], p["subg"], qmax, p["gk"], p["scal"])


def _mlp_kernel(l_ref, x_ref, mix_ref, mod_ref, n2g_ref, w_out_ref, w_up_ref, w_down_ref, fg_ref, o_ref,
                *, final_norm):
    m = mod_ref[0, 0]
    x = x_ref[...] + m[2:3] * jnp.dot(mix_ref[...], w_out_ref[0], preferred_element_type=F32)
    h = (_rms(x, n2g_ref[0]) * (1.0 + m[4:5]) + m[3:4]).astype(BF16)
    acc = jnp.zeros_like(x)
    for c in range(D_FF // FF_CHUNK):
        u = jnp.dot(h, w_up_ref[0, :, c * FF_CHUNK:(c + 1) * FF_CHUNK], preferred_element_type=F32)
        a = jnp.square(jnp.maximum(u, 0.0)).astype(BF16)
        acc = acc + jnp.dot(a, w_down_ref[0, c * FF_CHUNK:(c + 1) * FF_CHUNK, :], preferred_element_type=F32)
    x = x + m[5:6] * acc
    if final_norm:
        x = _rms(x, fg_ref[...])
    o_ref[...] = x


def _mlp_layer(l_idx, x, mix, mod, p, final_g, mod_row, final_norm):
    n_tok = x.shape[0]
    grid_spec = pltpu.PrefetchScalarGridSpec(
        num_scalar_prefetch=1,
        grid=(n_tok // TM_MLP,),
        in_specs=[
            pl.BlockSpec((TM_MLP, D_MODEL), lambda i, l: (i, 0)),
            pl.BlockSpec((TM_MLP, D_MODEL), lambda i, l: (i, 0)),
            pl.BlockSpec((1, 1, 6, D_MODEL), lambda i, l: (l[0], mod_row(i), 0, 0)),
            _layer_spec(p["norm2_g"].shape),
            _layer_spec(p["w_out"].shape),
            _layer_spec(p["w_up"].shape),
            _layer_spec(p["w_down"].shape),
            pl.BlockSpec((1, D_MODEL), lambda i, l: (0, 0)),
        ],
        out_specs=pl.BlockSpec((TM_MLP, D_MODEL), lambda i, l: (i, 0)),
    )
    return pl.pallas_call(
        functools.partial(_mlp_kernel, final_norm=final_norm),
        grid_spec=grid_spec,
        out_shape=jax.ShapeDtypeStruct((n_tok, D_MODEL), F32),
        compiler_params=pltpu.CompilerParams(
            dimension_semantics=("arbitrary",), vmem_limit_bytes=VMEM_LIMIT),
        name="out_proj_mlp",
    )(l_idx, x, mix, mod, p["norm2_g"], p["w_out"], p["w_up"], p["w_down"], final_g)


def _rope_tables(period, active_lanes=LANES):
    m = period // 2
    half = m // 2
    t = np.arange(DEC_SEQ)
    pos = np.stack([t // GRID_W, t % GRID_W], axis=0).astype(np.float64)
    lane = np.arange(LANES) % period
    axis = lane // m
    k = lane % m
    freq = ROPE_BASE ** (-(k % half).astype(np.float64) / half)
    ang = pos[axis, :].T * freq[None, :]
    first = (k < half)[None, :]
    active = (np.arange(LANES) < active_lanes)[None, :]
    cos, sin = np.where(active, np.cos(ang), 1.0), np.where(active, np.sin(ang), 0.0)
    return (jnp.asarray(cos, F32), jnp.asarray(np.where(first, -sin, 0.0), F32),
            jnp.asarray(np.where(first, 0.0, sin), F32))


def _group_matrices():
    gq = np.zeros((Q_W, LANES), np.float32)
    gk = np.zeros((K_W, LANES), np.float32)
    for d in range(DA_HEADS * 2 * DA_QK):
        gq[d, G_DIFF + d // DA_QK] = 1.0
        gk[d, G_DIFF + d // DA_QK] = 1.0
    for h in range(SW_HEADS):
        g = h // (SW_HEADS // SW_KV_HEADS)
        gq[256 + h * SW_DIM:256 + (h + 1) * SW_DIM, G_SWA + h] = 1.0
        gk[256 + g * SW_DIM:256 + (g + 1) * SW_DIM, G_SWA + h] = 1.0
    for h in range(MLA_HEADS):
        gq[O_MQ + h * LANES:O_MQ + (h + 1) * LANES, G_MLA + h] = 1.0
        gk[O_MK + h * LANES:O_MK + (h + 1) * LANES, G_MLA + h] = 1.0
    return jnp.asarray(gq, BF16), jnp.asarray(gk, BF16)


def _prepare_params(norm1_g, norm2_g, w_in, w_out, lq1, lk1, lq2, lk2, diff_subln_g, swa_sink,
                    mla_q_norm_g, mla_w_qb, mla_kv_norm_g, mla_w_kvb, w_up, w_down):
    w_in_ext = jnp.pad(w_in.astype(BF16), ((0, 0), (0, 0), (0, P_EXT - w_in.shape[-1])))
    w_out_p = w_out.astype(BF16)
    dq = MLA_NOPE + MLA_ROPE
    gap = LANES - dq
    zq = jnp.zeros((DEPTH, MLA_Q_RANK, gap), mla_w_qb.dtype)
    wqb = jnp.concatenate(
        [blk for h in range(MLA_HEADS)
         for blk in (mla_w_qb[:, :, h * dq + MLA_NOPE:(h + 1) * dq], zq, mla_w_qb[:, :, h * dq:h * dq + MLA_NOPE])],
        axis=-1).astype(BF16)
    dkv = MLA_NOPE + MLA_V
    zk = jnp.zeros((DEPTH, MLA_KV_RANK, LANES - MLA_NOPE), mla_w_kvb.dtype)
    wkvb = jnp.concatenate(
        [blk for h in range(MLA_HEADS) for blk in (zk, mla_w_kvb[:, :, h * dkv:h * dkv + MLA_NOPE])]
        + [mla_w_kvb[:, :, h * dkv + MLA_NOPE:(h + 1) * dkv] for h in range(MLA_HEADS)],
        axis=-1).astype(BF16)
    lam_init = np.array([0.8 - 0.6 * math.exp(-0.3 * l) for l in range(DEPTH)], np.float32)
    sink_l2 = swa_sink.astype(F32) * LOG2E
    scal = jnp.concatenate([
        sink_l2,
        jnp.asarray(np.stack([lam_init, (1.0 - lam_init.astype(np.float64)).astype(np.float32)], axis=1)),
        jnp.max(jnp.abs(sink_l2), axis=1, keepdims=True),
        jnp.zeros((DEPTH, S_COLS - SW_HEADS - 3), F32)], axis=1)
    gq, gk = _group_matrices()
    return {
        "gq": gq,
        "gk": gk,
        "norm1_g": norm1_g.reshape(DEPTH, 1, D_MODEL),
        "norm2_g": norm2_g.reshape(DEPTH, 1, D_MODEL),
        "w_in": w_in_ext,
        "w_out": w_out_p,
        "mla_q_norm_g": mla_q_norm_g.reshape(DEPTH, 1, MLA_Q_RANK),
        "mla_w_qb": wqb,
        "mla_kv_norm_g": mla_kv_norm_g.reshape(DEPTH, 1, MLA_KV_RANK),
        "mla_w_kvb": wkvb,
        "lamv": jnp.stack([lq1, lk1, lq2, lk2], axis=1),
        "subg": jnp.tile(diff_subln_g, (1, DA_HEADS)).reshape(DEPTH, 1, DA_HEADS * DA_V),
        "scal": scal,
        "w_up": w_up.astype(BF16),
        "w_down": w_down.astype(BF16),
    }


def kernel(x_prompt, x_sample, c, cache_diff_k, cache_diff_v, cache_swa_k, cache_swa_v, cache_mla_ckv,
           cache_mla_krope, c_ctx, norm1_g, norm2_g, w_ada, b_ada, w_in, w_out, diff_lambda_q1,
           diff_lambda_k1, diff_lambda_q2, diff_lambda_k2, diff_subln_g, swa_sink, mla_q_norm_g,
           mla_w_qb, mla_kv_norm_g, mla_w_kvb, w_up, w_down, final_g):
    p = _prepare_params(norm1_g, norm2_g, w_in, w_out, diff_lambda_q1, diff_lambda_k1, diff_lambda_q2,
                        diff_lambda_k2, diff_subln_g, swa_sink, mla_q_norm_g, mla_w_qb, mla_kv_norm_g,
                        mla_w_kvb, w_up, w_down)
    cvecs = jnp.concatenate([c, c_ctx[None, :], jnp.zeros((8 - DEC_BATCH - 1, D_MODEL), F32)], axis=0)
    mod = _modulation(cvecs, w_ada, b_ada)
    tables = _rope_tables(DA_QK) + _rope_tables(SW_DIM) + _rope_tables(MLA_ROPE, MLA_ROPE)
    pad_eye = jnp.asarray(np.eye(MLA_ROPE, LANES, dtype=np.float32), BF16)
    final_g2 = final_g.reshape(1, D_MODEL)
    caches = (cache_diff_k.reshape(DEC_BATCH, DEPTH, PAST_LEN, DA_HEADS * 2 * DA_QK),
              cache_diff_v.reshape(DEC_BATCH, DEPTH, PAST_LEN, DA_HEADS * DA_V),
              cache_swa_k.reshape(DEC_BATCH, DEPTH, PAST_LEN, SW_KV_HEADS * SW_DIM),
              cache_swa_v.reshape(DEC_BATCH, DEPTH, PAST_LEN, SW_KV_HEADS * SW_DIM),
              cache_mla_ckv, cache_mla_krope)

    x_ctx = x_prompt.reshape(N_CTX_TOK, D_MODEL)
    x_lat = x_sample.reshape(N_LAT_TOK, D_MODEL)
    new = None
    lat_tiles_per_req = DEC_SEQ // TM_MLP
    for l in range(DEPTH):
        l_idx = jnp.full((1,), l, jnp.int32)
        last = l == DEPTH - 1
        *new, mix_ctx = _ctx_layer(l_idx, x_ctx, mod, p, new)
        x_ctx = _mlp_layer(l_idx, x_ctx, mix_ctx, mod, p, final_g2, lambda i: CTX_MOD_ROW, last)
        q, k, v, qmax = _lat_project(l_idx, x_lat, mod, p, tables)
        mix_lat = _lat_attend(l_idx, q, k, v, qmax, caches, p, pad_eye).reshape(N_LAT_TOK, D_MODEL)
        x_lat = _mlp_layer(l_idx, x_lat, mix_lat, mod, p, final_g2, lambda i: i // lat_tiles_per_req, last)

    y_prompt = x_ctx.reshape(BATCH, SEQ, D_MODEL)
    y_sample = x_lat.reshape(DEC_BATCH, DEC_SEQ, D_MODEL)
    new_dk, new_dv, new_sk, new_sv, new_ckv, new_kr = new
    return (y_prompt, y_sample,
            new_dk.reshape(BATCH, DEPTH, SEQ, DA_HEADS, 2 * DA_QK),
            new_dv.reshape(BATCH, DEPTH, SEQ, DA_HEADS, DA_V),
            new_sk.reshape(BATCH, DEPTH, SEQ, SW_KV_HEADS, SW_DIM),
            new_sv.reshape(BATCH, DEPTH, SEQ, SW_KV_HEADS, SW_DIM),
            new_ckv, new_kr)
```

```python
import functools
import math

import numpy as np
import jax
import jax.numpy as jnp
from jax import lax
from jax.experimental import pallas as pl
from jax.experimental.pallas import tpu as pltpu

F32 = jnp.float32
BF16 = jnp.bfloat16

D_MODEL = 1024
BATCH = 32
SEQ = 256
DEPTH = 4
DEC_BATCH = 4
DEC_SEQ = 2048
PAST_LEN = 256
GRID_W = 64
WINDOW = 128
ROPE_BASE = 10000.0
EPS = 1e-6
DA_HEADS = 4
DA_QK = 32
DA_V = 64
SW_HEADS = 8
SW_KV_HEADS = 2
SW_DIM = 64
MLA_HEADS = 4
MLA_Q_RANK = 256
MLA_KV_RANK = 128
MLA_NOPE = 64
MLA_ROPE = 32
MLA_V = 64
D_FF = 4 * D_MODEL

LANES = 128
HALF = LANES // 2
N_CTX_TOK = BATCH * SEQ
N_LAT_TOK = DEC_BATCH * DEC_SEQ
N_KEYS = PAST_LEN + DEC_SEQ
CTX_MOD_ROW = DEC_BATCH

C_DAQ, C_DAK, C_DAV, C_SWQ, C_SWK, C_SWV, C_QA, C_CKV, C_KR, P_EXT = (
    0, 256, 512, 768, 1280, 1408, 1536, 1792, 1920, 2048)
MLA_CHUNKS = MLA_HEADS * 128
Q_W = 256 + 512 + MLA_CHUNKS
K_W = 256 + 128 + MLA_CHUNKS
V_W = 256 + 128 + 256
O_MQ = 768
O_MK = 384
LOG2E = 1.4426950408889634

DA_SCALE = DA_QK ** -0.5
SW_SCALE = SW_DIM ** -0.5
MLA_SCALE = (MLA_NOPE + MLA_ROPE) ** -0.5

CTX_REQ_PER_TILE = 2
TM_CTX = CTX_REQ_PER_TILE * SEQ
TM_LAT = 512
TM_MLP = 512
TQ = 256
SW_BAND = TQ + 2 * WINDOW
FF_CHUNK = 4096
MOD_TILE = 1536
VMEM_LIMIT = 56 * 1024 * 1024

_NT = (((1,), (1,)), ((), ()))

S_LAM_INIT = SW_HEADS
S_ONE_MINUS = SW_HEADS + 1
S_SINK_ABSMAX = SW_HEADS + 2
S_COLS = 16

NOSHIFT_BOUND = 50.0
G_DIFF, G_SWA, G_MLA = 0, 8, 16


def _scores(q, k):
    return lax.dot_general(q, k, _NT, preferred_element_type=F32)


def _rms(x, g):
    ms = jnp.mean(x * x, axis=-1, keepdims=True)
    return x * lax.rsqrt(ms + EPS) * g


def _lane_mask(n, lo, width):
    lane = lax.broadcasted_iota(jnp.int32, (1, n), 1)
    return (lane >= lo) & (lane < lo + width)


def _exp_scores(s, shift, sink=None):
    if not shift:
        return jnp.exp2(s), jnp.zeros((1, 1), F32)
    m = jnp.max(s, axis=-1, keepdims=True)
    if sink is not None:
        m = jnp.maximum(m, sink)
    return jnp.exp2(s - m), m


def _sq_bf16(x):
    xf = x.astype(F32)
    return (xf * xf).astype(BF16)


def _max_group_sq_norm(x, g):
    return jnp.max(jnp.dot(_sq_bf16(x), g, preferred_element_type=F32), axis=0, keepdims=True)


BOUND_MARGIN = 1.03


def _chunk(a, j):
    return a[:, j * LANES:(j + 1) * LANES]


def _project(x_ref, mod_ref, n1g_ref, w_in_ref):
    m = mod_ref[0, 0]
    h = _rms(x_ref[...], n1g_ref[0]) * (1.0 + m[1:2]) + m[0:1]
    return jnp.dot(h.astype(BF16), w_in_ref[0], preferred_element_type=F32)


def _diff_lambda(lamv_ref, lam_init):
    lv = lamv_ref[0]
    a = jnp.sum(lv[0:1] * lv[1:2], axis=-1, keepdims=True)
    b = jnp.sum(lv[2:3] * lv[3:4], axis=-1, keepdims=True)
    return jnp.exp(a) - jnp.exp(b) + lam_init


def _half_mask(par):
    lane = lax.broadcasted_iota(jnp.int32, (1, LANES), 1)
    return (lane >= par * HALF) & (lane < (par + 1) * HALF)


def _diff_attention(qf, load_k, load_v, lam, shift):
    width = DA_HEADS * DA_V
    chunks = []
    for c in range(DA_HEADS // 2):
        o_par = []
        for par in range(2):
            lo = (2 * c + par) * DA_V
            q1 = jnp.where(_lane_mask(width, lo, DA_QK), qf, 0.0).astype(BF16)
            q2 = jnp.where(_lane_mask(width, lo + DA_QK, DA_QK), qf, 0.0).astype(BF16)
            e1, _ = _exp_scores(_scores(q1, load_k()), shift)
            e2, _ = _exp_scores(_scores(q2, load_k()), shift)
            d1 = jnp.sum(e1, axis=-1, keepdims=True)
            d2 = jnp.sum(e2, axis=-1, keepdims=True)
            w = (e1 - e2 * (lam * d1 * (1.0 / d2))).astype(BF16)
            o_par.append(jnp.dot(w, load_v(c), preferred_element_type=F32) * (1.0 / d1))
        chunks.append(jnp.where(_half_mask(0), o_par[0], o_par[1]))
    return chunks


def _diff_subln(o, subg, one_minus_lam_init):
    sq = o * o
    r = None
    for par in range(2):
        ms = jnp.sum(jnp.where(_half_mask(par), sq, 0.0), axis=-1, keepdims=True) * (1.0 / DA_V)
        rs = lax.rsqrt(ms + EPS)
        r = rs if r is None else jnp.where(_half_mask(0), r, rs)
    return o * r * subg * one_minus_lam_init


def _both_halves(x, g):
    swapped = pltpu.roll(x, HALF, 1)
    return jnp.where(_half_mask(g), x, swapped)


def _swa_attention(qf, load_k, load_v, sinks, shift, valid=None):
    heads_per_group = SW_HEADS // SW_KV_HEADS
    outs = []
    for c in range(SW_HEADS // 2):
        qc = _chunk(qf, c)
        g = (2 * c) // heads_per_group
        o_par = []
        for par in range(2):
            h = 2 * c + par
            qm = jnp.where(_half_mask(par), qc, 0.0).astype(BF16)
            s = _scores(qm, load_k(g))
            if valid is not None:
                s = jnp.where(valid, s, -1e30)
            e, m = _exp_scores(s, shift, sink=sinks[h])
            o = jnp.dot(e.astype(BF16), load_v(g), preferred_element_type=F32)
            d = _chunk(o, 1) + jnp.exp2(sinks[h] - m)
            o_par.append(_chunk(o, 0) * (1.0 / d))
        outs.append(jnp.where(_half_mask(0), o_par[0], o_par[1]))
    return outs


def _mla_attention(load_q, load_k, load_v, shift):
    chunks = []
    for c in range(MLA_HEADS // 2):
        o_par = []
        for par in range(2):
            h = 2 * c + par
            e, _ = _exp_scores(_scores(load_q(h), load_k(h)), shift)
            o = jnp.dot(e.astype(BF16), load_v(h), preferred_element_type=F32)
            o_par.append(o * (1.0 / pltpu.roll(o, HALF, 1)))
        chunks.append(jnp.where(_half_mask(0), o_par[0], o_par[1]))
    return chunks


def _ones_in_other_half(v, par):
    return jnp.where(_half_mask(par), v, 1.0)


def _mod_kernel(c_ref, w_ref, b_ref, o_ref):
    c = c_ref[...]
    s = c * (1.0 / (1.0 + jnp.exp(-c)))
    o_ref[0] = jnp.dot(s.astype(BF16), w_ref[0].astype(BF16), preferred_element_type=F32) + b_ref[0]


def _modulation(cvecs, w_ada, b_ada):
    n_out = 6 * D_MODEL
    out = pl.pallas_call(
        _mod_kernel,
        grid=(DEPTH, n_out // MOD_TILE),
        in_specs=[
            pl.BlockSpec((8, D_MODEL), lambda l, j: (0, 0)),
            pl.BlockSpec((1, D_MODEL, MOD_TILE), lambda l, j: (l, 0, j)),
            pl.BlockSpec((1, 1, MOD_TILE), lambda l, j: (l, 0, j)),
        ],
        out_specs=pl.BlockSpec((1, 8, MOD_TILE), lambda l, j: (l, 0, j)),
        out_shape=jax.ShapeDtypeStruct((DEPTH, 8, n_out), F32),
        compiler_params=pltpu.CompilerParams(
            dimension_semantics=("arbitrary", "arbitrary"), vmem_limit_bytes=VMEM_LIMIT),
        name="adaln_modulation",
    )(cvecs, w_ada, b_ada.reshape(DEPTH, 1, n_out))
    return out.reshape(DEPTH, 8, 6, D_MODEL)


def _ctx_proj_kernel(l_ref, x_ref, mod_ref, n1g_ref, w_in_ref, qg_ref, wqb_ref, kvg_ref, wkvb_ref, gq_ref, gk_ref,
                     *rest, first_layer):
    dk_ref, dv_ref, sk_ref, sv_ref, ckv_ref, kr_ref, q_ref, k_ref, v_ref, qmax_ref, kmax_ref = rest[-11:]
    if first_layer:
        for ref in (dk_ref, dv_ref, sk_ref, sv_ref, ckv_ref, kr_ref):
            ref[:, 1:] = jnp.zeros((CTX_REQ_PER_TILE, DEPTH - 1) + ref.shape[2:], F32)
    proj = _project(x_ref, mod_ref, n1g_ref, w_in_ref)
    ckv = _rms(proj[:, C_CKV:C_KR], kvg_ref[0])
    mq = jnp.dot(_rms(proj[:, C_QA:C_CKV], qg_ref[0]).astype(BF16), wqb_ref[0], preferred_element_type=F32)
    kv = jnp.dot(ckv.astype(BF16), wkvb_ref[0], preferred_element_type=F32)
    krz = proj[:, C_KR:P_EXT]
    for r in range(CTX_REQ_PER_TILE):
        r0, r1 = r * SEQ, (r + 1) * SEQ
        dk_ref[r, 0] = proj[r0:r1, C_DAK:C_DAV]
        dv_ref[r, 0] = proj[r0:r1, C_DAV:C_SWQ]
        sk_ref[r, 0] = proj[r0:r1, C_SWK:C_SWV]
        sv_ref[r, 0] = proj[r0:r1, C_SWV:C_QA]
        ckv_ref[r, 0] = ckv[r0:r1]
        kr_ref[r, 0] = krz[r0:r1, 0:MLA_ROPE]

    q_ref[:, 0:256] = (proj[:, C_DAQ:C_DAK] * (DA_SCALE * LOG2E)).astype(BF16)
    q_ref[:, 256:O_MQ] = (proj[:, C_SWQ:C_SWK] * (SW_SCALE * LOG2E)).astype(BF16)
    q_ref[:, O_MQ:Q_W] = (mq * (MLA_SCALE * LOG2E)).astype(BF16)
    k_ref[:, 0:256] = proj[:, C_DAK:C_DAV].astype(BF16)
    k_ref[:, 256:O_MK] = proj[:, C_SWK:C_SWV].astype(BF16)
    for h in range(MLA_HEADS):
        k_ref[:, O_MK + h * LANES:O_MK + (h + 1) * LANES] = (_chunk(kv, h) + krz).astype(BF16)
    v_ref[:, 0:256] = proj[:, C_DAV:C_SWQ].astype(BF16)
    v_ref[:, 256:384] = proj[:, C_SWV:C_QA].astype(BF16)
    v_ref[:, 384:V_W] = kv[:, MLA_CHUNKS:].astype(BF16)
    for r in range(CTX_REQ_PER_TILE):
        r0, r1 = r * SEQ, (r + 1) * SEQ
        qmax_ref[r] = _max_group_sq_norm(q_ref[r0:r1, :], gq_ref[...])
        kmax_ref[r] = _max_group_sq_norm(k_ref[r0:r1, :], gk_ref[...])


def _ctx_attn_kernel(l_ref, q_ref, k_ref, v_ref, qmax_ref, kmax_ref, lamv_ref, subg_ref, scal_ref, mix_ref):
    l = l_ref[0]
    bound_sq = qmax_ref[0] * kmax_ref[0] * BOUND_MARGIN
    small = (jnp.max(bound_sq) <= NOSHIFT_BOUND ** 2) & (scal_ref[l, S_SINK_ABSMAX] <= NOSHIFT_BOUND)
    lam = _diff_lambda(lamv_ref, scal_ref[l, S_LAM_INIT])
    subg = subg_ref[0]
    sinks = [scal_ref[l, h] for h in range(SW_HEADS)]

    def attend(shift):
        da_o = _diff_attention(q_ref[:, 0:256].astype(F32), lambda: k_ref[:, 0:256],
                               lambda c: v_ref[:, c * LANES:(c + 1) * LANES], lam, shift)
        for c, o in enumerate(da_o):
            mix_ref[:, c * LANES:(c + 1) * LANES] = _diff_subln(
                o, _chunk(subg, c), scal_ref[l, S_ONE_MINUS]).astype(BF16)
        sw_k = k_ref[:, 256:O_MK].astype(F32)
        sw_v = v_ref[:, 256:384].astype(F32)
        sw_kg = [_both_halves(sw_k, g).astype(BF16) for g in range(SW_KV_HEADS)]
        sw_vg = [jnp.concatenate([_both_halves(sw_v, g), jnp.ones_like(sw_v)], axis=-1).astype(BF16)
                 for g in range(SW_KV_HEADS)]
        sw_o = _swa_attention(q_ref[:, 256:O_MQ].astype(F32), lambda g: sw_kg[g], lambda g: sw_vg[g], sinks, shift)
        for c, o in enumerate(sw_o):
            mix_ref[:, 256 + c * LANES:256 + (c + 1) * LANES] = o.astype(BF16)
        mla_o = _mla_attention(
            lambda h: q_ref[:, O_MQ + h * LANES:O_MQ + (h + 1) * LANES],
            lambda h: k_ref[:, O_MK + h * LANES:O_MK + (h + 1) * LANES],
            lambda h: _ones_in_other_half(
                v_ref[:, 384 + (h // 2) * LANES:384 + (h // 2 + 1) * LANES].astype(F32), h % 2).astype(BF16),
            shift)
        for c, o in enumerate(mla_o):
            mix_ref[:, 768 + c * LANES:768 + (c + 1) * LANES] = o.astype(BF16)

    pl.when(small)(functools.partial(attend, False))
    pl.when(jnp.logical_not(small))(functools.partial(attend, True))


def _layer_spec(shape):
    nd = len(shape)
    return pl.BlockSpec((1,) + tuple(shape[1:]), lambda *a: (a[-1][0],) + (0,) * (nd - 1))


CACHE_WIDTHS = (DA_HEADS * 2 * DA_QK, DA_HEADS * DA_V, SW_KV_HEADS * SW_DIM, SW_KV_HEADS * SW_DIM,
                MLA_KV_RANK, MLA_ROPE)


def _ctx_layer(l_idx, x, mod, p, new_caches):
    first_layer = new_caches is None
    if first_layer:
        new_caches = []
        out_shape = [jax.ShapeDtypeStruct((BATCH, DEPTH, SEQ, w), F32) for w in CACHE_WIDTHS]
        out_specs = [pl.BlockSpec((CTX_REQ_PER_TILE, DEPTH, SEQ, w), lambda i, l: (i, 0, 0, 0)) for w in CACHE_WIDTHS]
    else:
        out_shape = [jax.ShapeDtypeStruct(c.shape, c.dtype) for c in new_caches]
        out_specs = [pl.BlockSpec((CTX_REQ_PER_TILE, 1, SEQ, w), lambda i, l: (i, l[0], 0, 0)) for w in CACHE_WIDTHS]
    n_state = len(CACHE_WIDTHS)
    for w in (Q_W, K_W, V_W):
        out_shape.append(jax.ShapeDtypeStruct((N_CTX_TOK, w), BF16))
        out_specs.append(pl.BlockSpec((TM_CTX, w), lambda i, l: (i, 0)))
    for _ in range(2):
        out_shape.append(jax.ShapeDtypeStruct((BATCH, 1, LANES), F32))
        out_specs.append(pl.BlockSpec((CTX_REQ_PER_TILE, 1, LANES), lambda i, l: (i, 0, 0)))
    n_fixed_inputs = 11
    aliases = {n_fixed_inputs + j: j for j in range(len(new_caches))}
    proj_spec = pltpu.PrefetchScalarGridSpec(
        num_scalar_prefetch=1,
        grid=(N_CTX_TOK // TM_CTX,),
        in_specs=[
            pl.BlockSpec((TM_CTX, D_MODEL), lambda i, l: (i, 0)),
            pl.BlockSpec((1, 1, 6, D_MODEL), lambda i, l: (l[0], CTX_MOD_ROW, 0, 0)),
            _layer_spec(p["norm1_g"].shape),
            _layer_spec(p["w_in"].shape),
            _layer_spec(p["mla_q_norm_g"].shape),
            _layer_spec(p["mla_w_qb"].shape),
            _layer_spec(p["mla_kv_norm_g"].shape),
            _layer_spec(p["mla_w_kvb"].shape),
            pl.BlockSpec(p["gq"].shape, lambda i, l: (0, 0)),
            pl.BlockSpec(p["gk"].shape, lambda i, l: (0, 0)),
        ] + [pl.BlockSpec(memory_space=pl.ANY)] * len(new_caches),
        out_specs=out_specs,
    )
    *outs, q, k, v, qmax, kmax = pl.pallas_call(
        functools.partial(_ctx_proj_kernel, first_layer=first_layer),
        grid_spec=proj_spec,
        out_shape=out_shape,
        input_output_aliases=aliases,
        compiler_params=pltpu.CompilerParams(
            dimension_semantics=("arbitrary",), vmem_limit_bytes=VMEM_LIMIT),
        name="ctx_project",
    )(l_idx, x, mod, p["norm1_g"], p["w_in"], p["mla_q_norm_g"], p["mla_w_qb"], p["mla_kv_norm_g"],
      p["mla_w_kvb"], p["gq"], p["gk"], *new_caches)
    assert len(outs) == n_state

    attn_spec = pltpu.PrefetchScalarGridSpec(
        num_scalar_prefetch=1,
        grid=(BATCH,),
        in_specs=[
            pl.BlockSpec((SEQ, Q_W), lambda i, l: (i, 0)),
            pl.BlockSpec((SEQ, K_W), lambda i, l: (i, 0)),
            pl.BlockSpec((SEQ, V_W), lambda i, l: (i, 0)),
            pl.BlockSpec((1, 1, LANES), lambda i, l: (i, 0, 0)),
            pl.BlockSpec((1, 1, LANES), lambda i, l: (i, 0, 0)),
            _layer_spec(p["lamv"].shape),
            _layer_spec(p["subg"].shape),
            pl.BlockSpec(memory_space=pltpu.SMEM),
        ],
        out_specs=pl.BlockSpec((SEQ, D_MODEL), lambda i, l: (i, 0)),
    )
    mix = pl.pallas_call(
        _ctx_attn_kernel,
        grid_spec=attn_spec,
        out_shape=jax.ShapeDtypeStruct((N_CTX_TOK, D_MODEL), BF16),
        compiler_params=pltpu.CompilerParams(
            dimension_semantics=("arbitrary",), vmem_limit_bytes=VMEM_LIMIT),
        name="ctx_attend",
    )(l_idx, q, k, v, qmax, kmax, p["lamv"], p["subg"], p["scal"])
    return (*outs, mix)


def _rope(x, cos, sa, sb, half):
    return x * cos + pltpu.roll(x, LANES - half, 1) * sa + pltpu.roll(x, half, 1) * sb


def _lat_proj_kernel(l_ref, x_ref, mod_ref, n1g_ref, w_in_ref, qg_ref, wqb_ref, kvg_ref, wkvb_ref,
                     c32_ref, a32_ref, b32_ref, c64_ref, a64_ref, b64_ref, cm_ref, am_ref, bm_ref, gq_ref,
                     q_ref, k_ref, v_ref, qmax_ref):
    proj = _project(x_ref, mod_ref, n1g_ref, w_in_ref)
    t32 = (c32_ref[...], a32_ref[...], b32_ref[...], DA_QK // 4)
    t64 = (c64_ref[...], a64_ref[...], b64_ref[...], SW_DIM // 4)
    tml = (cm_ref[...], am_ref[...], bm_ref[...], MLA_ROPE // 4)

    for j in range(2):
        q = _rope(_chunk(proj, j), *t32) * (DA_SCALE * LOG2E)
        q_ref[:, j * LANES:(j + 1) * LANES] = q.astype(BF16)
    for j in range(4):
        q = _rope(_chunk(proj, C_SWQ // LANES + j), *t64) * (SW_SCALE * LOG2E)
        q_ref[:, 256 + j * LANES:256 + (j + 1) * LANES] = q.astype(BF16)
    mq = jnp.dot(_rms(proj[:, C_QA:C_CKV], qg_ref[0]).astype(BF16), wqb_ref[0], preferred_element_type=F32)
    for h in range(MLA_HEADS):
        q = _rope(_chunk(mq, h), *tml) * (MLA_SCALE * LOG2E)
        q_ref[:, O_MQ + h * LANES:O_MQ + (h + 1) * LANES] = q.astype(BF16)
    qmax_ref[0] = _max_group_sq_norm(q_ref[...], gq_ref[...])

    for j in range(2):
        k_ref[:, j * LANES:(j + 1) * LANES] = _rope(_chunk(proj, C_DAK // LANES + j), *t32).astype(BF16)
    k_ref[:, 256:384] = _rope(proj[:, C_SWK:C_SWV], *t64).astype(BF16)
    ckv = _rms(proj[:, C_CKV:C_KR], kvg_ref[0])
    kv = jnp.dot(ckv.astype(BF16), wkvb_ref[0], preferred_element_type=F32)
    krz = _rope(proj[:, C_KR:P_EXT], *tml)
    for h in range(MLA_HEADS):
        k_ref[:, O_MK + h * LANES:O_MK + (h + 1) * LANES] = (_chunk(kv, h) + krz).astype(BF16)

    v_ref[:, 0:256] = proj[:, C_DAV:C_SWQ].astype(BF16)
    v_ref[:, 256:384] = proj[:, C_SWV:C_QA].astype(BF16)
    v_ref[:, 384:640] = kv[:, MLA_CHUNKS:].astype(BF16)


def _lat_project(l_idx, x, mod, p, tables):
    tiles_per_req = DEC_SEQ // TM_LAT
    tab_spec = pl.BlockSpec((TM_LAT, LANES), lambda i, l: (i % tiles_per_req, 0))
    grid_spec = pltpu.PrefetchScalarGridSpec(
        num_scalar_prefetch=1,
        grid=(N_LAT_TOK // TM_LAT,),
        in_specs=[
            pl.BlockSpec((TM_LAT, D_MODEL), lambda i, l: (i, 0)),
            pl.BlockSpec((1, 1, 6, D_MODEL), lambda i, l: (l[0], i // tiles_per_req, 0, 0)),
            _layer_spec(p["norm1_g"].shape),
            _layer_spec(p["w_in"].shape),
            _layer_spec(p["mla_q_norm_g"].shape),
            _layer_spec(p["mla_w_qb"].shape),
            _layer_spec(p["mla_kv_norm_g"].shape),
            _layer_spec(p["mla_w_kvb"].shape),
        ] + [tab_spec] * len(tables) + [pl.BlockSpec(p["gq"].shape, lambda i, l: (0, 0))],
        out_specs=[
            pl.BlockSpec((TM_LAT, Q_W), lambda i, l: (i, 0)),
            pl.BlockSpec((TM_LAT, K_W), lambda i, l: (i, 0)),
            pl.BlockSpec((TM_LAT, V_W), lambda i, l: (i, 0)),
            pl.BlockSpec((1, 1, LANES), lambda i, l: (i, 0, 0)),
        ],
    )
    return pl.pallas_call(
        _lat_proj_kernel,
        grid_spec=grid_spec,
        out_shape=[jax.ShapeDtypeStruct((N_LAT_TOK, Q_W), BF16),
                   jax.ShapeDtypeStruct((N_LAT_TOK, K_W), BF16),
                   jax.ShapeDtypeStruct((N_LAT_TOK, V_W), BF16),
                   jax.ShapeDtypeStruct((N_LAT_TOK // TM_LAT, 1, LANES), F32)],
        compiler_params=pltpu.CompilerParams(
            dimension_semantics=("arbitrary",), vmem_limit_bytes=VMEM_LIMIT),
        name="latent_project",
    )(l_idx, x, mod, p["norm1_g"], p["w_in"], p["mla_q_norm_g"], p["mla_w_qb"], p["mla_kv_norm_g"],
      p["mla_w_kvb"], *tables, p["gq"])


def _lat_attn_kernel(l_ref, q_ref, k_ref, v_ref, cdk_ref, cdv_ref, csk_ref, csv_ref, cckv_ref, ckr_ref,
                     wkvb_ref, pad_ref, lamv_ref, subg_ref, qmax_ref, gk_ref, scal_ref,
                     mix_ref,
                     kda, vda, ksw, vsw, kml, vml, kmax_sq):
    l = l_ref[0]
    qb = pl.program_id(1)

    @pl.when(qb == 0)
    def _stage_keys():
        kda[0:PAST_LEN, :] = cdk_ref[0, 0].astype(BF16)
        kda[PAST_LEN:N_KEYS, :] = k_ref[0, :, 0:256]
        vda[0:PAST_LEN, :] = cdv_ref[0, 0].astype(BF16)
        vda[PAST_LEN:N_KEYS, :] = v_ref[0, :, 0:256]
        sk_lat = k_ref[0, :, 256:384].astype(F32)
        sv_lat = v_ref[0, :, 256:384].astype(F32)
        for g in range(SW_KV_HEADS):
            ksw[0:PAST_LEN, g * LANES:(g + 1) * LANES] = _both_halves(csk_ref[0, 0], g).astype(BF16)
            ksw[PAST_LEN:N_KEYS, g * LANES:(g + 1) * LANES] = _both_halves(sk_lat, g).astype(BF16)
            vsw[0:PAST_LEN, 2 * g * LANES:(2 * g + 1) * LANES] = _both_halves(csv_ref[0, 0], g).astype(BF16)
            vsw[PAST_LEN:N_KEYS, 2 * g * LANES:(2 * g + 1) * LANES] = _both_halves(sv_lat, g).astype(BF16)
            vsw[:, (2 * g + 1) * LANES:(2 * g + 2) * LANES] = jnp.ones((N_KEYS, LANES), BF16)
        kvc = jnp.dot(cckv_ref[0, 0].astype(BF16), wkvb_ref[0], preferred_element_type=F32)
        krz = jnp.dot(ckr_ref[0, 0].astype(BF16), pad_ref[...], preferred_element_type=F32)
        mk_ctx = [_chunk(kvc, h) + krz for h in range(MLA_HEADS)]
        for h in range(MLA_HEADS):
            kml[0:PAST_LEN, h * LANES:(h + 1) * LANES] = mk_ctx[h].astype(BF16)
            v_ctx = _chunk(kvc, MLA_HEADS + h // 2)
            v_lat = v_ref[0, :, 384 + (h // 2) * LANES:384 + (h // 2 + 1) * LANES].astype(F32)
            vml[0:PAST_LEN, h * LANES:(h + 1) * LANES] = _ones_in_other_half(v_ctx, h % 2).astype(BF16)
            vml[PAST_LEN:N_KEYS, h * LANES:(h + 1) * LANES] = _ones_in_other_half(v_lat, h % 2).astype(BF16)
        kml[PAST_LEN:N_KEYS, :] = k_ref[0, :, O_MK:K_W]
        gk = gk_ref[...]
        k_ctx = jnp.concatenate([cdk_ref[0, 0], csk_ref[0, 0]] + mk_ctx, axis=-1)
        kmax_sq[...] = jnp.maximum(_max_group_sq_norm(k_ctx, gk), _max_group_sq_norm(k_ref[0], gk))

    bound_sq = qmax_ref[0] * kmax_sq[...] * BOUND_MARGIN
    small = (jnp.max(bound_sq) <= NOSHIFT_BOUND ** 2) & (scal_ref[l, S_SINK_ABSMAX] <= NOSHIFT_BOUND)

    lam = _diff_lambda(lamv_ref, scal_ref[l, S_LAM_INIT])
    subg = subg_ref[0]
    sinks = [scal_ref[l, h] for h in range(SW_HEADS)]
    q0 = qb * TQ
    start = pl.multiple_of(jnp.clip(q0 - WINDOW, 0, DEC_SEQ - SW_BAND), LANES)

    def attend(shift):
        da_o = _diff_attention(q_ref[0, :, 0:256].astype(F32), lambda: kda[...],
                               lambda c: vda[:, c * LANES:(c + 1) * LANES], lam, shift)
        for c, o in enumerate(da_o):
            mix_ref[0, :, c * LANES:(c + 1) * LANES] = _diff_subln(
                o, _chunk(subg, c), scal_ref[l, S_ONE_MINUS]).astype(BF16)

        band = pl.ds(PAST_LEN + start, SW_BAND)
        k_sel = jnp.concatenate([ksw[0:PAST_LEN, :], ksw[band, :]], axis=0)
        v_sel = jnp.concatenate([vsw[0:PAST_LEN, :], vsw[band, :]], axis=0)
        n_sel = PAST_LEN + SW_BAND
        col = lax.broadcasted_iota(jnp.int32, (TQ, n_sel), 1)
        row = lax.broadcasted_iota(jnp.int32, (TQ, n_sel), 0)
        dist = (q0 + row) - (start + col - PAST_LEN)
        valid = (col < PAST_LEN) | ((dist <= WINDOW) & (dist >= -WINDOW))
        sw_o = _swa_attention(q_ref[0, :, 256:768].astype(F32), lambda g: _chunk(k_sel, g),
                              lambda g: v_sel[:, 2 * g * LANES:(2 * g + 2) * LANES], sinks, shift, valid)
        for c, o in enumerate(sw_o):
            mix_ref[0, :, 256 + c * LANES:256 + (c + 1) * LANES] = o.astype(BF16)

        mla_o = _mla_attention(lambda h: q_ref[0, :, O_MQ + h * LANES:O_MQ + (h + 1) * LANES],
                               lambda h: kml[:, h * LANES:(h + 1) * LANES],
                               lambda h: vml[:, h * LANES:(h + 1) * LANES], shift)
        for c, o in enumerate(mla_o):
            mix_ref[0, :, 768 + c * LANES:768 + (c + 1) * LANES] = o.astype(BF16)

    pl.when(small)(functools.partial(attend, False))
    pl.when(jnp.logical_not(small))(functools.partial(attend, True))


def _lat_attend(l_idx, q, k, v, qmax, caches, p, pad_eye):
    cdk, cdv, csk, csv, cckv, ckr = caches
    proj_tiles_per_req = DEC_SEQ // TM_LAT

    def cache_spec(a):
        return pl.BlockSpec((1, 1) + a.shape[2:], lambda b, j, l: (b, l[0], 0, 0))

    grid_spec = pltpu.PrefetchScalarGridSpec(
        num_scalar_prefetch=1,
        grid=(DEC_BATCH, DEC_SEQ // TQ),
        in_specs=[
            pl.BlockSpec((1, TQ, Q_W), lambda b, j, l: (b, j, 0)),
            pl.BlockSpec((1, DEC_SEQ, K_W), lambda b, j, l: (b, 0, 0)),
            pl.BlockSpec((1, DEC_SEQ, V_W), lambda b, j, l: (b, 0, 0)),
            cache_spec(cdk), cache_spec(cdv), cache_spec(csk), cache_spec(csv), cache_spec(cckv), cache_spec(ckr),
            _layer_spec(p["mla_w_kvb"].shape),
            pl.BlockSpec(pad_eye.shape, lambda b, j, l: (0, 0)),
            _layer_spec(p["lamv"].shape),
            _layer_spec(p["subg"].shape),
            pl.BlockSpec((1, 1, LANES), lambda b, j, l: (b * proj_tiles_per_req + (j * TQ) // TM_LAT, 0, 0)),
            pl.BlockSpec(p["gk"].shape, lambda b, j, l: (0, 0)),
            pl.BlockSpec(memory_space=pltpu.SMEM),
        ],
        out_specs=pl.BlockSpec((1, TQ, D_MODEL), lambda b, j, l: (b, j, 0)),
        scratch_shapes=[
            pltpu.VMEM((N_KEYS, 256), BF16), pltpu.VMEM((N_KEYS, 256), BF16),
            pltpu.VMEM((N_KEYS, SW_KV_HEADS * LANES), BF16), pltpu.VMEM((N_KEYS, SW_KV_HEADS * 2 * LANES), BF16),
            pltpu.VMEM((N_KEYS, MLA_CHUNKS), BF16), pltpu.VMEM((N_KEYS, MLA_CHUNKS), BF16),
            pltpu.VMEM((1, LANES), F32),
        ],
    )
    return pl.pallas_call(
        _lat_attn_kernel,
        grid_spec=grid_spec,
        out_shape=jax.ShapeDtypeStruct((DEC_BATCH, DEC_SEQ, D_MODEL), BF16),
        compiler_params=pltpu.CompilerParams(
            dimension_semantics=("arbitrary", "arbitrary"), vmem_limit_bytes=VMEM_LIMIT),
        name="latent_attend",
    )(l_idx, q.reshape(DEC_BATCH, DEC_SEQ, Q_W), k.reshape(DEC_BATCH, DEC_SEQ, K_W),
      v.reshape(DEC_BATCH, DEC_SEQ, V_W), cdk, cdv, csk, csv, cckv, ckr,
      p["mla_w_kvb"], pad_eye, p["lamv"], p["subg"], qmax, p["gk"], p["scal"])


def _mlp_kernel(l_ref, x_ref, mix_ref, mod_ref, n2g_ref, w_out_ref, w_up_ref, w_down_ref, fg_ref, o_ref,
                *, final_norm):
    m = mod_ref[0, 0]
    x = x_ref[...] + m[2:3] * jnp.dot(mix_ref[...], w_out_ref[0], preferred_element_type=F32)
    h = (_rms(x, n2g_ref[0]) * (1.0 + m[4:5]) + m[3:4]).astype(BF16)
    acc = jnp.zeros_like(x)
    for c in range(D_FF // FF_CHUNK):
        u = jnp.dot(h, w_up_ref[0, :, c * FF_CHUNK:(c + 1) * FF_CHUNK], preferred_element_type=F32)
        a = jnp.square(jnp.maximum(u, 0.0)).astype(BF16)
        acc = acc + jnp.dot(a, w_down_ref[0, c * FF_CHUNK:(c + 1) * FF_CHUNK, :], preferred_element_type=F32)
    x = x + m[5:6] * acc
    if final_norm:
        x = _rms(x, fg_ref[...])
    o_ref[...] = x


def _mlp_layer(l_idx, x, mix, mod, p, final_g, mod_row, final_norm):
    n_tok = x.shape[0]
    grid_spec = pltpu.PrefetchScalarGridSpec(
        num_scalar_prefetch=1,
        grid=(n_tok // TM_MLP,),
        in_specs=[
            pl.BlockSpec((TM_MLP, D_MODEL), lambda i, l: (i, 0)),
            pl.BlockSpec((TM_MLP, D_MODEL), lambda i, l: (i, 0)),
            pl.BlockSpec((1, 1, 6, D_MODEL), lambda i, l: (l[0], mod_row(i), 0, 0)),
            _layer_spec(p["norm2_g"].shape),
            _layer_spec(p["w_out"].shape),
            _layer_spec(p["w_up"].shape),
            _layer_spec(p["w_down"].shape),
            pl.BlockSpec((1, D_MODEL), lambda i, l: (0, 0)),
        ],
        out_specs=pl.BlockSpec((TM_MLP, D_MODEL), lambda i, l: (i, 0)),
    )
    return pl.pallas_call(
        functools.partial(_mlp_kernel, final_norm=final_norm),
        grid_spec=grid_spec,
        out_shape=jax.ShapeDtypeStruct((n_tok, D_MODEL), F32),
        compiler_params=pltpu.CompilerParams(
            dimension_semantics=("arbitrary",), vmem_limit_bytes=VMEM_LIMIT),
        name="out_proj_mlp",
    )(l_idx, x, mix, mod, p["norm2_g"], p["w_out"], p["w_up"], p["w_down"], final_g)


def _rope_tables(period, active_lanes=LANES):
    m = period // 2
    half = m // 2
    t = np.arange(DEC_SEQ)
    pos = np.stack([t // GRID_W, t % GRID_W], axis=0).astype(np.float64)
    lane = np.arange(LANES) % period
    axis = lane // m
    k = lane % m
    freq = ROPE_BASE ** (-(k % half).astype(np.float64) / half)
    ang = pos[axis, :].T * freq[None, :]
    first = (k < half)[None, :]
    active = (np.arange(LANES) < active_lanes)[None, :]
    cos, sin = np.where(active, np.cos(ang), 1.0), np.where(active, np.sin(ang), 0.0)
    return (jnp.asarray(cos, F32), jnp.asarray(np.where(first, -sin, 0.0), F32),
            jnp.asarray(np.where(first, 0.0, sin), F32))


def _group_matrices():
    gq = np.zeros((Q_W, LANES), np.float32)
    gk = np.zeros((K_W, LANES), np.float32)
    for d in range(DA_HEADS * 2 * DA_QK):
        gq[d, G_DIFF + d // DA_QK] = 1.0
        gk[d, G_DIFF + d // DA_QK] = 1.0
    for h in range(SW_HEADS):
        g = h // (SW_HEADS // SW_KV_HEADS)
        gq[256 + h * SW_DIM:256 + (h + 1) * SW_DIM, G_SWA + h] = 1.0
        gk[256 + g * SW_DIM:256 + (g + 1) * SW_DIM, G_SWA + h] = 1.0
    for h in range(MLA_HEADS):
        gq[O_MQ + h * LANES:O_MQ + (h + 1) * LANES, G_MLA + h] = 1.0
        gk[O_MK + h * LANES:O_MK + (h + 1) * LANES, G_MLA + h] = 1.0
    return jnp.asarray(gq, BF16), jnp.asarray(gk, BF16)


def _prepare_params(norm1_g, norm2_g, w_in, w_out, lq1, lk1, lq2, lk2, diff_subln_g, swa_sink,
                    mla_q_norm_g, mla_w_qb, mla_kv_norm_g, mla_w_kvb, w_up, w_down):
    w_in_ext = jnp.pad(w_in.astype(BF16), ((0, 0), (0, 0), (0, P_EXT - w_in.shape[-1])))
    w_out_p = w_out.astype(BF16)
    dq = MLA_NOPE + MLA_ROPE
    gap = LANES - dq
    zq = jnp.zeros((DEPTH, MLA_Q_RANK, gap), mla_w_qb.dtype)
    wqb = jnp.concatenate(
        [blk for h in range(MLA_HEADS)
         for blk in (mla_w_qb[:, :, h * dq + MLA_NOPE:(h + 1) * dq], zq, mla_w_qb[:, :, h * dq:h * dq + MLA_NOPE])],
        axis=-1).astype(BF16)
    dkv = MLA_NOPE + MLA_V
    zk = jnp.zeros((DEPTH, MLA_KV_RANK, LANES - MLA_NOPE), mla_w_kvb.dtype)
    wkvb = jnp.concatenate(
        [blk for h in range(MLA_HEADS) for blk in (zk, mla_w_kvb[:, :, h * dkv:h * dkv + MLA_NOPE])]
        + [mla_w_kvb[:, :, h * dkv + MLA_NOPE:(h + 1) * dkv] for h in range(MLA_HEADS)],
        axis=-1).astype(BF16)
    lam_init = np.array([0.8 - 0.6 * math.exp(-0.3 * l) for l in range(DEPTH)], np.float32)
    sink_l2 = swa_sink.astype(F32) * LOG2E
    scal = jnp.concatenate([
        sink_l2,
        jnp.asarray(np.stack([lam_init, (1.0 - lam_init.astype(np.float64)).astype(np.float32)], axis=1)),
        jnp.max(jnp.abs(sink_l2), axis=1, keepdims=True),
        jnp.zeros((DEPTH, S_COLS - SW_HEADS - 3), F32)], axis=1)
    gq, gk = _group_matrices()
    return {
        "gq": gq,
        "gk": gk,
        "norm1_g": norm1_g.reshape(DEPTH, 1, D_MODEL),
        "norm2_g": norm2_g.reshape(DEPTH, 1, D_MODEL),
        "w_in": w_in_ext,
        "w_out": w_out_p,
        "mla_q_norm_g": mla_q_norm_g.reshape(DEPTH, 1, MLA_Q_RANK),
        "mla_w_qb": wqb,
        "mla_kv_norm_g": mla_kv_norm_g.reshape(DEPTH, 1, MLA_KV_RANK),
        "mla_w_kvb": wkvb,
        "lamv": jnp.stack([lq1, lk1, lq2, lk2], axis=1),
        "subg": jnp.tile(diff_subln_g, (1, DA_HEADS)).reshape(DEPTH, 1, DA_HEADS * DA_V),
        "scal": scal,
        "w_up": w_up.astype(BF16),
        "w_down": w_down.astype(BF16),
    }


def kernel(x_prompt, x_sample, c, cache_diff_k, cache_diff_v, cache_swa_k, cache_swa_v, cache_mla_ckv,
           cache_mla_krope, c_ctx, norm1_g, norm2_g, w_ada, b_ada, w_in, w_out, diff_lambda_q1,
           diff_lambda_k1, diff_lambda_q2, diff_lambda_k2, diff_subln_g, swa_sink, mla_q_norm_g,
           mla_w_qb, mla_kv_norm_g, mla_w_kvb, w_up, w_down, final_g):
    p = _prepare_params(norm1_g, norm2_g, w_in, w_out, diff_lambda_q1, diff_lambda_k1, diff_lambda_q2,
                        diff_lambda_k2, diff_subln_g, swa_sink, mla_q_norm_g, mla_w_qb, mla_kv_norm_g,
                        mla_w_kvb, w_up, w_down)
    cvecs = jnp.concatenate([c, c_ctx[None, :], jnp.zeros((8 - DEC_BATCH - 1, D_MODEL), F32)], axis=0)
    mod = _modulation(cvecs, w_ada, b_ada)
    tables = _rope_tables(DA_QK) + _rope_tables(SW_DIM) + _rope_tables(MLA_ROPE, MLA_ROPE)
    pad_eye = jnp.asarray(np.eye(MLA_ROPE, LANES, dtype=np.float32), BF16)
    final_g2 = final_g.reshape(1, D_MODEL)
    caches = (cache_diff_k.reshape(DEC_BATCH, DEPTH, PAST_LEN, DA_HEADS * 2 * DA_QK),
              cache_diff_v.reshape(DEC_BATCH, DEPTH, PAST_LEN, DA_HEADS * DA_V),
              cache_swa_k.reshape(DEC_BATCH, DEPTH, PAST_LEN, SW_KV_HEADS * SW_DIM),
              cache_swa_v.reshape(DEC_BATCH, DEPTH, PAST_LEN, SW_KV_HEADS * SW_DIM),
              cache_mla_ckv, cache_mla_krope)

    x_ctx = x_prompt.reshape(N_CTX_TOK, D_MODEL)
    x_lat = x_sample.reshape(N_LAT_TOK, D_MODEL)
    new = None
    lat_tiles_per_req = DEC_SEQ // TM_MLP
    for l in range(DEPTH):
        l_idx = jnp.full((1,), l, jnp.int32)
        last = l == DEPTH - 1
        *new, mix_ctx = _ctx_layer(l_idx, x_ctx, mod, p, new)
        x_ctx = _mlp_layer(l_idx, x_ctx, mix_ctx, mod, p, final_g2, lambda i: CTX_MOD_ROW, last)
        q, k, v, qmax = _lat_project(l_idx, x_lat, mod, p, tables)
        mix_lat = _lat_attend(l_idx, q, k, v, qmax, caches, p, pad_eye).reshape(N_LAT_TOK, D_MODEL)
        x_lat = _mlp_layer(l_idx, x_lat, mix_lat, mod, p, final_g2, lambda i: i // lat_tiles_per_req, last)

    y_prompt = x_ctx.reshape(BATCH, SEQ, D_MODEL)
    y_sample = x_lat.reshape(DEC_BATCH, DEC_SEQ, D_MODEL)
    new_dk, new_dv, new_sk, new_sv, new_ckv, new_kr = new
    return (y_prompt, y_sample,
            new_dk.reshape(BATCH, DEPTH, SEQ, DA_HEADS, 2 * DA_QK),
            new_dv.reshape(BATCH, DEPTH, SEQ, DA_HEADS, DA_V),
            new_sk.reshape(BATCH, DEPTH, SEQ, SW_KV_HEADS, SW_DIM),
            new_sv.reshape(BATCH, DEPTH, SEQ, SW_KV_HEADS, SW_DIM),
            new_ckv, new_kr)
```

```python
import functools
import math

import numpy as np
import jax
import jax.numpy as jnp
from jax import lax
from jax.experimental import pallas as pl
from jax.experimental.pallas import tpu as pltpu

F32 = jnp.float32
BF16 = jnp.bfloat16

D_MODEL = 1024
BATCH = 32
SEQ = 256
DEPTH = 4
DEC_BATCH = 4
DEC_SEQ = 2048
PAST_LEN = 256
GRID_W = 64
WINDOW = 128
ROPE_BASE = 10000.0
EPS = 1e-6
DA_HEADS = 4
DA_QK = 32
DA_V = 64
SW_HEADS = 8
SW_KV_HEADS = 2
SW_DIM = 64
MLA_HEADS = 4
MLA_Q_RANK = 256
MLA_KV_RANK = 128
MLA_NOPE = 64
MLA_ROPE = 32
MLA_V = 64
D_FF = 4 * D_MODEL

LANES = 128
HALF = LANES // 2
N_CTX_TOK = BATCH * SEQ
N_LAT_TOK = DEC_BATCH * DEC_SEQ
N_KEYS = PAST_LEN + DEC_SEQ
CTX_MOD_ROW = DEC_BATCH

C_DAQ, C_DAK, C_DAV, C_SWQ, C_SWK, C_SWV, C_QA, C_CKV, C_KR, P_EXT = (
    0, 256, 512, 768, 1280, 1408, 1536, 1792, 1920, 2048)
MLA_CHUNKS = MLA_HEADS * 128
Q_W = 256 + 512 + MLA_CHUNKS
K_W = 256 + 128 + MLA_CHUNKS
V_W = 256 + 128 + 256
O_MQ = 768
O_MK = 384
LOG2E = 1.4426950408889634

DA_SCALE = DA_QK ** -0.5
SW_SCALE = SW_DIM ** -0.5
MLA_SCALE = (MLA_NOPE + MLA_ROPE) ** -0.5

CTX_REQ_PER_TILE = 2
TM_CTX = CTX_REQ_PER_TILE * SEQ
TM_LAT = 512
TM_MLP = 512
TQ = 256
MLA_KEY_SLABS = 3
SW_BAND = TQ + 2 * WINDOW
FF_CHUNK = 1024
MOD_TILE = 1536
VMEM_LIMIT = 56 * 1024 * 1024

_NT = (((1,), (1,)), ((), ()))

S_LAM_INIT = SW_HEADS
S_ONE_MINUS = SW_HEADS + 1
S_SINK_ABSMAX = SW_HEADS + 2
S_COLS = 16

NOSHIFT_BOUND = 50.0
G_DIFF, G_SWA, G_MLA = 0, 8, 16


def _scores(q, k):
    return lax.dot_general(q, k, _NT, preferred_element_type=F32)


def _rms(x, g):
    ms = jnp.mean(x * x, axis=-1, keepdims=True)
    return x * lax.rsqrt(ms + EPS) * g


def _lane_mask(n, lo, width):
    lane = lax.broadcasted_iota(jnp.int32, (1, n), 1)
    return (lane >= lo) & (lane < lo + width)


def _exp_scores(s, shift, sink=None):
    if not shift:
        return jnp.exp2(s), jnp.zeros((1, 1), F32)
    m = jnp.max(s, axis=-1, keepdims=True)
    if sink is not None:
        m = jnp.maximum(m, sink)
    return jnp.exp2(s - m), m


def _sq_bf16(x):
    xf = x.astype(F32)
    return (xf * xf).astype(BF16)


def _max_group_sq_norm(x, g):
    return jnp.max(jnp.dot(_sq_bf16(x), g, preferred_element_type=F32), axis=0, keepdims=True)


BOUND_MARGIN = 1.03


def _chunk(a, j):
    return a[:, j * LANES:(j + 1) * LANES]


def _project(x_ref, mod_ref, n1g_ref, w_in_ref):
    m = mod_ref[0, 0]
    h = _rms(x_ref[...], n1g_ref[0]) * (1.0 + m[1:2]) + m[0:1]
    return jnp.dot(h.astype(BF16), w_in_ref[0], preferred_element_type=F32)


def _diff_lambda(lamv_ref, lam_init):
    lv = lamv_ref[0]
    a = jnp.sum(lv[0:1] * lv[1:2], axis=-1, keepdims=True)
    b = jnp.sum(lv[2:3] * lv[3:4], axis=-1, keepdims=True)
    return jnp.exp(a) - jnp.exp(b) + lam_init


def _half_mask(par):
    lane = lax.broadcasted_iota(jnp.int32, (1, LANES), 1)
    return (lane >= par * HALF) & (lane < (par + 1) * HALF)


def _diff_attention(qf, load_k, load_v, lam, shift):
    width = DA_HEADS * DA_V
    chunks = []
    for c in range(DA_HEADS // 2):
        o_par = []
        for par in range(2):
            lo = (2 * c + par) * DA_V
            q1 = jnp.where(_lane_mask(width, lo, DA_QK), qf, 0.0).astype(BF16)
            q2 = jnp.where(_lane_mask(width, lo + DA_QK, DA_QK), qf, 0.0).astype(BF16)
            e1, _ = _exp_scores(_scores(q1, load_k()), shift)
            e2, _ = _exp_scores(_scores(q2, load_k()), shift)
            d1 = jnp.sum(e1, axis=-1, keepdims=True)
            d2 = jnp.sum(e2, axis=-1, keepdims=True)
            w = (e1 - e2 * (lam * d1 * (1.0 / d2))).astype(BF16)
            o_par.append(jnp.dot(w, load_v(c), preferred_element_type=F32) * (1.0 / d1))
        chunks.append(jnp.where(_half_mask(0), o_par[0], o_par[1]))
    return chunks


def _diff_subln(o, subg, one_minus_lam_init):
    sq = o * o
    r = None
    for par in range(2):
        ms = jnp.sum(jnp.where(_half_mask(par), sq, 0.0), axis=-1, keepdims=True) * (1.0 / DA_V)
        rs = lax.rsqrt(ms + EPS)
        r = rs if r is None else jnp.where(_half_mask(0), r, rs)
    return o * r * subg * one_minus_lam_init


def _both_halves(x, g):
    swapped = pltpu.roll(x, HALF, 1)
    return jnp.where(_half_mask(g), x, swapped)


def _swa_attention(qf, load_k, load_v, sinks, shift, valid=None):
    heads_per_group = SW_HEADS // SW_KV_HEADS
    outs = []
    for c in range(SW_HEADS // 2):
        qc = _chunk(qf, c)
        g = (2 * c) // heads_per_group
        o_par = []
        for par in range(2):
            h = 2 * c + par
            qm = jnp.where(_half_mask(par), qc, 0.0).astype(BF16)
            s = _scores(qm, load_k(g))
            if valid is not None:
                s = jnp.where(valid, s, -1e30)
            e, m = _exp_scores(s, shift, sink=sinks[h])
            o = jnp.dot(e.astype(BF16), load_v(g), preferred_element_type=F32)
            d = _chunk(o, 1) + jnp.exp2(sinks[h] - m)
            o_par.append(_chunk(o, 0) * (1.0 / d))
        outs.append(jnp.where(_half_mask(0), o_par[0], o_par[1]))
    return outs


def _mla_attention(load_q, load_k, load_v, shift, n_keys, key_chunks=1):
    slabs = key_chunks if not shift else 1
    rows_per_slab = n_keys // slabs
    chunks = []
    for c in range(MLA_HEADS // 2):
        o_par = []
        for par in range(2):
            h = 2 * c + par
            o = None
            for j in range(slabs):
                rows = slice(j * rows_per_slab, (j + 1) * rows_per_slab)
                e, _ = _exp_scores(_scores(load_q(h), load_k(h, rows)), shift)
                part = jnp.dot(e.astype(BF16), load_v(h, rows), preferred_element_type=F32)
                o = part if o is None else o + part
            o_par.append(o * (1.0 / pltpu.roll(o, HALF, 1)))
        chunks.append(jnp.where(_half_mask(0), o_par[0], o_par[1]))
    return chunks


def _ones_in_other_half(v, par):
    return jnp.where(_half_mask(par), v, 1.0)


def _mod_kernel(c_ref, w_ref, b_ref, o_ref):
    c = c_ref[...]
    s = c * (1.0 / (1.0 + jnp.exp(-c)))
    o_ref[0] = jnp.dot(s.astype(BF16), w_ref[0].astype(BF16), preferred_element_type=F32) + b_ref[0]


def _modulation(cvecs, w_ada, b_ada):
    n_out = 6 * D_MODEL
    out = pl.pallas_call(
        _mod_kernel,
        grid=(DEPTH, n_out // MOD_TILE),
        in_specs=[
            pl.BlockSpec((8, D_MODEL), lambda l, j: (0, 0)),
            pl.BlockSpec((1, D_MODEL, MOD_TILE), lambda l, j: (l, 0, j)),
            pl.BlockSpec((1, 1, MOD_TILE), lambda l, j: (l, 0, j)),
        ],
        out_specs=pl.BlockSpec((1, 8, MOD_TILE), lambda l, j: (l, 0, j)),
        out_shape=jax.ShapeDtypeStruct((DEPTH, 8, n_out), F32),
        compiler_params=pltpu.CompilerParams(
            dimension_semantics=("arbitrary", "arbitrary"), vmem_limit_bytes=VMEM_LIMIT),
        name="adaln_modulation",
    )(cvecs, w_ada, b_ada.reshape(DEPTH, 1, n_out))
    return out.reshape(DEPTH, 8, 6, D_MODEL)


def _ctx_proj_kernel(l_ref, x_ref, mod_ref, n1g_ref, w_in_ref, qg_ref, wqb_ref, kvg_ref, wkvb_ref, gq_ref, gk_ref,
                     *rest, first_layer):
    dk_ref, dv_ref, sk_ref, sv_ref, ckv_ref, kr_ref, q_ref, k_ref, v_ref, qmax_ref, kmax_ref = rest[-11:]
    if first_layer:
        for ref in (dk_ref, dv_ref, sk_ref, sv_ref, ckv_ref, kr_ref):
            ref[:, 1:] = jnp.zeros((CTX_REQ_PER_TILE, DEPTH - 1) + ref.shape[2:], F32)
    proj = _project(x_ref, mod_ref, n1g_ref, w_in_ref)
    ckv = _rms(proj[:, C_CKV:C_KR], kvg_ref[0])
    mq = jnp.dot(_rms(proj[:, C_QA:C_CKV], qg_ref[0]).astype(BF16), wqb_ref[0], preferred_element_type=F32)
    kv = jnp.dot(ckv.astype(BF16), wkvb_ref[0], preferred_element_type=F32)
    krz = proj[:, C_KR:P_EXT]
    for r in range(CTX_REQ_PER_TILE):
        r0, r1 = r * SEQ, (r + 1) * SEQ
        dk_ref[r, 0] = proj[r0:r1, C_DAK:C_DAV]
        dv_ref[r, 0] = proj[r0:r1, C_DAV:C_SWQ]
        sk_ref[r, 0] = proj[r0:r1, C_SWK:C_SWV]
        sv_ref[r, 0] = proj[r0:r1, C_SWV:C_QA]
        ckv_ref[r, 0] = ckv[r0:r1]
        kr_ref[r, 0] = krz[r0:r1, 0:MLA_ROPE]

    q_ref[:, 0:256] = (proj[:, C_DAQ:C_DAK] * (DA_SCALE * LOG2E)).astype(BF16)
    q_ref[:, 256:O_MQ] = (proj[:, C_SWQ:C_SWK] * (SW_SCALE * LOG2E)).astype(BF16)
    q_ref[:, O_MQ:Q_W] = (mq * (MLA_SCALE * LOG2E)).astype(BF16)
    k_ref[:, 0:256] = proj[:, C_DAK:C_DAV].astype(BF16)
    k_ref[:, 256:O_MK] = proj[:, C_SWK:C_SWV].astype(BF16)
    for h in range(MLA_HEADS):
        k_ref[:, O_MK + h * LANES:O_MK + (h + 1) * LANES] = (_chunk(kv, h) + krz).astype(BF16)
    v_ref[:, 0:256] = proj[:, C_DAV:C_SWQ].astype(BF16)
    v_ref[:, 256:384] = proj[:, C_SWV:C_QA].astype(BF16)
    v_ref[:, 384:V_W] = kv[:, MLA_CHUNKS:].astype(BF16)
    for r in range(CTX_REQ_PER_TILE):
        r0, r1 = r * SEQ, (r + 1) * SEQ
        qmax_ref[r] = _max_group_sq_norm(q_ref[r0:r1, :], gq_ref[...])
        kmax_ref[r] = _max_group_sq_norm(k_ref[r0:r1, :], gk_ref[...])


def _ctx_attn_kernel(l_ref, q_ref, k_ref, v_ref, qmax_ref, kmax_ref, lamv_ref, subg_ref, scal_ref, mix_ref):
    l = l_ref[0]
    bound_sq = qmax_ref[0] * kmax_ref[0] * BOUND_MARGIN
    small = (jnp.max(bound_sq) <= NOSHIFT_BOUND ** 2) & (scal_ref[l, S_SINK_ABSMAX] <= NOSHIFT_BOUND)
    lam = _diff_lambda(lamv_ref, scal_ref[l, S_LAM_INIT])
    subg = subg_ref[0]
    sinks = [scal_ref[l, h] for h in range(SW_HEADS)]

    def attend(shift):
        da_o = _diff_attention(q_ref[:, 0:256].astype(F32), lambda: k_ref[:, 0:256],
                               lambda c: v_ref[:, c * LANES:(c + 1) * LANES], lam, shift)
        for c, o in enumerate(da_o):
            mix_ref[:, c * LANES:(c + 1) * LANES] = _diff_subln(
                o, _chunk(subg, c), scal_ref[l, S_ONE_MINUS]).astype(BF16)
        sw_k = k_ref[:, 256:O_MK].astype(F32)
        sw_v = v_ref[:, 256:384].astype(F32)
        sw_kg = [_both_halves(sw_k, g).astype(BF16) for g in range(SW_KV_HEADS)]
        sw_vg = [jnp.concatenate([_both_halves(sw_v, g), jnp.ones_like(sw_v)], axis=-1).astype(BF16)
                 for g in range(SW_KV_HEADS)]
        sw_o = _swa_attention(q_ref[:, 256:O_MQ].astype(F32), lambda g: sw_kg[g], lambda g: sw_vg[g], sinks, shift)
        for c, o in enumerate(sw_o):
            mix_ref[:, 256 + c * LANES:256 + (c + 1) * LANES] = o.astype(BF16)
        mla_o = _mla_attention(
            lambda h: q_ref[:, O_MQ + h * LANES:O_MQ + (h + 1) * LANES],
            lambda h, rows: k_ref[rows, O_MK + h * LANES:O_MK + (h + 1) * LANES],
            lambda h, rows: _ones_in_other_half(
                v_ref[rows, 384 + (h // 2) * LANES:384 + (h // 2 + 1) * LANES].astype(F32), h % 2).astype(BF16),
            shift, SEQ)
        for c, o in enumerate(mla_o):
            mix_ref[:, 768 + c * LANES:768 + (c + 1) * LANES] = o.astype(BF16)

    pl.when(small)(functools.partial(attend, False))
    pl.when(jnp.logical_not(small))(functools.partial(attend, True))


def _layer_spec(shape):
    nd = len(shape)
    return pl.BlockSpec((1,) + tuple(shape[1:]), lambda *a: (a[-1][0],) + (0,) * (nd - 1))


CACHE_WIDTHS = (DA_HEADS * 2 * DA_QK, DA_HEADS * DA_V, SW_KV_HEADS * SW_DIM, SW_KV_HEADS * SW_DIM,
                MLA_KV_RANK, MLA_ROPE)


def _ctx_layer(l_idx, x, mod, p, new_caches):
    first_layer = new_caches is None
    if first_layer:
        new_caches = []
        out_shape = [jax.ShapeDtypeStruct((BATCH, DEPTH, SEQ, w), F32) for w in CACHE_WIDTHS]
        out_specs = [pl.BlockSpec((CTX_REQ_PER_TILE, DEPTH, SEQ, w), lambda i, l: (i, 0, 0, 0)) for w in CACHE_WIDTHS]
    else:
        out_shape = [jax.ShapeDtypeStruct(c.shape, c.dtype) for c in new_caches]
        out_specs = [pl.BlockSpec((CTX_REQ_PER_TILE, 1, SEQ, w), lambda i, l: (i, l[0], 0, 0)) for w in CACHE_WIDTHS]
    n_state = len(CACHE_WIDTHS)
    for w in (Q_W, K_W, V_W):
        out_shape.append(jax.ShapeDtypeStruct((N_CTX_TOK, w), BF16))
        out_specs.append(pl.BlockSpec((TM_CTX, w), lambda i, l: (i, 0)))
    for _ in range(2):
        out_shape.append(jax.ShapeDtypeStruct((BATCH, 1, LANES), F32))
        out_specs.append(pl.BlockSpec((CTX_REQ_PER_TILE, 1, LANES), lambda i, l: (i, 0, 0)))
    n_fixed_inputs = 11
    aliases = {n_fixed_inputs + j: j for j in range(len(new_caches))}
    proj_spec = pltpu.PrefetchScalarGridSpec(
        num_scalar_prefetch=1,
        grid=(N_CTX_TOK // TM_CTX,),
        in_specs=[
            pl.BlockSpec((TM_CTX, D_MODEL), lambda i, l: (i, 0)),
            pl.BlockSpec((1, 1, 6, D_MODEL), lambda i, l: (l[0], CTX_MOD_ROW, 0, 0)),
            _layer_spec(p["norm1_g"].shape),
            _layer_spec(p["w_in"].shape),
            _layer_spec(p["mla_q_norm_g"].shape),
            _layer_spec(p["mla_w_qb"].shape),
            _layer_spec(p["mla_kv_norm_g"].shape),
            _layer_spec(p["mla_w_kvb"].shape),
            pl.BlockSpec(p["gq"].shape, lambda i, l: (0, 0)),
            pl.BlockSpec(p["gk"].shape, lambda i, l: (0, 0)),
        ] + [pl.BlockSpec(memory_space=pl.ANY)] * len(new_caches),
        out_specs=out_specs,
    )
    *outs, q, k, v, qmax, kmax = pl.pallas_call(
        functools.partial(_ctx_proj_kernel, first_layer=first_layer),
        grid_spec=proj_spec,
        out_shape=out_shape,
        input_output_aliases=aliases,
        compiler_params=pltpu.CompilerParams(
            dimension_semantics=("arbitrary",), vmem_limit_bytes=VMEM_LIMIT),
        name="ctx_project",
    )(l_idx, x, mod, p["norm1_g"], p["w_in"], p["mla_q_norm_g"], p["mla_w_qb"], p["mla_kv_norm_g"],
      p["mla_w_kvb"], p["gq"], p["gk"], *new_caches)
    assert len(outs) == n_state

    attn_spec = pltpu.PrefetchScalarGridSpec(
        num_scalar_prefetch=1,
        grid=(BATCH,),
        in_specs=[
            pl.BlockSpec((SEQ, Q_W), lambda i, l: (i, 0)),
            pl.BlockSpec((SEQ, K_W), lambda i, l: (i, 0)),
            pl.BlockSpec((SEQ, V_W), lambda i, l: (i, 0)),
            pl.BlockSpec((1, 1, LANES), lambda i, l: (i, 0, 0)),
            pl.BlockSpec((1, 1, LANES), lambda i, l: (i, 0, 0)),
            _layer_spec(p["lamv"].shape),
            _layer_spec(p["subg"].shape),
            pl.BlockSpec(memory_space=pltpu.SMEM),
        ],
        out_specs=pl.BlockSpec((SEQ, D_MODEL), lambda i, l: (i, 0)),
    )
    mix = pl.pallas_call(
        _ctx_attn_kernel,
        grid_spec=attn_spec,
        out_shape=jax.ShapeDtypeStruct((N_CTX_TOK, D_MODEL), BF16),
        compiler_params=pltpu.CompilerParams(
            dimension_semantics=("arbitrary",), vmem_limit_bytes=VMEM_LIMIT),
        name="ctx_attend",
    )(l_idx, q, k, v, qmax, kmax, p["lamv"], p["subg"], p["scal"])
    return (*outs, mix)


def _rope(x, cos, sa, sb, half):
    return x * cos + pltpu.roll(x, LANES - half, 1) * sa + pltpu.roll(x, half, 1) * sb


def _lat_proj_kernel(l_ref, x_ref, mod_ref, n1g_ref, w_in_ref, qg_ref, wqb_ref, kvg_ref, wkvb_ref,
                     c32_ref, a32_ref, b32_ref, c64_ref, a64_ref, b64_ref, cm_ref, am_ref, bm_ref, gq_ref,
                     q_ref, k_ref, v_ref, qmax_ref):
    proj = _project(x_ref, mod_ref, n1g_ref, w_in_ref)
    t32 = (c32_ref[...], a32_ref[...], b32_ref[...], DA_QK // 4)
    t64 = (c64_ref[...], a64_ref[...], b64_ref[...], SW_DIM // 4)
    tml = (cm_ref[...], am_ref[...], bm_ref[...], MLA_ROPE // 4)

    for j in range(2):
        q = _rope(_chunk(proj, j), *t32) * (DA_SCALE * LOG2E)
        q_ref[:, j * LANES:(j + 1) * LANES] = q.astype(BF16)
    for j in range(4):
        q = _rope(_chunk(proj, C_SWQ // LANES + j), *t64) * (SW_SCALE * LOG2E)
        q_ref[:, 256 + j * LANES:256 + (j + 1) * LANES] = q.astype(BF16)
    mq = jnp.dot(_rms(proj[:, C_QA:C_CKV], qg_ref[0]).astype(BF16), wqb_ref[0], preferred_element_type=F32)
    for h in range(MLA_HEADS):
        q = _rope(_chunk(mq, h), *tml) * (MLA_SCALE * LOG2E)
        q_ref[:, O_MQ + h * LANES:O_MQ + (h + 1) * LANES] = q.astype(BF16)
    qmax_ref[0] = _max_group_sq_norm(q_ref[...], gq_ref[...])

    for j in range(2):
        k_ref[:, j * LANES:(j + 1) * LANES] = _rope(_chunk(proj, C_DAK // LANES + j), *t32).astype(BF16)
    k_ref[:, 256:384] = _rope(proj[:, C_SWK:C_SWV], *t64).astype(BF16)
    ckv = _rms(proj[:, C_CKV:C_KR], kvg_ref[0])
    kv = jnp.dot(ckv.astype(BF16), wkvb_ref[0], preferred_element_type=F32)
    krz = _rope(proj[:, C_KR:P_EXT], *tml)
    for h in range(MLA_HEADS):
        k_ref[:, O_MK + h * LANES:O_MK + (h + 1) * LANES] = (_chunk(kv, h) + krz).astype(BF16)

    v_ref[:, 0:256] = proj[:, C_DAV:C_SWQ].astype(BF16)
    v_ref[:, 256:384] = proj[:, C_SWV:C_QA].astype(BF16)
    v_ref[:, 384:640] = kv[:, MLA_CHUNKS:].astype(BF16)


def _lat_project(l_idx, x, mod, p, tables):
    tiles_per_req = DEC_SEQ // TM_LAT
    tab_spec = pl.BlockSpec((TM_LAT, LANES), lambda i, l: (i % tiles_per_req, 0))
    grid_spec = pltpu.PrefetchScalarGridSpec(
        num_scalar_prefetch=1,
        grid=(N_LAT_TOK // TM_LAT,),
        in_specs=[
            pl.BlockSpec((TM_LAT, D_MODEL), lambda i, l: (i, 0)),
            pl.BlockSpec((1, 1, 6, D_MODEL), lambda i, l: (l[0], i // tiles_per_req, 0, 0)),
            _layer_spec(p["norm1_g"].shape),
            _layer_spec(p["w_in"].shape),
            _layer_spec(p["mla_q_norm_g"].shape),
            _layer_spec(p["mla_w_qb"].shape),
            _layer_spec(p["mla_kv_norm_g"].shape),
            _layer_spec(p["mla_w_kvb"].shape),
        ] + [tab_spec] * len(tables) + [pl.BlockSpec(p["gq"].shape, lambda i, l: (0, 0))],
        out_specs=[
            pl.BlockSpec((TM_LAT, Q_W), lambda i, l: (i, 0)),
            pl.BlockSpec((TM_LAT, K_W), lambda i, l: (i, 0)),
            pl.BlockSpec((TM_LAT, V_W), lambda i, l: (i, 0)),
            pl.BlockSpec((1, 1, LANES), lambda i, l: (i, 0, 0)),
        ],
    )
    return pl.pallas_call(
        _lat_proj_kernel,
        grid_spec=grid_spec,
        out_shape=[jax.ShapeDtypeStruct((N_LAT_TOK, Q_W), BF16),
                   jax.ShapeDtypeStruct((N_LAT_TOK, K_W), BF16),
                   jax.ShapeDtypeStruct((N_LAT_TOK, V_W), BF16),
                   jax.ShapeDtypeStruct((N_LAT_TOK // TM_LAT, 1, LANES), F32)],
        compiler_params=pltpu.CompilerParams(
            dimension_semantics=("arbitrary",), vmem_limit_bytes=VMEM_LIMIT),
        name="latent_project",
    )(l_idx, x, mod, p["norm1_g"], p["w_in"], p["mla_q_norm_g"], p["mla_w_qb"], p["mla_kv_norm_g"],
      p["mla_w_kvb"], *tables, p["gq"])


def _lat_attn_kernel(l_ref, q_ref, k_ref, v_ref, cdk_ref, cdv_ref, csk_ref, csv_ref, cckv_ref, ckr_ref,
                     wkvb_ref, pad_ref, lamv_ref, subg_ref, qmax_ref, gk_ref, scal_ref,
                     mix_ref,
                     kda, vda, ksw, vsw, kml, vml, kmax_sq):
    l = l_ref[0]
    qb = pl.program_id(1)

    @pl.when(qb == 0)
    def _stage_keys():
        kda[0:PAST_LEN, :] = cdk_ref[0, 0].astype(BF16)
        kda[PAST_LEN:N_KEYS, :] = k_ref[0, :, 0:256]
        vda[0:PAST_LEN, :] = cdv_ref[0, 0].astype(BF16)
        vda[PAST_LEN:N_KEYS, :] = v_ref[0, :, 0:256]
        sk_lat = k_ref[0, :, 256:384].astype(F32)
        sv_lat = v_ref[0, :, 256:384].astype(F32)
        for g in range(SW_KV_HEADS):
            ksw[0:PAST_LEN, g * LANES:(g + 1) * LANES] = _both_halves(csk_ref[0, 0], g).astype(BF16)
            ksw[PAST_LEN:N_KEYS, g * LANES:(g + 1) * LANES] = _both_halves(sk_lat, g).astype(BF16)
            vsw[0:PAST_LEN, 2 * g * LANES:(2 * g + 1) * LANES] = _both_halves(csv_ref[0, 0], g).astype(BF16)
            vsw[PAST_LEN:N_KEYS, 2 * g * LANES:(2 * g + 1) * LANES] = _both_halves(sv_lat, g).astype(BF16)
            vsw[:, (2 * g + 1) * LANES:(2 * g + 2) * LANES] = jnp.ones((N_KEYS, LANES), BF16)
        kvc = jnp.dot(cckv_ref[0, 0].astype(BF16), wkvb_ref[0], preferred_element_type=F32)
        krz = jnp.dot(ckr_ref[0, 0].astype(BF16), pad_ref[...], preferred_element_type=F32)
        mk_ctx = [_chunk(kvc, h) + krz for h in range(MLA_HEADS)]
        for h in range(MLA_HEADS):
            kml[0:PAST_LEN, h * LANES:(h + 1) * LANES] = mk_ctx[h].astype(BF16)
            v_ctx = _chunk(kvc, MLA_HEADS + h // 2)
            v_lat = v_ref[0, :, 384 + (h // 2) * LANES:384 + (h // 2 + 1) * LANES].astype(F32)
            vml[0:PAST_LEN, h * LANES:(h + 1) * LANES] = _ones_in_other_half(v_ctx, h % 2).astype(BF16)
            vml[PAST_LEN:N_KEYS, h * LANES:(h + 1) * LANES] = _ones_in_other_half(v_lat, h % 2).astype(BF16)
        kml[PAST_LEN:N_KEYS, :] = k_ref[0, :, O_MK:K_W]
        gk = gk_ref[...]
        k_ctx = jnp.concatenate([cdk_ref[0, 0], csk_ref[0, 0]] + mk_ctx, axis=-1)
        kmax_sq[...] = jnp.maximum(_max_group_sq_norm(k_ctx, gk), _max_group_sq_norm(k_ref[0], gk))

    bound_sq = qmax_ref[0] * kmax_sq[...] * BOUND_MARGIN
    small = (jnp.max(bound_sq) <= NOSHIFT_BOUND ** 2) & (scal_ref[l, S_SINK_ABSMAX] <= NOSHIFT_BOUND)

    lam = _diff_lambda(lamv_ref, scal_ref[l, S_LAM_INIT])
    subg = subg_ref[0]
    sinks = [scal_ref[l, h] for h in range(SW_HEADS)]
    q0 = qb * TQ
    start = pl.multiple_of(jnp.clip(q0 - WINDOW, 0, DEC_SEQ - SW_BAND), LANES)

    def attend(shift):
        da_o = _diff_attention(q_ref[0, :, 0:256].astype(F32), lambda: kda[...],
                               lambda c: vda[:, c * LANES:(c + 1) * LANES], lam, shift)
        for c, o in enumerate(da_o):
            mix_ref[0, :, c * LANES:(c + 1) * LANES] = _diff_subln(
                o, _chunk(subg, c), scal_ref[l, S_ONE_MINUS]).astype(BF16)

        band = pl.ds(PAST_LEN + start, SW_BAND)
        k_sel = jnp.concatenate([ksw[0:PAST_LEN, :], ksw[band, :]], axis=0)
        v_sel = jnp.concatenate([vsw[0:PAST_LEN, :], vsw[band, :]], axis=0)
        n_sel = PAST_LEN + SW_BAND
        col = lax.broadcasted_iota(jnp.int32, (TQ, n_sel), 1)
        row = lax.broadcasted_iota(jnp.int32, (TQ, n_sel), 0)
        dist = (q0 + row) - (start + col - PAST_LEN)
        valid = (col < PAST_LEN) | ((dist <= WINDOW) & (dist >= -WINDOW))
        sw_o = _swa_attention(q_ref[0, :, 256:768].astype(F32), lambda g: _chunk(k_sel, g),
                              lambda g: v_sel[:, 2 * g * LANES:(2 * g + 2) * LANES], sinks, shift, valid)
        for c, o in enumerate(sw_o):
            mix_ref[0, :, 256 + c * LANES:256 + (c + 1) * LANES] = o.astype(BF16)

        mla_o = _mla_attention(lambda h: q_ref[0, :, O_MQ + h * LANES:O_MQ + (h + 1) * LANES],
                               lambda h, rows: kml[rows, h * LANES:(h + 1) * LANES],
                               lambda h, rows: vml[rows, h * LANES:(h + 1) * LANES], shift, N_KEYS,
                               key_chunks=MLA_KEY_SLABS)
        for c, o in enumerate(mla_o):
            mix_ref[0, :, 768 + c * LANES:768 + (c + 1) * LANES] = o.astype(BF16)

    pl.when(small)(functools.partial(attend, False))
    pl.when(jnp.logical_not(small))(functools.partial(attend, True))


def _lat_attend(l_idx, q, k, v, qmax, caches, p, pad_eye):
    cdk, cdv, csk, csv, cckv, ckr = caches
    proj_tiles_per_req = DEC_SEQ // TM_LAT

    def cache_spec(a):
        return pl.BlockSpec((1, 1) + a.shape[2:], lambda b, j, l: (b, l[0], 0, 0))

    grid_spec = pltpu.PrefetchScalarGridSpec(
        num_scalar_prefetch=1,
        grid=(DEC_BATCH, DEC_SEQ // TQ),
        in_specs=[
            pl.BlockSpec((1, TQ, Q_W), lambda b, j, l: (b, j, 0)),
            pl.BlockSpec((1, DEC_SEQ, K_W), lambda b, j, l: (b, 0, 0)),
            pl.BlockSpec((1, DEC_SEQ, V_W), lambda b, j, l: (b, 0, 0)),
            cache_spec(cdk), cache_spec(cdv), cache_spec(csk), cache_spec(csv), cache_spec(cckv), cache_spec(ckr),
            _layer_spec(p["mla_w_kvb"].shape),
            pl.BlockSpec(pad_eye.shape, lambda b, j, l: (0, 0)),
            _layer_spec(p["lamv"].shape),
            _layer_spec(p["subg"].shape),
            pl.BlockSpec((1, 1, LANES), lambda b, j, l: (b * proj_tiles_per_req + (j * TQ) // TM_LAT, 0, 0)),
            pl.BlockSpec(p["gk"].shape, lambda b, j, l: (0, 0)),
            pl.BlockSpec(memory_space=pltpu.SMEM),
        ],
        out_specs=pl.BlockSpec((1, TQ, D_MODEL), lambda b, j, l: (b, j, 0)),
        scratch_shapes=[
            pltpu.VMEM((N_KEYS, 256), BF16), pltpu.VMEM((N_KEYS, 256), BF16),
            pltpu.VMEM((N_KEYS, SW_KV_HEADS * LANES), BF16), pltpu.VMEM((N_KEYS, SW_KV_HEADS * 2 * LANES), BF16),
            pltpu.VMEM((N_KEYS, MLA_CHUNKS), BF16), pltpu.VMEM((N_KEYS, MLA_CHUNKS), BF16),
            pltpu.VMEM((1, LANES), F32),
        ],
    )
    return pl.pallas_call(
        _lat_attn_kernel,
        grid_spec=grid_spec,
        out_shape=jax.ShapeDtypeStruct((DEC_BATCH, DEC_SEQ, D_MODEL), BF16),
        compiler_params=pltpu.CompilerParams(
            dimension_semantics=("arbitrary", "arbitrary"), vmem_limit_bytes=VMEM_LIMIT),
        name="latent_attend",
    )(l_idx, q.reshape(DEC_BATCH, DEC_SEQ, Q_W), k.reshape(DEC_BATCH, DEC_SEQ, K_W),
      v.reshape(DEC_BATCH, DEC_SEQ, V_W), cdk, cdv, csk, csv, cckv, ckr,
      p["mla_w_kvb"], pad_eye, p["lamv"], p["subg"], qmax, p["gk"], p["scal"])


def _mlp_kernel(l_ref, x_ref, mix_ref, mod_ref, n2g_ref, w_out_ref, w_up_ref, w_down_ref, fg_ref, o_ref,
                *, final_norm):
    m = mod_ref[0, 0]
    x = x_ref[...] + m[2:3] * jnp.dot(mix_ref[...], w_out_ref[0], preferred_element_type=F32)
    h = (_rms(x, n2g_ref[0]) * (1.0 + m[4:5]) + m[3:4]).astype(BF16)
    acc = jnp.zeros_like(x)
    for c in range(D_FF // FF_CHUNK):
        u = jnp.dot(h, w_up_ref[0, :, c * FF_CHUNK:(c + 1) * FF_CHUNK], preferred_element_type=F32)
        a = jnp.square(jnp.maximum(u, 0.0)).astype(BF16)
        acc = acc + jnp.dot(a, w_down_ref[0, c * FF_CHUNK:(c + 1) * FF_CHUNK, :], preferred_element_type=F32)
    x = x + m[5:6] * acc
    if final_norm:
        x = _rms(x, fg_ref[...])
    o_ref[...] = x


def _mlp_layer(l_idx, x, mix, mod, p, final_g, mod_row, final_norm):
    n_tok = x.shape[0]
    grid_spec = pltpu.PrefetchScalarGridSpec(
        num_scalar_prefetch=1,
        grid=(n_tok // TM_MLP,),
        in_specs=[
            pl.BlockSpec((TM_MLP, D_MODEL), lambda i, l: (i, 0)),
            pl.BlockSpec((TM_MLP, D_MODEL), lambda i, l: (i, 0)),
            pl.BlockSpec((1, 1, 6, D_MODEL), lambda i, l: (l[0], mod_row(i), 0, 0)),
            _layer_spec(p["norm2_g"].shape),
            _layer_spec(p["w_out"].shape),
            _layer_spec(p["w_up"].shape),
            _layer_spec(p["w_down"].shape),
            pl.BlockSpec((1, D_MODEL), lambda i, l: (0, 0)),
        ],
        out_specs=pl.BlockSpec((TM_MLP, D_MODEL), lambda i, l: (i, 0)),
    )
    return pl.pallas_call(
        functools.partial(_mlp_kernel, final_norm=final_norm),
        grid_spec=grid_spec,
        out_shape=jax.ShapeDtypeStruct((n_tok, D_MODEL), F32),
        compiler_params=pltpu.CompilerParams(
            dimension_semantics=("arbitrary",), vmem_limit_bytes=VMEM_LIMIT),
        name="out_proj_mlp",
    )(l_idx, x, mix, mod, p["norm2_g"], p["w_out"], p["w_up"], p["w_down"], final_g)


def _rope_tables(period, active_lanes=LANES):
    m = period // 2
    half = m // 2
    t = np.arange(DEC_SEQ)
    pos = np.stack([t // GRID_W, t % GRID_W], axis=0).astype(np.float64)
    lane = np.arange(LANES) % period
    axis = lane // m
    k = lane % m
    freq = ROPE_BASE ** (-(k % half).astype(np.float64) / half)
    ang = pos[axis, :].T * freq[None, :]
    first = (k < half)[None, :]
    active = (np.arange(LANES) < active_lanes)[None, :]
    cos, sin = np.where(active, np.cos(ang), 1.0), np.where(active, np.sin(ang), 0.0)
    return (jnp.asarray(cos, F32), jnp.asarray(np.where(first, -sin, 0.0), F32),
            jnp.asarray(np.where(first, 0.0, sin), F32))


def _group_matrices():
    gq = np.zeros((Q_W, LANES), np.float32)
    gk = np.zeros((K_W, LANES), np.float32)
    for d in range(DA_HEADS * 2 * DA_QK):
        gq[d, G_DIFF + d // DA_QK] = 1.0
        gk[d, G_DIFF + d // DA_QK] = 1.0
    for h in range(SW_HEADS):
        g = h // (SW_HEADS // SW_KV_HEADS)
        gq[256 + h * SW_DIM:256 + (h + 1) * SW_DIM, G_SWA + h] = 1.0
        gk[256 + g * SW_DIM:256 + (g + 1) * SW_DIM, G_SWA + h] = 1.0
    for h in range(MLA_HEADS):
        gq[O_MQ + h * LANES:O_MQ + (h + 1) * LANES, G_MLA + h] = 1.0
        gk[O_MK + h * LANES:O_MK + (h + 1) * LANES, G_MLA + h] = 1.0
    return jnp.asarray(gq, BF16), jnp.asarray(gk, BF16)


def _prepare_params(norm1_g, norm2_g, w_in, w_out, lq1, lk1, lq2, lk2, diff_subln_g, swa_sink,
                    mla_q_norm_g, mla_w_qb, mla_kv_norm_g, mla_w_kvb, w_up, w_down):
    w_in_ext = jnp.pad(w_in.astype(BF16), ((0, 0), (0, 0), (0, P_EXT - w_in.shape[-1])))
    w_out_p = w_out.astype(BF16)
    dq = MLA_NOPE + MLA_ROPE
    gap = LANES - dq
    zq = jnp.zeros((DEPTH, MLA_Q_RANK, gap), mla_w_qb.dtype)
    wqb = jnp.concatenate(
        [blk for h in range(MLA_HEADS)
         for blk in (mla_w_qb[:, :, h * dq + MLA_NOPE:(h + 1) * dq], zq, mla_w_qb[:, :, h * dq:h * dq + MLA_NOPE])],
        axis=-1).astype(BF16)
    dkv = MLA_NOPE + MLA_V
    zk = jnp.zeros((DEPTH, MLA_KV_RANK, LANES - MLA_NOPE), mla_w_kvb.dtype)
    wkvb = jnp.concatenate(
        [blk for h in range(MLA_HEADS) for blk in (zk, mla_w_kvb[:, :, h * dkv:h * dkv + MLA_NOPE])]
        + [mla_w_kvb[:, :, h * dkv + MLA_NOPE:(h + 1) * dkv] for h in range(MLA_HEADS)],
        axis=-1).astype(BF16)
    lam_init = np.array([0.8 - 0.6 * math.exp(-0.3 * l) for l in range(DEPTH)], np.float32)
    sink_l2 = swa_sink.astype(F32) * LOG2E
    scal = jnp.concatenate([
        sink_l2,
        jnp.asarray(np.stack([lam_init, (1.0 - lam_init.astype(np.float64)).astype(np.float32)], axis=1)),
        jnp.max(jnp.abs(sink_l2), axis=1, keepdims=True),
        jnp.zeros((DEPTH, S_COLS - SW_HEADS - 3), F32)], axis=1)
    gq, gk = _group_matrices()
    return {
        "gq": gq,
        "gk": gk,
        "norm1_g": norm1_g.reshape(DEPTH, 1, D_MODEL),
        "norm2_g": norm2_g.reshape(DEPTH, 1, D_MODEL),
        "w_in": w_in_ext,
        "w_out": w_out_p,
        "mla_q_norm_g": mla_q_norm_g.reshape(DEPTH, 1, MLA_Q_RANK),
        "mla_w_qb": wqb,
        "mla_kv_norm_g": mla_kv_norm_g.reshape(DEPTH, 1, MLA_KV_RANK),
        "mla_w_kvb": wkvb,
        "lamv": jnp.stack([lq1, lk1, lq2, lk2], axis=1),
        "subg": jnp.tile(diff_subln_g, (1, DA_HEADS)).reshape(DEPTH, 1, DA_HEADS * DA_V),
        "scal": scal,
        "w_up": w_up.astype(BF16),
        "w_down": w_down.astype(BF16),
    }


def kernel(x_prompt, x_sample, c, cache_diff_k, cache_diff_v, cache_swa_k, cache_swa_v, cache_mla_ckv,
           cache_mla_krope, c_ctx, norm1_g, norm2_g, w_ada, b_ada, w_in, w_out, diff_lambda_q1,
           diff_lambda_k1, diff_lambda_q2, diff_lambda_k2, diff_subln_g, swa_sink, mla_q_norm_g,
           mla_w_qb, mla_kv_norm_g, mla_w_kvb, w_up, w_down, final_g):
    p = _prepare_params(norm1_g, norm2_g, w_in, w_out, diff_lambda_q1, diff_lambda_k1, diff_lambda_q2,
                        diff_lambda_k2, diff_subln_g, swa_sink, mla_q_norm_g, mla_w_qb, mla_kv_norm_g,
                        mla_w_kvb, w_up, w_down)
    cvecs = jnp.concatenate([c, c_ctx[None, :], jnp.zeros((8 - DEC_BATCH - 1, D_MODEL), F32)], axis=0)
    mod = _modulation(cvecs, w_ada, b_ada)
    tables = _rope_tables(DA_QK) + _rope_tables(SW_DIM) + _rope_tables(MLA_ROPE, MLA_ROPE)
    pad_eye = jnp.asarray(np.eye(MLA_ROPE, LANES, dtype=np.float32), BF16)
    final_g2 = final_g.reshape(1, D_MODEL)
    caches = (cache_diff_k.reshape(DEC_BATCH, DEPTH, PAST_LEN, DA_HEADS * 2 * DA_QK),
              cache_diff_v.reshape(DEC_BATCH, DEPTH, PAST_LEN, DA_HEADS * DA_V),
              cache_swa_k.reshape(DEC_BATCH, DEPTH, PAST_LEN, SW_KV_HEADS * SW_DIM),
              cache_swa_v.reshape(DEC_BATCH, DEPTH, PAST_LEN, SW_KV_HEADS * SW_DIM),
              cache_mla_ckv, cache_mla_krope)

    x_ctx = x_prompt.reshape(N_CTX_TOK, D_MODEL)
    x_lat = x_sample.reshape(N_LAT_TOK, D_MODEL)
    new = None
    lat_tiles_per_req = DEC_SEQ // TM_MLP
    for l in range(DEPTH):
        l_idx = jnp.full((1,), l, jnp.int32)
        last = l == DEPTH - 1
        *new, mix_ctx = _ctx_layer(l_idx, x_ctx, mod, p, new)
        x_ctx = _mlp_layer(l_idx, x_ctx, mix_ctx, mod, p, final_g2, lambda i: CTX_MOD_ROW, last)
        q, k, v, qmax = _lat_project(l_idx, x_lat, mod, p, tables)
        mix_lat = _lat_attend(l_idx, q, k, v, qmax, caches, p, pad_eye).reshape(N_LAT_TOK, D_MODEL)
        x_lat = _mlp_layer(l_idx, x_lat, mix_lat, mod, p, final_g2, lambda i: i // lat_tiles_per_req, last)

    y_prompt = x_ctx.reshape(BATCH, SEQ, D_MODEL)
    y_sample = x_lat.reshape(DEC_BATCH, DEC_SEQ, D_MODEL)
    new_dk, new_dv, new_sk, new_sv, new_ckv, new_kr = new
    return (y_prompt, y_sample,
            new_dk.reshape(BATCH, DEPTH, SEQ, DA_HEADS, 2 * DA_QK),
            new_dv.reshape(BATCH, DEPTH, SEQ, DA_HEADS, DA_V),
            new_sk.reshape(BATCH, DEPTH, SEQ, SW_KV_HEADS, SW_DIM),
            new_sv.reshape(BATCH, DEPTH, SEQ, SW_KV_HEADS, SW_DIM),
            new_ckv, new_kr)
```
